```python
import math
import jax
import jax.numpy as jnp
from jax import lax
import numpy as np

D_MODEL = 1024
BATCH = 16
SEQ = 2048
DEPTH = 1
DEC_BATCH = 16
DEC_SEQ = 32
PAST_LEN = 4096

CHUNK = 64
Q_BLOCK = 128
A_HEADS = 8
A_DIM = 64
A_VDIM = 2 * A_DIM
B_HEADS = 8
B_DK = 128
B_DV = 128
CONV_W = 4
DELTA_BLOCK = 64
D_FF = -(-8 * D_MODEL // (3 * 256)) * 256
EPS = 1e-6

A_QK = A_HEADS * 2 * A_DIM
A_V = A_HEADS * A_VDIM
B_QK = B_HEADS * B_DK
B_V = B_HEADS * B_DV
CONV_CH = 2 * B_QK + B_V
IN_WIDTHS = (A_QK, A_QK, A_V, CONV_CH, B_V, B_HEADS, B_HEADS, D_MODEL, D_MODEL)
IN_SPLITS = tuple(sum(IN_WIDTHS[:i + 1]) for i in range(len(IN_WIDTHS) - 1))
D_IN = sum(IN_WIDTHS)

kernel_name = 'diffattn_gdn_gated_hybrid_stream_step'


def rms_norm(x, w):
    xf = x.astype(jnp.float32)
    y = xf * lax.rsqrt(jnp.mean(xf * xf, axis=-1, keepdims=True) + EPS)
    return (y * w.astype(jnp.float32)).astype(x.dtype)


def l2_norm(x):
    xf = x.astype(jnp.float32)
    return xf * lax.rsqrt(jnp.sum(xf * xf, axis=-1, keepdims=True) + EPS)


def diff_attn_core(q, k, v, lam, mask):
    s = jnp.einsum('bqhmd,bkhmd->bhmqk', q, k).astype(jnp.float32) * (A_DIM ** -0.5)
    if mask is not None:
        s = jnp.where(mask, s, -jnp.inf)
    p = jax.nn.softmax(s, axis=-1)
    a = p[:, :, 0] - lam * p[:, :, 1]
    return jnp.einsum('bhqk,bkhe->bqhe', a.astype(v.dtype), v)


def diff_attn_prompt(q, k, v, lam):
    bsz, seq = q.shape[:2]
    nb = seq // Q_BLOCK
    q_blocks = jnp.swapaxes(q.reshape(bsz, nb, Q_BLOCK, A_HEADS, 2, A_DIM), 0, 1)
    k_chunk = jnp.arange(seq) // CHUNK

    def one_block(args):
        q_blk, i = args
        q_chunk = (i * Q_BLOCK + jnp.arange(Q_BLOCK)) // CHUNK
        mask = k_chunk[None, :] <= q_chunk[:, None]
        return diff_attn_core(q_blk, k, v, lam, mask)

    o = lax.map(one_block, (q_blocks, jnp.arange(nb)))
    return jnp.swapaxes(o, 0, 1).reshape(bsz, seq, A_HEADS, A_VDIM)


def causal_conv(x, buf, w):
    seq = x.shape[1]
    xpad = jnp.concatenate([buf.astype(x.dtype), x], axis=1)
    y = xpad[:, 0:seq] * w[0]
    for j in range(1, CONV_W):
        y = y + xpad[:, j:j + seq] * w[j]
    return y, xpad[:, -(CONV_W - 1):]


def gated_delta_rule(q, k, v, g, beta, s0):
    bsz, seq, nh, dk = q.shape
    dv = v.shape[-1]
    c = DELTA_BLOCK if seq % DELTA_BLOCK == 0 else seq
    n = seq // c

    def blocks(t):
        t = t.astype(jnp.float32).reshape(bsz, n, c, nh, *t.shape[3:])
        return jnp.swapaxes(t, 2, 3)

    q = blocks(q) * (dk ** -0.5)
    k = blocks(k)
    v = blocks(v)
    g = blocks(g)
    beta = blocks(beta)
    G = jnp.cumsum(g, axis=-1)
    idx = jnp.arange(c)
    causal = idx[:, None] >= idx[None, :]
    strict = idx[:, None] > idx[None, :]
    decay = jnp.exp(jnp.where(causal, G[..., :, None] - G[..., None, :], -jnp.inf))
    kk = jnp.einsum('bnhid,bnhjd->bnhij', k, k)
    m = jnp.where(strict, beta[..., :, None] * kk * decay, 0.0)
    rhs = jnp.concatenate([v * beta[..., None], k * (beta * jnp.exp(G))[..., None]], axis=-1)
    sol = lax.linalg.triangular_solve(m, rhs, left_side=True, lower=True, unit_diagonal=True)
    u, w = sol[..., :dv], sol[..., dv:]
    a_loc = jnp.where(causal, jnp.einsum('bnhid,bnhjd->bnhij', q, k) * decay, 0.0)
    q_dec = q * jnp.exp(G)[..., None]
    k_tail = k * jnp.exp(G[..., -1:] - G)[..., None]
    g_tail = jnp.exp(G[..., -1])

    def step(s, xs):
        u_i, w_i, a_i, qd_i, kt_i, gt_i = xs
        v_new = u_i - jnp.einsum('bhcd,bhde->bhce', w_i, s)
        o_i = jnp.einsum('bhcd,bhde->bhce', qd_i, s) + jnp.einsum('bhij,bhje->bhie', a_i, v_new)
        s = s * gt_i[..., None, None] + jnp.einsum('bhcd,bhce->bhde', kt_i, v_new)
        return s, o_i

    xs = tuple(jnp.moveaxis(t, 1, 0) for t in (u, w, a_loc, q_dec, k_tail, g_tail))
    s_final, o = lax.scan(step, s0.astype(jnp.float32), xs)
    o = jnp.moveaxis(o, 0, 1)
    o = jnp.swapaxes(o, 2, 3).reshape(bsz, seq, nh, dv)
    return o, s_final


def hybrid_layer(x, past_k, past_v, s0, conv_buf, lam_init,
                 ln1_w, w_in, q_norm_w, k_norm_w, lambda_q1, lambda_k1, lambda_q2, lambda_k2,
                 subln_w, w_branch_a, conv_w, a_log, dt_bias, delta_norm_w, w_branch_b,
                 w_out, ln2_w, w_gate_up, w_down):
    f32 = jnp.float32
    bsz, seq, _ = x.shape
    h = rms_norm(x, ln1_w)
    z = h @ w_in
    aq, ak, av, dqkv, dgate, dbeta, dalpha, gate_a, gate_b = jnp.split(z, IN_SPLITS, axis=-1)

    aq = rms_norm(aq.reshape(bsz, seq, A_HEADS, 2, A_DIM), q_norm_w)
    ak = rms_norm(ak.reshape(bsz, seq, A_HEADS, 2, A_DIM), k_norm_w)
    av = av.reshape(bsz, seq, A_HEADS, A_VDIM)
    lam = (jnp.exp(jnp.dot(lambda_q1.astype(f32), lambda_k1.astype(f32)))
           - jnp.exp(jnp.dot(lambda_q2.astype(f32), lambda_k2.astype(f32))) + lam_init)
    if past_k is None:
        o_a = diff_attn_prompt(aq, ak, av, lam)
    else:
        keys = jnp.concatenate([past_k.astype(ak.dtype), ak], axis=1)
        vals = jnp.concatenate([past_v.astype(av.dtype), av], axis=1)
        o_a = diff_attn_core(aq, keys, vals, lam, None)
    o_a = rms_norm(o_a, subln_w) * (1.0 - lam_init)
    branch_a = o_a.reshape(bsz, seq, A_V) @ w_branch_a

    conv, new_conv = causal_conv(dqkv, conv_buf, conv_w)
    conv = jax.nn.silu(conv)
    dq, dk, dv = jnp.split(conv, (B_QK, 2 * B_QK), axis=-1)
    dq = l2_norm(dq.reshape(bsz, seq, B_HEADS, B_DK))
    dk = l2_norm(dk.reshape(bsz, seq, B_HEADS, B_DK))
    dv = dv.reshape(bsz, seq, B_HEADS, B_DV)
    beta = jax.nn.sigmoid(dbeta.astype(f32))
    g = -jnp.exp(a_log.astype(f32)) * jax.nn.softplus(dalpha.astype(f32) + dt_bias.astype(f32))
    o_b, s_new = gated_delta_rule(dq, dk, dv, g, beta, s0)
    o_b = rms_norm(o_b, delta_norm_w) * jax.nn.silu(dgate.reshape(bsz, seq, B_HEADS, B_DV).astype(f32))
    branch_b = o_b.astype(x.dtype).reshape(bsz, seq, B_V) @ w_branch_b

    merged = jax.nn.sigmoid(gate_a) * branch_a + jax.nn.sigmoid(gate_b) * branch_b
    x = x + merged @ w_out

    h2 = rms_norm(x, ln2_w)
    gg, uu = jnp.split(h2 @ w_gate_up, 2, axis=-1)
    x = x + (jax.nn.silu(gg) * uu) @ w_down
    return x, ak, av, s_new, new_conv


def setup_inputs(seed: int = 0) -> dict:
    key = jax.random.key(seed)
    ks = jax.random.split(key, 32)
    f32 = jnp.float32

    def nrm(k, shape, scale):
        return jax.random.normal(k, shape, f32) * scale

    def gain(k, shape):
        return 1.0 + 0.01 * jax.random.normal(k, shape, f32)

    dt = jnp.exp(jax.random.uniform(ks[20], (DEPTH, B_HEADS), f32, math.log(1e-3), math.log(1e-1)))
    return {
        'x_prompt': nrm(ks[0], (BATCH, SEQ, D_MODEL), 1.0),
        'x_sample': nrm(ks[1], (DEC_BATCH, DEC_SEQ, D_MODEL), 1.0),
        'cache_k': nrm(ks[2], (DEPTH, DEC_BATCH, PAST_LEN, A_HEADS, 2, A_DIM), 1.0),
        'cache_v': nrm(ks[3], (DEPTH, DEC_BATCH, PAST_LEN, A_HEADS, A_VDIM), 1.0),
        'state_delta': nrm(ks[4], (DEPTH, DEC_BATCH, B_HEADS, B_DK, B_DV), 0.05),
        'state_conv': nrm(ks[5], (DEPTH, DEC_BATCH, CONV_W - 1, CONV_CH), 1.0),
        'ln1_w': gain(ks[6], (DEPTH, D_MODEL)),
        'w_in': nrm(ks[7], (DEPTH, D_MODEL, D_IN), D_MODEL ** -0.5),
        'q_norm_w': gain(ks[8], (DEPTH, A_DIM)),
        'k_norm_w': gain(ks[9], (DEPTH, A_DIM)),
        'lambda_q1': nrm(ks[10], (DEPTH, A_DIM), 0.1),
        'lambda_k1': nrm(ks[11], (DEPTH, A_DIM), 0.1),
        'lambda_q2': nrm(ks[12], (DEPTH, A_DIM), 0.1),
        'lambda_k2': nrm(ks[13], (DEPTH, A_DIM), 0.1),
        'subln_w': gain(ks[14], (DEPTH, A_VDIM)),
        'w_branch_a': nrm(ks[15], (DEPTH, A_V, D_MODEL), A_V ** -0.5),
        'conv_w': nrm(ks[16], (DEPTH, CONV_W, CONV_CH), CONV_W ** -0.5),
        'a_log': jnp.log(jax.random.uniform(ks[17], (DEPTH, B_HEADS), f32, 1.0, 16.0)),
        'dt_bias': dt + jnp.log(-jnp.expm1(-dt)),
        'delta_norm_w': gain(ks[18], (DEPTH, B_DV)),
        'w_branch_b': nrm(ks[19], (DEPTH, B_V, D_MODEL), B_V ** -0.5),
        'w_out': nrm(ks[21], (DEPTH, D_MODEL, D_MODEL), D_MODEL ** -0.5),
        'ln2_w': gain(ks[22], (DEPTH, D_MODEL)),
        'w_gate_up': nrm(ks[23], (DEPTH, D_MODEL, 2 * D_FF), D_MODEL ** -0.5),
        'w_down': nrm(ks[24], (DEPTH, D_FF, D_MODEL), D_FF ** -0.5),
    }


def reference(x_prompt, x_sample, cache_k, cache_v, state_delta, state_conv,
              ln1_w, w_in, q_norm_w, k_norm_w, lambda_q1, lambda_k1, lambda_q2, lambda_k2,
              subln_w, w_branch_a, conv_w, a_log, dt_bias, delta_norm_w, w_branch_b,
              w_out, ln2_w, w_gate_up, w_down):
    xp, xs = x_prompt, x_sample
    pk, pv, pd, pc = [], [], [], []
    sk, sv, sd, sc = [], [], [], []
    for l in range(DEPTH):
        lam_init = 0.8 - 0.6 * math.exp(-0.3 * l)
        weights = (ln1_w[l], w_in[l], q_norm_w[l], k_norm_w[l], lambda_q1[l], lambda_k1[l],
                   lambda_q2[l], lambda_k2[l], subln_w[l], w_branch_a[l], conv_w[l], a_log[l],
                   dt_bias[l], delta_norm_w[l], w_branch_b[l], w_out[l], ln2_w[l],
                   w_gate_up[l], w_down[l])
        bp = xp.shape[0]
        zero_conv = jnp.zeros((bp, CONV_W - 1, CONV_CH), xp.dtype)
        zero_s = jnp.zeros((bp, B_HEADS, B_DK, B_DV), jnp.float32)
        xp, k_p, v_p, d_p, c_p = hybrid_layer(xp, None, None, zero_s, zero_conv, lam_init, *weights)
        xs, k_s, v_s, d_s, c_s = hybrid_layer(xs, cache_k[l], cache_v[l], state_delta[l],
                                              state_conv[l], lam_init, *weights)
        pk.append(k_p); pv.append(v_p); pd.append(d_p); pc.append(c_p)
        sk.append(k_s); sv.append(v_s); sd.append(d_s); sc.append(c_s)
    return (xp, xs, jnp.stack(pk), jnp.stack(pv), jnp.stack(pd), jnp.stack(pc),
            jnp.stack(sk), jnp.stack(sv), jnp.stack(sd), jnp.stack(sc))
```

```python
import functools
import math

import jax
import jax.numpy as jnp
from jax import lax
from jax.experimental import pallas as pl
from jax.experimental.pallas import tpu as pltpu

F32 = jnp.float32
BF16 = jnp.bfloat16

D_MODEL = 1024
CHUNK = 64
A_HEADS = 8
A_DIM = 64
A_VDIM = 2 * A_DIM
B_HEADS = 8
B_DK = 128
B_DV = 128
CONV_W = 4
DELTA_BLOCK = 64
D_FF = -(-8 * D_MODEL // (3 * 256)) * 256
EPS = 1e-6

A_QK = A_HEADS * 2 * A_DIM
A_V = A_HEADS * A_VDIM
B_QK = B_HEADS * B_DK
B_V = B_HEADS * B_DV
CONV_CH = 2 * B_QK + B_V

LANES = 128
SUBLANES = 8
NEG_BIG = -1e30
VMEM_LIMIT = 56 * 1024 * 1024


def _cparams(n_axes):
    return pltpu.CompilerParams(dimension_semantics=("arbitrary",) * n_axes,
                                vmem_limit_bytes=VMEM_LIMIT)


def _dot(a, b):
    return jnp.dot(a, b, preferred_element_type=F32)


def _dot_nt(a, b):
    return lax.dot_general(a, b, (((1,), (1,)), ((), ())), preferred_element_type=F32)


def _split(a):
    hi = a.astype(BF16)
    lo = (a - hi.astype(F32)).astype(BF16)
    return hi, lo


def _dot3(a, b):
    ah, al = _split(a)
    bh, bl = _split(b)
    return _dot(ah, bh) + (_dot(ah, bl) + _dot(al, bh))


def _rms_rows(x, w):
    ms = jnp.mean(x * x, axis=-1, keepdims=True)
    return x * lax.rsqrt(ms + EPS) * w


def _group_sumsq(z, gshift):
    r = lax.broadcasted_iota(jnp.int32, (LANES, LANES), 0) >> gshift
    c = lax.broadcasted_iota(jnp.int32, (LANES, LANES), 1) >> gshift
    blk = (r == c).astype(BF16)
    outs = []
    for s in range(z.shape[1] // LANES):
        zs = z[:, s * LANES:(s + 1) * LANES]
        hi, lo = _split(zs * zs)
        outs.append(_dot(hi, blk) + _dot(lo, blk))
    return jnp.concatenate(outs, axis=1)


def _silu(x):
    return x * jax.nn.sigmoid(x)


def _inproj_attn_kernel(x_ref, ln_ref, w_ref, qn_ref, kn_ref, q_ref, k_ref, v_ref, h_scr):
    j = pl.program_id(1)

    @pl.when(j == 0)
    def _():
        h_scr[...] = _rms_rows(x_ref[...], ln_ref[...]).astype(BF16)

    z = _dot(h_scr[...], w_ref[...])

    @pl.when(j == 0)
    def _():
        n = z * lax.rsqrt(_group_sumsq(z, 6) * (1.0 / A_DIM) + EPS) * qn_ref[...]
        q_ref[...] = (n * (A_DIM ** -0.5)).astype(BF16)

    @pl.when(j == 1)
    def _():
        k_ref[...] = z * lax.rsqrt(_group_sumsq(z, 6) * (1.0 / A_DIM) + EPS) * kn_ref[...]

    @pl.when(j == 2)
    def _():
        v_ref[...] = z


def _inproj_attn(x2, ln_w, w_qkv, qn_row, kn_row, tm):
    n = x2.shape[0]
    grid = (n // tm, 3)
    row = lambda i, j: (0, 0)
    return pl.pallas_call(
        _inproj_attn_kernel,
        grid=grid,
        in_specs=[
            pl.BlockSpec((tm, D_MODEL), lambda i, j: (i, 0)),
            pl.BlockSpec((1, D_MODEL), row),
            pl.BlockSpec((D_MODEL, A_QK), lambda i, j: (0, j)),
            pl.BlockSpec((1, A_QK), row),
            pl.BlockSpec((1, A_QK), row),
        ],
        out_specs=[
            pl.BlockSpec((tm, A_QK), lambda i, j: (i, 0)),
            pl.BlockSpec((tm, A_QK), lambda i, j: (i, 0)),
            pl.BlockSpec((tm, A_V), lambda i, j: (i, 0)),
        ],
        out_shape=[
            jax.ShapeDtypeStruct((n, A_QK), BF16),
            jax.ShapeDtypeStruct((n, A_QK), F32),
            jax.ShapeDtypeStruct((n, A_V), F32),
        ],
        scratch_shapes=[pltpu.VMEM((tm, D_MODEL), BF16)],
        compiler_params=_cparams(2),
    )(x2, ln_w, w_qkv, qn_row, kn_row)


def _inproj_delta_kernel(x_ref, ln_ref, w_ref, wbg_ref, cw_ref, cbuf_ref, alog_ref, dtb_ref,
                         dqkv_ref, bg_ref, cst_ref, h_scr, zbuf, carry, *, tiles_per_seq):
    i = pl.program_id(0)
    j = pl.program_id(1)
    tm = x_ref.shape[0]

    @pl.when(j == 0)
    def _():
        h = _rms_rows(x_ref[...], ln_ref[...]).astype(BF16)
        h_scr[...] = h
        zb = _dot(h, wbg_ref[...])
        lane = lax.broadcasted_iota(jnp.int32, zb.shape, 1)
        a = zb + dtb_ref[...]
        softplus = jnp.maximum(a, 0.0) + jnp.log1p(jnp.exp(-jnp.abs(a)))
        g = -jnp.exp(alog_ref[...]) * softplus
        bg_ref[...] = jnp.where(lane < B_HEADS, jax.nn.sigmoid(zb),
                                jnp.where(lane < 2 * B_HEADS, g, 0.0))

    z = _dot(h_scr[...], w_ref[...])
    first = (i % tiles_per_seq) == 0
    prev = jnp.where(first, cbuf_ref[0], carry[j])
    zbuf[0:SUBLANES, :] = prev
    zbuf[SUBLANES:, :] = z
    tail = z[tm - SUBLANES:, :]
    carry[j] = tail
    cst_ref[0] = tail

    cw = cw_ref[...]
    y = z * cw[CONV_W - 1:CONV_W, :]
    for s in range(1, CONV_W):
        y = y + zbuf[pl.ds(SUBLANES - s, tm), :] * cw[CONV_W - 1 - s:CONV_W - s, :]
    y = _silu(y)

    @pl.when(j < 2)
    def _():
        n = y * lax.rsqrt(_group_sumsq(y, 7) + EPS)
        dqkv_ref[...] = n * jnp.where(j == 0, B_DK ** -0.5, 1.0)

    @pl.when(j == 2)
    def _():
        dqkv_ref[...] = y


def _inproj_delta(x2, ln_w, w_d, w_bg, conv_w, cbuf8, alog_row, dtb_row, tm, seq_len):
    n = x2.shape[0]
    tiles_per_seq = seq_len // tm
    n_tiles = n // tm
    row = lambda i, j: (0, 0)
    kern = functools.partial(_inproj_delta_kernel, tiles_per_seq=tiles_per_seq)
    return pl.pallas_call(
        kern,
        grid=(n_tiles, 3),
        in_specs=[
            pl.BlockSpec((tm, D_MODEL), lambda i, j: (i, 0)),
            pl.BlockSpec((1, D_MODEL), row),
            pl.BlockSpec((D_MODEL, B_QK), lambda i, j: (0, j)),
            pl.BlockSpec((D_MODEL, LANES), row),
            pl.BlockSpec((CONV_W, B_QK), lambda i, j: (0, j)),
            pl.BlockSpec((1, SUBLANES, B_QK), lambda i, j: (i // tiles_per_seq, 0, j)),
            pl.BlockSpec((1, LANES), row),
            pl.BlockSpec((1, LANES), row),
        ],
        out_specs=[
            pl.BlockSpec((tm, B_QK), lambda i, j: (i, j)),
            pl.BlockSpec((tm, LANES), lambda i, j: (i, 0)),
            pl.BlockSpec((1, SUBLANES, B_QK), lambda i, j: (i, 0, j)),
        ],
        out_shape=[
            jax.ShapeDtypeStruct((n, CONV_CH), F32),
            jax.ShapeDtypeStruct((n, LANES), F32),
            jax.ShapeDtypeStruct((n_tiles, SUBLANES, CONV_CH), F32),
        ],
        scratch_shapes=[
            pltpu.VMEM((tm, D_MODEL), BF16),
            pltpu.VMEM((tm + SUBLANES, B_QK), F32),
            pltpu.VMEM((3, SUBLANES, B_QK), F32),
        ],
        compiler_params=_cparams(2),
    )(x2, ln_w, w_d, w_bg, conv_w, cbuf8, alog_row, dtb_row)


def _lambda_value(lq1, lk1, lq2, lk2, lam_init):
    d1 = jnp.sum(lq1[...] * lk1[...], axis=-1, keepdims=True)
    d2 = jnp.sum(lq2[...] * lk2[...], axis=-1, keepdims=True)
    return jnp.exp(d1) - jnp.exp(d2) + lam_init


def _split_maps(q):
    lane = lax.broadcasted_iota(jnp.int32, q.shape, 1)
    zero = jnp.zeros_like(q)
    return jnp.where(lane < A_DIM, q, zero), jnp.where(lane >= A_DIM, q, zero)


def _attn_finish(o, sw, lam_init):
    ms = jnp.mean(o * o, axis=-1, keepdims=True)
    return (o * lax.rsqrt(ms + EPS) * sw * (1.0 - lam_init)).astype(BF16)


def _attn_prompt_kernel(q_ref, k_ref, v_ref, lq1, lk1, lq2, lk2, sw_ref, o_ref, *, tq, lam_init):
    qi = pl.program_id(2)
    lam = _lambda_value(lq1, lk1, lq2, lk2, lam_init)
    qmaps = _split_maps(q_ref[0])
    ri = lax.broadcasted_iota(jnp.int32, (tq, tq), 0) // CHUNK
    ci = lax.broadcasted_iota(jnp.int32, (tq, tq), 1) // CHUNK
    diag_mask = ci <= ri

    def block(kj, carry, masked):
        start = pl.multiple_of(kj * tq, tq)
        kb = k_ref[0, pl.ds(start, tq), :].astype(BF16)
        vb = v_ref[0, pl.ds(start, tq), :].astype(BF16)
        new = []
        for m in range(2):
            m_old, l_old, acc = carry[m]
            s = _dot_nt(qmaps[m], kb)
            if masked:
                s = jnp.where(diag_mask, s, NEG_BIG)
            m_new = jnp.maximum(m_old, jnp.max(s, axis=-1, keepdims=True))
            alpha = jnp.exp(m_old - m_new)
            p = jnp.exp(s - m_new)
            l_new = alpha * l_old + jnp.sum(p, axis=-1, keepdims=True)
            acc = alpha * acc + _dot(p.astype(BF16), vb)
            new.append((m_new, l_new, acc))
        return tuple(new)

    init = tuple((jnp.full((tq, 1), NEG_BIG, F32), jnp.zeros((tq, 1), F32),
                  jnp.zeros((tq, A_VDIM), F32)) for _ in range(2))
    carry = lax.fori_loop(0, qi, lambda kj, c: block(kj, c, False), init)
    carry = block(qi, carry, True)
    o = carry[0][2] / carry[0][1] - lam * (carry[1][2] / carry[1][1])
    o_ref[0] = _attn_finish(o, sw_ref[...], lam_init)


def _attn_prompt(q3, k3, v3, lams, sw_row, lam_init, tq):
    b, l, _ = q3.shape
    vec = lambda bi, h, qi: (0, 0)
    kern = functools.partial(_attn_prompt_kernel, tq=tq, lam_init=lam_init)
    return pl.pallas_call(
        kern,
        grid=(b, A_HEADS, l // tq),
        in_specs=[
            pl.BlockSpec((1, tq, LANES), lambda bi, h, qi: (bi, qi, h)),
            pl.BlockSpec((1, l, LANES), lambda bi, h, qi: (bi, 0, h)),
            pl.BlockSpec((1, l, LANES), lambda bi, h, qi: (bi, 0, h)),
            pl.BlockSpec((1, A_DIM), vec), pl.BlockSpec((1, A_DIM), vec),
            pl.BlockSpec((1, A_DIM), vec), pl.BlockSpec((1, A_DIM), vec),
            pl.BlockSpec((1, A_VDIM), vec),
        ],
        out_specs=pl.BlockSpec((1, tq, LANES), lambda bi, h, qi: (bi, qi, h)),
        out_shape=jax.ShapeDtypeStruct((b, l, A_V), BF16),
        compiler_params=_cparams(3),
    )(q3, k3, v3, *lams, sw_row)


def _attn_sample_kernel(q_ref, kc_ref, vc_ref, kn_ref, vn_ref, lq1, lk1, lq2, lk2, sw_ref, o_ref,
                        *, lam_init):
    lam = _lambda_value(lq1, lk1, lq2, lk2, lam_init)
    qmaps = _split_maps(q_ref[0])
    kc = kc_ref[0].astype(BF16)
    vc = vc_ref[0].astype(BF16)
    kn = kn_ref[0].astype(BF16)
    vn = vn_ref[0].astype(BF16)
    outs = []
    for m in range(2):
        sc = _dot_nt(qmaps[m], kc)
        sn = _dot_nt(qmaps[m], kn)
        mx = jnp.maximum(jnp.max(sc, axis=-1, keepdims=True), jnp.max(sn, axis=-1, keepdims=True))
        pc = jnp.exp(sc - mx)
        pn = jnp.exp(sn - mx)
        den = jnp.sum(pc, axis=-1, keepdims=True) + jnp.sum(pn, axis=-1, keepdims=True)
        acc = _dot(pc.astype(BF16), vc) + _dot(pn.astype(BF16), vn)
        outs.append(acc / den)
    o_ref[0] = _attn_finish(outs[0] - lam * outs[1], sw_ref[...], lam_init)


def _attn_sample(q3, kc3, vc3, kn3, vn3, lams, sw_row, lam_init):
    b, l, _ = q3.shape
    past = kc3.shape[1]
    vec = lambda bi, h: (0, 0)
    blk = lambda rows: pl.BlockSpec((1, rows, LANES), lambda bi, h: (bi, 0, h))
    kern = functools.partial(_attn_sample_kernel, lam_init=lam_init)
    return pl.pallas_call(
        kern,
        grid=(b, A_HEADS),
        in_specs=[
            blk(l), blk(past), blk(past), blk(l), blk(l),
            pl.BlockSpec((1, A_DIM), vec), pl.BlockSpec((1, A_DIM), vec),
            pl.BlockSpec((1, A_DIM), vec), pl.BlockSpec((1, A_DIM), vec),
            pl.BlockSpec((1, A_VDIM), vec),
        ],
        out_specs=blk(l),
        out_shape=jax.ShapeDtypeStruct((b, l, A_V), BF16),
        compiler_params=_cparams(2),
    )(q3, kc3, vc3, kn3, vn3, *lams, sw_row)


def _unit_lower_inverse(m, c):
    ri = lax.broadcasted_iota(jnp.int32, (c, c), 0)
    ci = lax.broadcasted_iota(jnp.int32, (c, c), 1)
    p = jnp.where(ri == ci, 1.0, 0.0) - m
    q = m
    for _ in range(int(math.log2(c)) - 1):
        q = _dot3(q, q)
        p = p + _dot3(p, q)
    return p


def _delta_kernel(q_ref, k_ref, v_ref, bg_ref, s0_ref, nw_ref, o_ref, sout_ref, s_scr, *, c, nblk):
    h = pl.program_id(1)
    t = pl.program_id(2)

    @pl.when(t == 0)
    def _():
        s_scr[...] = s0_ref[0, 0]

    bg = bg_ref[0]
    lane = lax.broadcasted_iota(jnp.int32, bg.shape, 1)
    beta_col = jnp.sum(jnp.where(lane == h, bg, 0.0), axis=-1, keepdims=True)
    g_col = jnp.sum(jnp.where(lane == h + B_HEADS, bg, 0.0), axis=-1, keepdims=True)
    bg_t = bg.T
    sub = lax.broadcasted_iota(jnp.int32, bg_t.shape, 0)
    g_row = jnp.sum(jnp.where(sub == h + B_HEADS, bg_t, 0.0), axis=0, keepdims=True)

    ri = lax.broadcasted_iota(jnp.int32, (c, c), 0)
    ci = lax.broadcasted_iota(jnp.int32, (c, c), 1)
    causal = ri >= ci
    strict = ri > ci
    ltri = causal.astype(BF16)
    utri = (ri <= ci).astype(BF16)

    s = s_scr[...]
    for blk in range(nblk):
        rows = slice(blk * c, (blk + 1) * c)
        q = q_ref[0, rows, :]
        k = k_ref[0, rows, :]
        v = v_ref[0, rows, :]
        bc = beta_col[rows, :]
        ghi, glo = _split(jnp.broadcast_to(g_col[rows, :], (c, LANES)))
        g_cum = _dot(ltri, ghi) + _dot(ltri, glo)
        rhi, rlo = _split(jnp.broadcast_to(g_row[:, rows], (c, c)))
        g_cum_row = _dot(rhi, utri) + _dot(rlo, utri)
        diff = jnp.where(causal, g_cum[:, :c] - g_cum_row, 0.0)
        decay = jnp.where(causal, jnp.exp(diff), 0.0)
        kb = k.astype(BF16)
        m = jnp.where(strict, bc * _dot_nt(kb, kb) * decay, 0.0)
        tinv = _unit_lower_inverse(m, c)
        e_g = jnp.exp(g_cum)
        rhs = jnp.concatenate([v * bc, k * (bc * e_g)], axis=1)
        sol = _dot3(tinv, rhs)
        u = sol[:, :B_DV]
        w = sol[:, B_DV:]
        a_loc = jnp.where(causal, _dot_nt(q.astype(BF16), kb) * decay, 0.0)
        g_last = g_cum[c - 1:c, :]
        q_dec = q * e_g
        k_tail = k * jnp.exp(g_last - g_cum)
        sb = s.astype(BF16)
        v_new = u - _dot(w.astype(BF16), sb)
        vb = v_new.astype(BF16)
        o = _dot(q_dec.astype(BF16), sb) + _dot(a_loc.astype(BF16), vb)
        s = s * jnp.exp(g_last) + _dot(k_tail.T.astype(BF16), vb)
        o_ref[0, rows, :] = _rms_rows(o, nw_ref[...]).astype(BF16)
    s_scr[...] = s

    @pl.when(t == pl.num_programs(2) - 1)
    def _():
        sout_ref[0, 0] = s


def _delta(dqkv3, bg3, s0, nw_row, c, tl):
    b, l, _ = dqkv3.shape
    kern = functools.partial(_delta_kernel, c=c, nblk=tl // c)
    lane_blk = lambda off: pl.BlockSpec((1, tl, LANES), lambda bi, h, t: (bi, t, h + off))
    state = pl.BlockSpec((1, 1, B_DK, B_DV), lambda bi, h, t: (bi, h, 0, 0))
    return pl.pallas_call(
        kern,
        grid=(b, B_HEADS, l // tl),
        in_specs=[
            lane_blk(0), lane_blk(B_HEADS), lane_blk(2 * B_HEADS),
            pl.BlockSpec((1, tl, LANES), lambda bi, h, t: (bi, t, 0)),
            state,
            pl.BlockSpec((1, B_DV), lambda bi, h, t: (0, 0)),
        ],
        out_specs=[lane_blk(0), state],
        out_shape=[
            jax.ShapeDtypeStruct((b, l, B_V), BF16),
            jax.ShapeDtypeStruct((b, B_HEADS, B_DK, B_DV), F32),
        ],
        scratch_shapes=[pltpu.VMEM((B_DK, B_DV), F32)],
        compiler_params=_cparams(3),
    )(dqkv3, dqkv3, dqkv3, bg3, s0, nw_row)


def _merge_kernel(x_ref, oa_ref, ob_ref, ln_ref, wdg, wga, wgb, wba, wbb, wout, y_ref):
    x = x_ref[...]
    h = _rms_rows(x, ln_ref[...]).astype(BF16)
    ob = ob_ref[...].astype(F32) * _silu(_dot(h, wdg[...]))
    branch_b = _dot(ob.astype(BF16), wbb[...])
    branch_a = _dot(oa_ref[...], wba[...])
    merged = (jax.nn.sigmoid(_dot(h, wga[...])) * branch_a
              + jax.nn.sigmoid(_dot(h, wgb[...])) * branch_b)
    y_ref[...] = x + _dot(merged.astype(BF16), wout[...])


def _merge(x2, oa2, ob2, ln_w, wdg, wga, wgb, wba, wbb, wout, tm):
    n = x2.shape[0]
    tile = pl.BlockSpec((tm, D_MODEL), lambda i: (i, 0))
    wspec = pl.BlockSpec((D_MODEL, D_MODEL), lambda i: (0, 0))
    return pl.pallas_call(
        _merge_kernel,
        grid=(n // tm,),
        in_specs=[tile, tile, tile, pl.BlockSpec((1, D_MODEL), lambda i: (0, 0))] + [wspec] * 6,
        out_specs=tile,
        out_shape=jax.ShapeDtypeStruct((n, D_MODEL), F32),
        compiler_params=_cparams(1),
    )(x2, oa2, ob2, ln_w, wdg, wga, wgb, wba, wbb, wout)


def _ffn_kernel(x_ref, ln_ref, wg_ref, wu_ref, wd_ref, y_ref, h_scr):
    cidx = pl.program_id(1)

    @pl.when(cidx == 0)
    def _():
        x = x_ref[...]
        h_scr[...] = _rms_rows(x, ln_ref[...]).astype(BF16)
        y_ref[...] = x

    h = h_scr[...]
    act = _silu(_dot(h, wg_ref[...])) * _dot(h, wu_ref[...])
    y_ref[...] += _dot(act.astype(BF16), wd_ref[...])


def _ffn(x2, ln_w, w_gate_up, w_down, tm, n_chunks):
    n = x2.shape[0]
    fc = D_FF // n_chunks
    return pl.pallas_call(
        _ffn_kernel,
        grid=(n // tm, n_chunks),
        in_specs=[
            pl.BlockSpec((tm, D_MODEL), lambda i, c: (i, 0)),
            pl.BlockSpec((1, D_MODEL), lambda i, c: (0, 0)),
            pl.BlockSpec((D_MODEL, fc), lambda i, c: (0, c)),
            pl.BlockSpec((D_MODEL, fc), lambda i, c: (0, c + n_chunks)),
            pl.BlockSpec((fc, D_MODEL), lambda i, c: (c, 0)),
        ],
        out_specs=pl.BlockSpec((tm, D_MODEL), lambda i, c: (i, 0)),
        out_shape=jax.ShapeDtypeStruct((n, D_MODEL), F32),
        scratch_shapes=[pltpu.VMEM((tm, D_MODEL), BF16)],
        compiler_params=_cparams(2),
    )(x2, ln_w, w_gate_up, w_gate_up, w_down)


def _pick_tile(n, target):
    t = min(n, target)
    while n % t:
        t //= 2
    return t


def _prep_weights(ln1_w, w_in, q_norm_w, k_norm_w, lambda_q1, lambda_k1, lambda_q2, lambda_k2,
                  subln_w, w_branch_a, conv_w, a_log, dt_bias, delta_norm_w, w_branch_b,
                  w_out, ln2_w, w_gate_up, w_down):
    o_d = 2 * A_QK + A_V
    o_dg = o_d + CONV_CH
    o_bg = o_dg + B_V
    o_ga = o_bg + 2 * B_HEADS
    o_gb = o_ga + D_MODEL
    pad_lanes = lambda v: jnp.pad(v.astype(F32), (B_HEADS, LANES - 2 * B_HEADS))[None, :]
    return dict(
        ln1=ln1_w[None, :], ln2=ln2_w[None, :],
        w_qkv=w_in[:, :o_d].astype(BF16),
        w_d=w_in[:, o_d:o_dg].astype(BF16),
        w_dg=w_in[:, o_dg:o_bg].astype(BF16),
        w_bg=jnp.pad(w_in[:, o_bg:o_ga], ((0, 0), (0, LANES - 2 * B_HEADS))).astype(BF16),
        w_ga=w_in[:, o_ga:o_gb].astype(BF16),
        w_gb=w_in[:, o_gb:].astype(BF16),
        qn=jnp.tile(q_norm_w, A_QK // A_DIM)[None, :],
        kn=jnp.tile(k_norm_w, A_QK // A_DIM)[None, :],
        lams=tuple(v[None, :] for v in (lambda_q1, lambda_k1, lambda_q2, lambda_k2)),
        subln=subln_w[None, :],
        w_ba=w_branch_a.astype(BF16), w_bb=w_branch_b.astype(BF16),
        conv_w=conv_w, alog=pad_lanes(a_log), dtb=pad_lanes(dt_bias),
        dnorm=delta_norm_w[None, :],
        w_out=w_out.astype(BF16),
        w_gate_up=w_gate_up.astype(BF16), w_down=w_down.astype(BF16),
    )


def _layer(x, past_k, past_v, s0, conv_buf, lam_init, p):
    b, l, _ = x.shape
    n = b * l
    x2 = x.reshape(n, D_MODEL)

    q2, k2, v2 = _inproj_attn(x2, p["ln1"], p["w_qkv"], p["qn"], p["kn"], _pick_tile(n, 512))

    tm_d = _pick_tile(l, 512)
    cbuf8 = jnp.pad(conv_buf, ((0, 0), (SUBLANES - (CONV_W - 1), 0), (0, 0)))
    dqkv, bg, cst = _inproj_delta(x2, p["ln1"], p["w_d"], p["w_bg"], p["conv_w"], cbuf8,
                                  p["alog"], p["dtb"], tm_d, l)
    tiles_per_seq = l // tm_d
    new_conv = cst[tiles_per_seq - 1::tiles_per_seq, SUBLANES - (CONV_W - 1):, :]

    q3 = q2.reshape(b, l, A_QK)
    k3 = k2.reshape(b, l, A_QK)
    v3 = v2.reshape(b, l, A_V)
    if past_k is None:
        oa = _attn_prompt(q3, k3, v3, p["lams"], p["subln"], lam_init, _pick_tile(l, 256))
    else:
        past = past_k.shape[1]
        oa = _attn_sample(q3, past_k.reshape(b, past, A_QK), past_v.reshape(b, past, A_V),
                          k3, v3, p["lams"], p["subln"], lam_init)

    c = DELTA_BLOCK if l % DELTA_BLOCK == 0 else l
    tl = c * max(1, min(4, l // c))
    ob, s_new = _delta(dqkv.reshape(b, l, CONV_CH), bg.reshape(b, l, LANES), s0, p["dnorm"], c, tl)

    x1 = _merge(x2, oa.reshape(n, A_V), ob.reshape(n, B_V), p["ln1"], p["w_dg"], p["w_ga"],
                p["w_gb"], p["w_ba"], p["w_bb"], p["w_out"], _pick_tile(n, 256))
    y = _ffn(x1, p["ln2"], p["w_gate_up"], p["w_down"], _pick_tile(n, 512), 2)
    return (y.reshape(b, l, D_MODEL), k3.reshape(b, l, A_HEADS, 2, A_DIM),
            v3.reshape(b, l, A_HEADS, A_VDIM), s_new, new_conv)


def kernel(x_prompt, x_sample, cache_k, cache_v, state_delta, state_conv, ln1_w, w_in, q_norm_w,
           k_norm_w, lambda_q1, lambda_k1, lambda_q2, lambda_k2, subln_w, w_branch_a, conv_w,
           a_log, dt_bias, delta_norm_w, w_branch_b, w_out, ln2_w, w_gate_up, w_down):
    depth = ln1_w.shape[0]
    xp, xs = x_prompt, x_sample
    outs_p, outs_s = [], []
    for layer in range(depth):
        lam_init = 0.8 - 0.6 * math.exp(-0.3 * layer)
        p = _prep_weights(*(w[layer] for w in (
            ln1_w, w_in, q_norm_w, k_norm_w, lambda_q1, lambda_k1, lambda_q2, lambda_k2, subln_w,
            w_branch_a, conv_w, a_log, dt_bias, delta_norm_w, w_branch_b, w_out, ln2_w,
            w_gate_up, w_down)))
        bp = xp.shape[0]
        zero_conv = jnp.zeros((bp, CONV_W - 1, CONV_CH), xp.dtype)
        zero_s = jnp.zeros((bp, B_HEADS, B_DK, B_DV), F32)
        xp, *rest_p = _layer(xp, None, None, zero_s, zero_conv, lam_init, p)
        xs, *rest_s = _layer(xs, cache_k[layer], cache_v[layer], state_delta[layer],
                             state_conv[layer], lam_init, p)
        outs_p.append(rest_p)
        outs_s.append(rest_s)
    stack = lambda outs, idx: jnp.stack([o[idx] for o in outs])
    return (xp, xs,
            stack(outs_p, 0), stack(outs_p, 1), stack(outs_p, 2), stack(outs_p, 3),
            stack(outs_s, 0), stack(outs_s, 1), stack(outs_s, 2), stack(outs_s, 3))
```

```python
import functools
import math

import jax
import jax.numpy as jnp
from jax import lax
from jax.experimental import pallas as pl
from jax.experimental.pallas import tpu as pltpu

F32 = jnp.float32
BF16 = jnp.bfloat16

D_MODEL = 1024
CHUNK = 64
A_HEADS = 8
A_DIM = 64
A_VDIM = 2 * A_DIM
B_HEADS = 8
B_DK = 128
B_DV = 128
CONV_W = 4
DELTA_BLOCK = 64
D_FF = -(-8 * D_MODEL // (3 * 256)) * 256
EPS = 1e-6

A_QK = A_HEADS * 2 * A_DIM
A_V = A_HEADS * A_VDIM
B_QK = B_HEADS * B_DK
B_V = B_HEADS * B_DV
CONV_CH = 2 * B_QK + B_V

LANES = 128
SUBLANES = 8
NEG_BIG = -1e30
VMEM_LIMIT = 56 * 1024 * 1024


def _cparams(n_axes):
    return pltpu.CompilerParams(dimension_semantics=("arbitrary",) * n_axes,
                                vmem_limit_bytes=VMEM_LIMIT)


def _dot(a, b):
    return jnp.dot(a, b, preferred_element_type=F32)


def _dot_nt(a, b):
    return lax.dot_general(a, b, (((1,), (1,)), ((), ())), preferred_element_type=F32)


def _split(a):
    hi = a.astype(BF16)
    lo = (a - hi.astype(F32)).astype(BF16)
    return hi, lo


def _rms_rows(x, w):
    ms = jnp.mean(x * x, axis=-1, keepdims=True)
    return x * lax.rsqrt(ms + EPS) * w


def _group_sumsq(z, gshift):
    r = lax.broadcasted_iota(jnp.int32, (LANES, LANES), 0) >> gshift
    c = lax.broadcasted_iota(jnp.int32, (LANES, LANES), 1) >> gshift
    blk = (r == c).astype(BF16)
    outs = []
    for s in range(z.shape[1] // LANES):
        zs = z[:, s * LANES:(s + 1) * LANES]
        hi, lo = _split(zs * zs)
        outs.append(_dot(hi, blk) + _dot(lo, blk))
    return jnp.concatenate(outs, axis=1)


def _silu(x):
    return x * jax.nn.sigmoid(x)


def _inproj_attn_kernel(x_ref, ln_ref, w_ref, qn_ref, kn_ref, q_ref, k_ref, v_ref, h_scr):
    j = pl.program_id(1)

    @pl.when(j == 0)
    def _():
        h_scr[...] = _rms_rows(x_ref[...], ln_ref[...]).astype(BF16)

    z = _dot(h_scr[...], w_ref[...])

    @pl.when(j == 0)
    def _():
        n = z * lax.rsqrt(_group_sumsq(z, 6) * (1.0 / A_DIM) + EPS) * qn_ref[...]
        q_ref[...] = (n * (A_DIM ** -0.5)).astype(BF16)

    @pl.when(j == 1)
    def _():
        k_ref[...] = z * lax.rsqrt(_group_sumsq(z, 6) * (1.0 / A_DIM) + EPS) * kn_ref[...]

    @pl.when(j == 2)
    def _():
        v_ref[...] = z


def _inproj_attn(x2, ln_w, w_qkv, qn_row, kn_row, tm):
    n = x2.shape[0]
    grid = (n // tm, 3)
    row = lambda i, j: (0, 0)
    return pl.pallas_call(
        _inproj_attn_kernel,
        grid=grid,
        in_specs=[
            pl.BlockSpec((tm, D_MODEL), lambda i, j: (i, 0)),
            pl.BlockSpec((1, D_MODEL), row),
            pl.BlockSpec((D_MODEL, A_QK), lambda i, j: (0, j)),
            pl.BlockSpec((1, A_QK), row),
            pl.BlockSpec((1, A_QK), row),
        ],
        out_specs=[
            pl.BlockSpec((tm, A_QK), lambda i, j: (i, 0)),
            pl.BlockSpec((tm, A_QK), lambda i, j: (i, 0)),
            pl.BlockSpec((tm, A_V), lambda i, j: (i, 0)),
        ],
        out_shape=[
            jax.ShapeDtypeStruct((n, A_QK), BF16),
            jax.ShapeDtypeStruct((n, A_QK), F32),
            jax.ShapeDtypeStruct((n, A_V), F32),
        ],
        scratch_shapes=[pltpu.VMEM((tm, D_MODEL), BF16)],
        compiler_params=_cparams(2),
        name="inproj_attn",
    )(x2, ln_w, w_qkv, qn_row, kn_row)


def _inproj_delta_kernel(x_ref, ln_ref, w_ref, wbg_ref, cw_ref, cbuf_ref, alog_ref, dtb_ref,
                         dqkv_ref, bg_ref, cst_ref, h_scr, zbuf, carry, *, tiles_per_seq):
    i = pl.program_id(0)
    j = pl.program_id(1)
    tm = x_ref.shape[0]

    @pl.when(j == 0)
    def _():
        h = _rms_rows(x_ref[...], ln_ref[...]).astype(BF16)
        h_scr[...] = h
        zb = _dot(h, wbg_ref[...])
        lane = lax.broadcasted_iota(jnp.int32, zb.shape, 1)
        a = zb + dtb_ref[...]
        softplus = jnp.maximum(a, 0.0) + jnp.log1p(jnp.exp(-jnp.abs(a)))
        g = -jnp.exp(alog_ref[...]) * softplus
        bg_ref[...] = jnp.where(lane < B_HEADS, jax.nn.sigmoid(zb),
                                jnp.where(lane < 2 * B_HEADS, g, 0.0))

    z = _dot(h_scr[...], w_ref[...])
    first = (i % tiles_per_seq) == 0
    prev = jnp.where(first, cbuf_ref[0], carry[j])
    zbuf[0:SUBLANES, :] = prev
    zbuf[SUBLANES:, :] = z
    tail = z[tm - SUBLANES:, :]
    carry[j] = tail
    cst_ref[0] = tail

    cw = cw_ref[...]
    y = z * cw[CONV_W - 1:CONV_W, :]
    for s in range(1, CONV_W):
        y = y + zbuf[pl.ds(SUBLANES - s, tm), :] * cw[CONV_W - 1 - s:CONV_W - s, :]
    y = _silu(y)

    @pl.when(j < 2)
    def _():
        n = y * lax.rsqrt(_group_sumsq(y, 7) + EPS)
        dqkv_ref[...] = n * jnp.where(j == 0, B_DK ** -0.5, 1.0)

    @pl.when(j == 2)
    def _():
        dqkv_ref[...] = y


def _inproj_delta(x2, ln_w, w_d, w_bg, conv_w, cbuf8, alog_row, dtb_row, tm, seq_len):
    n = x2.shape[0]
    tiles_per_seq = seq_len // tm
    n_tiles = n // tm
    row = lambda i, j: (0, 0)
    kern = functools.partial(_inproj_delta_kernel, tiles_per_seq=tiles_per_seq)
    return pl.pallas_call(
        kern,
        grid=(n_tiles, 3),
        in_specs=[
            pl.BlockSpec((tm, D_MODEL), lambda i, j: (i, 0)),
            pl.BlockSpec((1, D_MODEL), row),
            pl.BlockSpec((D_MODEL, B_QK), lambda i, j: (0, j)),
            pl.BlockSpec((D_MODEL, LANES), row),
            pl.BlockSpec((CONV_W, B_QK), lambda i, j: (0, j)),
            pl.BlockSpec((1, SUBLANES, B_QK), lambda i, j: (i // tiles_per_seq, 0, j)),
            pl.BlockSpec((1, LANES), row),
            pl.BlockSpec((1, LANES), row),
        ],
        out_specs=[
            pl.BlockSpec((tm, B_QK), lambda i, j: (i, j)),
            pl.BlockSpec((tm, LANES), lambda i, j: (i, 0)),
            pl.BlockSpec((1, SUBLANES, B_QK), lambda i, j: (i, 0, j)),
        ],
        out_shape=[
            jax.ShapeDtypeStruct((n, CONV_CH), F32),
            jax.ShapeDtypeStruct((n, LANES), F32),
            jax.ShapeDtypeStruct((n_tiles, SUBLANES, CONV_CH), F32),
        ],
        scratch_shapes=[
            pltpu.VMEM((tm, D_MODEL), BF16),
            pltpu.VMEM((tm + SUBLANES, B_QK), F32),
            pltpu.VMEM((3, SUBLANES, B_QK), F32),
        ],
        compiler_params=_cparams(2),
        name="inproj_delta",
    )(x2, ln_w, w_d, w_bg, conv_w, cbuf8, alog_row, dtb_row)


def _lambda_value(lq1, lk1, lq2, lk2, lam_init):
    d1 = jnp.sum(lq1[...] * lk1[...], axis=-1, keepdims=True)
    d2 = jnp.sum(lq2[...] * lk2[...], axis=-1, keepdims=True)
    return jnp.exp(d1) - jnp.exp(d2) + lam_init


def _split_maps(q):
    lane = lax.broadcasted_iota(jnp.int32, q.shape, 1)
    zero = jnp.zeros_like(q)
    return jnp.where(lane < A_DIM, q, zero), jnp.where(lane >= A_DIM, q, zero)


def _attn_finish(o, sw, lam_init):
    ms = jnp.mean(o * o, axis=-1, keepdims=True)
    return (o * lax.rsqrt(ms + EPS) * sw * (1.0 - lam_init)).astype(BF16)


def _attn_prompt_kernel(q_ref, k_ref, v_ref, lq1, lk1, lq2, lk2, sw_ref, o_ref,
                        m_scr, l_scr, acc_scr, *, tq, lam_init):
    qi = pl.program_id(2)
    lam = _lambda_value(lq1, lk1, lq2, lk2, lam_init)
    q2 = jnp.concatenate(_split_maps(q_ref[0]), axis=0)
    ri = lax.broadcasted_iota(jnp.int32, (2 * tq, tq), 0)
    ci = lax.broadcasted_iota(jnp.int32, (2 * tq, tq), 1)
    diag_mask = (ci // CHUNK) <= ((ri & (tq - 1)) // CHUNK)
    n_half = tq // LANES

    def scores(kj, masked):
        start = pl.multiple_of(kj * tq, tq)
        s = _dot_nt(q2, k_ref[0, pl.ds(start, tq), :].astype(BF16))
        if masked:
            s = jnp.where(diag_mask, s, NEG_BIG)
        return [s[:, i * LANES:(i + 1) * LANES] for i in range(n_half)]

    def lane_max(kj, masked):
        return functools.reduce(jnp.maximum, scores(kj, masked))

    m_scr[...] = jnp.full(m_scr.shape, NEG_BIG, F32)

    def pass1(kj, carry):
        m_scr[...] = jnp.maximum(m_scr[...], lane_max(kj, False))
        return carry

    lax.fori_loop(0, qi, pass1, 0)
    m_lane = jnp.maximum(m_scr[...], lane_max(qi, True))
    m_scr[...] = jnp.broadcast_to(jnp.max(m_lane, axis=-1, keepdims=True), m_scr.shape)
    l_scr[...] = jnp.zeros(l_scr.shape, F32)
    acc_scr[...] = jnp.zeros(acc_scr.shape, F32)

    def accumulate(kj, masked):
        start = pl.multiple_of(kj * tq, tq)
        m_full = m_scr[...]
        ps = [jnp.exp(sh - m_full) for sh in scores(kj, masked)]
        l_scr[...] += functools.reduce(jnp.add, ps)
        p = jnp.concatenate([ph.astype(BF16) for ph in ps], axis=1)
        acc_scr[...] += _dot(p, v_ref[0, pl.ds(start, tq), :].astype(BF16))

    def pass2(kj, carry):
        accumulate(kj, False)
        return carry

    lax.fori_loop(0, qi, pass2, 0)
    accumulate(qi, True)
    den = jnp.sum(l_scr[...], axis=-1, keepdims=True)
    o2 = acc_scr[...] / den
    o_ref[0] = _attn_finish(o2[:tq] - lam * o2[tq:], sw_ref[...], lam_init)


def _attn_prompt(q3, k3, v3, lams, sw_row, lam_init, tq):
    b, l, _ = q3.shape
    vec = lambda bi, h, qi: (0, 0)
    kern = functools.partial(_attn_prompt_kernel, tq=tq, lam_init=lam_init)
    return pl.pallas_call(
        kern,
        grid=(b, A_HEADS, l // tq),
        in_specs=[
            pl.BlockSpec((1, tq, LANES), lambda bi, h, qi: (bi, qi, h)),
            pl.BlockSpec((1, l, LANES), lambda bi, h, qi: (bi, 0, h)),
            pl.BlockSpec((1, l, LANES), lambda bi, h, qi: (bi, 0, h)),
            pl.BlockSpec((1, A_DIM), vec), pl.BlockSpec((1, A_DIM), vec),
            pl.BlockSpec((1, A_DIM), vec), pl.BlockSpec((1, A_DIM), vec),
            pl.BlockSpec((1, A_VDIM), vec),
        ],
        out_specs=pl.BlockSpec((1, tq, LANES), lambda bi, h, qi: (bi, qi, h)),
        out_shape=jax.ShapeDtypeStruct((b, l, A_V), BF16),
        scratch_shapes=[pltpu.VMEM((2 * tq, LANES), F32)] * 3,
        compiler_params=_cparams(3),
        name="attn_prompt",
    )(q3, k3, v3, *lams, sw_row)


def _attn_sample_kernel(q_ref, kc_ref, vc_ref, kn_ref, vn_ref, lq1, lk1, lq2, lk2, sw_ref, o_ref,
                        *, lam_init):
    lam = _lambda_value(lq1, lk1, lq2, lk2, lam_init)
    qmaps = _split_maps(q_ref[0])
    kc = kc_ref[0].astype(BF16)
    vc = vc_ref[0].astype(BF16)
    kn = kn_ref[0].astype(BF16)
    vn = vn_ref[0].astype(BF16)
    outs = []
    for m in range(2):
        sc = _dot_nt(qmaps[m], kc)
        sn = _dot_nt(qmaps[m], kn)
        mx = jnp.maximum(jnp.max(sc, axis=-1, keepdims=True), jnp.max(sn, axis=-1, keepdims=True))
        pc = jnp.exp(sc - mx)
        pn = jnp.exp(sn - mx)
        den = jnp.sum(pc, axis=-1, keepdims=True) + jnp.sum(pn, axis=-1, keepdims=True)
        acc = _dot(pc.astype(BF16), vc) + _dot(pn.astype(BF16), vn)
        outs.append(acc / den)
    o_ref[0] = _attn_finish(outs[0] - lam * outs[1], sw_ref[...], lam_init)


def _attn_sample(q3, kc3, vc3, kn3, vn3, lams, sw_row, lam_init):
    b, l, _ = q3.shape
    past = kc3.shape[1]
    vec = lambda bi, h: (0, 0)
    blk = lambda rows: pl.BlockSpec((1, rows, LANES), lambda bi, h: (bi, 0, h))
    kern = functools.partial(_attn_sample_kernel, lam_init=lam_init)
    return pl.pallas_call(
        kern,
        grid=(b, A_HEADS),
        in_specs=[
            blk(l), blk(past), blk(past), blk(l), blk(l),
            pl.BlockSpec((1, A_DIM), vec), pl.BlockSpec((1, A_DIM), vec),
            pl.BlockSpec((1, A_DIM), vec), pl.BlockSpec((1, A_DIM), vec),
            pl.BlockSpec((1, A_VDIM), vec),
        ],
        out_specs=blk(l),
        out_shape=jax.ShapeDtypeStruct((b, l, A_V), BF16),
        compiler_params=_cparams(2),
        name="attn_sample",
    )(q3, kc3, vc3, kn3, vn3, *lams, sw_row)


def _delta_kernel(q_ref, k_ref, v_ref, bg_ref, s0_ref, nw_ref, o_ref, sout_ref, s_scr,
                  *, c, nblk, hb):
    t = pl.program_id(2)

    @pl.when(t == 0)
    def _():
        s_scr[...] = s0_ref[0]

    bg = bg_ref[0]
    lane = lax.broadcasted_iota(jnp.int32, bg.shape, 1)
    bg_t = bg.T
    sub = lax.broadcasted_iota(jnp.int32, bg_t.shape, 0)
    ri = lax.broadcasted_iota(jnp.int32, (c, c), 0)
    ci = lax.broadcasted_iota(jnp.int32, (c, c), 1)
    causal = ri >= ci
    strict = ri > ci
    ltri = causal.astype(BF16)
    utri = (ri <= ci).astype(BF16)
    eye = jnp.where(ri == ci, 1.0, 0.0)

    items = [(hh, blk) for hh in range(hb) for blk in range(nblk)]
    pre = {}
    gates = []
    for hh in range(hb):
        h = pl.program_id(1) * hb + hh
        gates.append((
            jnp.sum(jnp.where(lane == h, bg, 0.0), axis=-1, keepdims=True),
            jnp.sum(jnp.where(lane == h + B_HEADS, bg, 0.0), axis=-1, keepdims=True),
            jnp.sum(jnp.where(sub == h + B_HEADS, bg_t, 0.0), axis=0, keepdims=True)))
    for hh, blk in items:
        rows = slice(blk * c, (blk + 1) * c)
        cols = slice(hh * LANES, (hh + 1) * LANES)
        bc = gates[hh][0][rows, :]
        g_col = gates[hh][1][rows, :]
        g_row = gates[hh][2][:, rows]
        q = q_ref[0, rows, cols]
        k = k_ref[0, rows, cols]
        v = v_ref[0, rows, cols]
        ghi, glo = _split(jnp.broadcast_to(g_col, (c, LANES)))
        g_cum = _dot(ltri, ghi) + _dot(ltri, glo)
        rhi, rlo = _split(jnp.broadcast_to(g_row, (c, c)))
        g_cum_row = _dot(rhi, utri) + _dot(rlo, utri)
        diff = jnp.where(causal, g_cum[:, :c] - g_cum_row, 0.0)
        decay = jnp.where(causal, jnp.exp(diff), 0.0)
        kb = k.astype(BF16)
        m = jnp.where(strict, bc * _dot_nt(kb, kb) * decay, 0.0)
        pre[hh, blk] = dict(q=q, k=k, v=v, bc=bc, g_cum=g_cum, decay=decay, kb=kb, m=m)

    pmat = {it: eye - pre[it]["m"] for it in items}
    qmat = {it: pre[it]["m"] for it in items}
    for _ in range(int(math.log2(c)) - 1):
        qb = {it: qmat[it].astype(BF16) for it in items}
        qmat = {it: _dot(qb[it], qb[it]) for it in items}
        pmat = {it: pmat[it] + _dot(pmat[it].astype(BF16), qmat[it].astype(BF16)) for it in items}

    for it in items:
        d = pre[it]
        e_g = jnp.exp(d["g_cum"])
        rhs = jnp.concatenate([d["v"] * d["bc"], d["k"] * (d["bc"] * e_g)], axis=1)
        sol = _dot(pmat[it].astype(BF16), rhs.astype(BF16))
        g_last = d["g_cum"][c - 1:c, :]
        d.update(
            u=sol[:, :B_DV], wb=sol[:, B_DV:].astype(BF16),
            a_loc=jnp.where(causal, _dot_nt(d["q"].astype(BF16), d["kb"]) * d["decay"],
                            0.0).astype(BF16),
            q_dec=(d["q"] * e_g).astype(BF16),
            k_tail_t=(d["k"] * jnp.exp(g_last - d["g_cum"])).T.astype(BF16),
            g_tail=jnp.exp(g_last))

    states = [s_scr[hh] for hh in range(hb)]
    outs = {}
    for blk in range(nblk):
        for hh in range(hb):
            d = pre[hh, blk]
            sb = states[hh].astype(BF16)
            vb = (d["u"] - _dot(d["wb"], sb)).astype(BF16)
            outs[hh, blk] = _dot(d["q_dec"], sb) + _dot(d["a_loc"], vb)
            states[hh] = states[hh] * d["g_tail"] + _dot(d["k_tail_t"], vb)

    for hh, blk in items:
        o_ref[0, blk * c:(blk + 1) * c, hh * LANES:(hh + 1) * LANES] = (
            _rms_rows(outs[hh, blk], nw_ref[...]).astype(BF16))
    for hh in range(hb):
        s_scr[hh] = states[hh]

    @pl.when(t == pl.num_programs(2) - 1)
    def _():
        for hh in range(hb):
            sout_ref[0, hh] = states[hh]


def _delta(dqkv3, bg3, s0, nw_row, c, tl, hb):
    b, l, _ = dqkv3.shape
    n_hg = B_HEADS // hb
    kern = functools.partial(_delta_kernel, c=c, nblk=tl // c, hb=hb)
    lane_blk = lambda off: pl.BlockSpec((1, tl, hb * LANES), lambda bi, h, t: (bi, t, h + off))
    state = pl.BlockSpec((1, hb, B_DK, B_DV), lambda bi, h, t: (bi, h, 0, 0))
    return pl.pallas_call(
        kern,
        grid=(b, n_hg, l // tl),
        in_specs=[
            lane_blk(0), lane_blk(n_hg), lane_blk(2 * n_hg),
            pl.BlockSpec((1, tl, LANES), lambda bi, h, t: (bi, t, 0)),
            state,
            pl.BlockSpec((1, B_DV), lambda bi, h, t: (0, 0)),
        ],
        out_specs=[lane_blk(0), state],
        out_shape=[
            jax.ShapeDtypeStruct((b, l, B_V), BF16),
            jax.ShapeDtypeStruct((b, B_HEADS, B_DK, B_DV), F32),
        ],
        scratch_shapes=[pltpu.VMEM((hb, B_DK, B_DV), F32)],
        compiler_params=_cparams(3),
        name="delta_rule",
    )(dqkv3, dqkv3, dqkv3, bg3, s0, nw_row)


def _merge_kernel(x_ref, oa_ref, ob_ref, ln_ref, wdg, wga, wgb, wba, wbb, wout, y_ref):
    x = x_ref[...]
    h = _rms_rows(x, ln_ref[...]).astype(BF16)
    ob = ob_ref[...].astype(F32) * _silu(_dot(h, wdg[...]))
    branch_b = _dot(ob.astype(BF16), wbb[...])
    branch_a = _dot(oa_ref[...], wba[...])
    merged = (jax.nn.sigmoid(_dot(h, wga[...])) * branch_a
              + jax.nn.sigmoid(_dot(h, wgb[...])) * branch_b)
    y_ref[...] = x + _dot(merged.astype(BF16), wout[...])


def _merge(x2, oa2, ob2, ln_w, wdg, wga, wgb, wba, wbb, wout, tm):
    n = x2.shape[0]
    tile = pl.BlockSpec((tm, D_MODEL), lambda i: (i, 0))
    wspec = pl.BlockSpec((D_MODEL, D_MODEL), lambda i: (0, 0))
    return pl.pallas_call(
        _merge_kernel,
        grid=(n // tm,),
        in_specs=[tile, tile, tile, pl.BlockSpec((1, D_MODEL), lambda i: (0, 0))] + [wspec] * 6,
        out_specs=tile,
        out_shape=jax.ShapeDtypeStruct((n, D_MODEL), F32),
        compiler_params=_cparams(1),
        name="merge_outproj",
    )(x2, oa2, ob2, ln_w, wdg, wga, wgb, wba, wbb, wout)


def _ffn_kernel(x_ref, ln_ref, wg_ref, wu_ref, wd_ref, y_ref, h_scr):
    cidx = pl.program_id(1)

    @pl.when(cidx == 0)
    def _():
        x = x_ref[...]
        h_scr[...] = _rms_rows(x, ln_ref[...]).astype(BF16)
        y_ref[...] = x

    h = h_scr[...]
    act = _silu(_dot(h, wg_ref[...])) * _dot(h, wu_ref[...])
    y_ref[...] += _dot(act.astype(BF16), wd_ref[...])


def _ffn(x2, ln_w, w_gate_up, w_down, tm, n_chunks):
    n = x2.shape[0]
    fc = D_FF // n_chunks
    return pl.pallas_call(
        _ffn_kernel,
        grid=(n // tm, n_chunks),
        in_specs=[
            pl.BlockSpec((tm, D_MODEL), lambda i, c: (i, 0)),
            pl.BlockSpec((1, D_MODEL), lambda i, c: (0, 0)),
            pl.BlockSpec((D_MODEL, fc), lambda i, c: (0, c)),
            pl.BlockSpec((D_MODEL, fc), lambda i, c: (0, c + n_chunks)),
            pl.BlockSpec((fc, D_MODEL), lambda i, c: (c, 0)),
        ],
        out_specs=pl.BlockSpec((tm, D_MODEL), lambda i, c: (i, 0)),
        out_shape=jax.ShapeDtypeStruct((n, D_MODEL), F32),
        scratch_shapes=[pltpu.VMEM((tm, D_MODEL), BF16)],
        compiler_params=_cparams(2),
        name="swiglu_ffn",
    )(x2, ln_w, w_gate_up, w_gate_up, w_down)


def _pick_tile(n, target):
    t = min(n, target)
    while n % t:
        t //= 2
    return t


def _prep_weights(ln1_w, w_in, q_norm_w, k_norm_w, lambda_q1, lambda_k1, lambda_q2, lambda_k2,
                  subln_w, w_branch_a, conv_w, a_log, dt_bias, delta_norm_w, w_branch_b,
                  w_out, ln2_w, w_gate_up, w_down):
    o_d = 2 * A_QK + A_V
    o_dg = o_d + CONV_CH
    o_bg = o_dg + B_V
    o_ga = o_bg + 2 * B_HEADS
    o_gb = o_ga + D_MODEL
    pad_lanes = lambda v: jnp.pad(v.astype(F32), (B_HEADS, LANES - 2 * B_HEADS))[None, :]
    return dict(
        ln1=ln1_w[None, :], ln2=ln2_w[None, :],
        w_qkv=w_in[:, :o_d].astype(BF16),
        w_d=w_in[:, o_d:o_dg].astype(BF16),
        w_dg=w_in[:, o_dg:o_bg].astype(BF16),
        w_bg=jnp.pad(w_in[:, o_bg:o_ga], ((0, 0), (0, LANES - 2 * B_HEADS))).astype(BF16),
        w_ga=w_in[:, o_ga:o_gb].astype(BF16),
        w_gb=w_in[:, o_gb:].astype(BF16),
        qn=jnp.tile(q_norm_w, A_QK // A_DIM)[None, :],
        kn=jnp.tile(k_norm_w, A_QK // A_DIM)[None, :],
        lams=tuple(v[None, :] for v in (lambda_q1, lambda_k1, lambda_q2, lambda_k2)),
        subln=subln_w[None, :],
        w_ba=w_branch_a.astype(BF16), w_bb=w_branch_b.astype(BF16),
        conv_w=conv_w, alog=pad_lanes(a_log), dtb=pad_lanes(dt_bias),
        dnorm=delta_norm_w[None, :],
        w_out=w_out.astype(BF16),
        w_gate_up=w_gate_up.astype(BF16), w_down=w_down.astype(BF16),
    )


def _layer(x, past_k, past_v, s0, conv_buf, lam_init, p):
    b, l, _ = x.shape
    n = b * l
    x2 = x.reshape(n, D_MODEL)

    q2, k2, v2 = _inproj_attn(x2, p["ln1"], p["w_qkv"], p["qn"], p["kn"], _pick_tile(n, 512))

    tm_d = _pick_tile(l, 512)
    cbuf8 = jnp.pad(conv_buf, ((0, 0), (SUBLANES - (CONV_W - 1), 0), (0, 0)))
    dqkv, bg, cst = _inproj_delta(x2, p["ln1"], p["w_d"], p["w_bg"], p["conv_w"], cbuf8,
                                  p["alog"], p["dtb"], tm_d, l)
    tiles_per_seq = l // tm_d
    new_conv = cst[tiles_per_seq - 1::tiles_per_seq, SUBLANES - (CONV_W - 1):, :]

    q3 = q2.reshape(b, l, A_QK)
    k3 = k2.reshape(b, l, A_QK)
    v3 = v2.reshape(b, l, A_V)
    if past_k is None:
        oa = _attn_prompt(q3, k3, v3, p["lams"], p["subln"], lam_init, _pick_tile(l, 256))
    else:
        past = past_k.shape[1]
        oa = _attn_sample(q3, past_k.reshape(b, past, A_QK), past_v.reshape(b, past, A_V),
                          k3, v3, p["lams"], p["subln"], lam_init)

    c = DELTA_BLOCK if l % DELTA_BLOCK == 0 else l
    tl = c * max(1, min(4, l // c))
    ob, s_new = _delta(dqkv.reshape(b, l, CONV_CH), bg.reshape(b, l, LANES), s0, p["dnorm"], c, tl, 4)

    x1 = _merge(x2, oa.reshape(n, A_V), ob.reshape(n, B_V), p["ln1"], p["w_dg"], p["w_ga"],
                p["w_gb"], p["w_ba"], p["w_bb"], p["w_out"], _pick_tile(n, 256))
    y = _ffn(x1, p["ln2"], p["w_gate_up"], p["w_down"], _pick_tile(n, 512), 2)
    return (y.reshape(b, l, D_MODEL), k3.reshape(b, l, A_HEADS, 2, A_DIM),
            v3.reshape(b, l, A_HEADS, A_VDIM), s_new, new_conv)


def kernel(x_prompt, x_sample, cache_k, cache_v, state_delta, state_conv, ln1_w, w_in, q_norm_w,
           k_norm_w, lambda_q1, lambda_k1, lambda_q2, lambda_k2, subln_w, w_branch_a, conv_w,
           a_log, dt_bias, delta_norm_w, w_branch_b, w_out, ln2_w, w_gate_up, w_down):
    depth = ln1_w.shape[0]
    xp, xs = x_prompt, x_sample
    outs_p, outs_s = [], []
    for layer in range(depth):
        lam_init = 0.8 - 0.6 * math.exp(-0.3 * layer)
        p = _prep_weights(*(w[layer] for w in (
            ln1_w, w_in, q_norm_w, k_norm_w, lambda_q1, lambda_k1, lambda_q2, lambda_k2, subln_w,
            w_branch_a, conv_w, a_log, dt_bias, delta_norm_w, w_branch_b, w_out, ln2_w,
            w_gate_up, w_down)))
        bp = xp.shape[0]
        zero_conv = jnp.zeros((bp, CONV_W - 1, CONV_CH), xp.dtype)
        zero_s = jnp.zeros((bp, B_HEADS, B_DK, B_DV), F32)
        xp, *rest_p = _layer(xp, None, None, zero_s, zero_conv, lam_init, p)
        xs, *rest_s = _layer(xs, cache_k[layer], cache_v[layer], state_delta[layer],
                             state_conv[layer], lam_init, p)
        outs_p.append(rest_p)
        outs_s.append(rest_s)
    stack = lambda outs, idx: jnp.stack([o[idx] for o in outs])
    return (xp, xs,
            stack(outs_p, 0), stack(outs_p, 1), stack(outs_p, 2), stack(outs_p, 3),
            stack(outs_s, 0), stack(outs_s, 1), stack(outs_s, 2), stack(outs_s, 3))
```

```python
import functools
import math

import jax
import jax.numpy as jnp
from jax import lax
from jax.experimental import pallas as pl
from jax.experimental.pallas import tpu as pltpu

F32 = jnp.float32
BF16 = jnp.bfloat16

D_MODEL = 1024
CHUNK = 64
A_HEADS = 8
A_DIM = 64
A_VDIM = 2 * A_DIM
B_HEADS = 8
B_DK = 128
B_DV = 128
CONV_W = 4
DELTA_BLOCK = 64
D_FF = -(-8 * D_MODEL // (3 * 256)) * 256
EPS = 1e-6

A_QK = A_HEADS * 2 * A_DIM
A_V = A_HEADS * A_VDIM
B_QK = B_HEADS * B_DK
B_V = B_HEADS * B_DV
CONV_CH = 2 * B_QK + B_V

LANES = 128
SUBLANES = 8
MXU_DIM = 256
NEG_BIG = -1e30
VMEM_LIMIT = 56 * 1024 * 1024


def _cparams(n_axes):
    return pltpu.CompilerParams(dimension_semantics=("arbitrary",) * n_axes,
                                vmem_limit_bytes=VMEM_LIMIT)


def _dot(a, b):
    return jnp.dot(a, b, preferred_element_type=F32)


def _dot_nt(a, b):
    return lax.dot_general(a, b, (((1,), (1,)), ((), ())), preferred_element_type=F32)


def _split(a):
    hi = a.astype(BF16)
    lo = (a - hi.astype(F32)).astype(BF16)
    return hi, lo


def _rms_rows(x, w):
    ms = jnp.mean(x * x, axis=-1, keepdims=True)
    return x * lax.rsqrt(ms + EPS) * w


def _group_sumsq(z, gshift):
    r = lax.broadcasted_iota(jnp.int32, (MXU_DIM, MXU_DIM), 0) >> gshift
    c = lax.broadcasted_iota(jnp.int32, (MXU_DIM, MXU_DIM), 1) >> gshift
    blk = (r == c).astype(BF16)
    outs = []
    for s in range(z.shape[1] // MXU_DIM):
        zs = z[:, s * MXU_DIM:(s + 1) * MXU_DIM]
        outs.append(_dot((zs * zs).astype(BF16), blk))
    return jnp.concatenate(outs, axis=1)


def _silu(x):
    return x * jax.nn.sigmoid(x)


def _inproj_attn_kernel(x_ref, ln_ref, w_ref, qn_ref, kn_ref, q_ref, k_ref, v_ref, h_scr):
    j = pl.program_id(1)

    @pl.when(j == 0)
    def _():
        h_scr[...] = _rms_rows(x_ref[...], ln_ref[...]).astype(BF16)

    z = _dot(h_scr[...], w_ref[...])

    @pl.when(j == 0)
    def _():
        n = z * lax.rsqrt(_group_sumsq(z, 6) * (1.0 / A_DIM) + EPS) * qn_ref[...]
        q_ref[...] = (n * (A_DIM ** -0.5)).astype(BF16)

    @pl.when(j == 1)
    def _():
        k_ref[...] = z * lax.rsqrt(_group_sumsq(z, 6) * (1.0 / A_DIM) + EPS) * kn_ref[...]

    @pl.when(j == 2)
    def _():
        v_ref[...] = z


def _inproj_attn(x2, ln_w, w_qkv, qn_row, kn_row, tm):
    n = x2.shape[0]
    grid = (n // tm, 3)
    row = lambda i, j: (0, 0)
    return pl.pallas_call(
        _inproj_attn_kernel,
        grid=grid,
        in_specs=[
            pl.BlockSpec((tm, D_MODEL), lambda i, j: (i, 0)),
            pl.BlockSpec((1, D_MODEL), row),
            pl.BlockSpec((D_MODEL, A_QK), lambda i, j: (0, j)),
            pl.BlockSpec((1, A_QK), row),
            pl.BlockSpec((1, A_QK), row),
        ],
        out_specs=[
            pl.BlockSpec((tm, A_QK), lambda i, j: (i, 0)),
            pl.BlockSpec((tm, A_QK), lambda i, j: (i, 0)),
            pl.BlockSpec((tm, A_V), lambda i, j: (i, 0)),
        ],
        out_shape=[
            jax.ShapeDtypeStruct((n, A_QK), BF16),
            jax.ShapeDtypeStruct((n, A_QK), F32),
            jax.ShapeDtypeStruct((n, A_V), F32),
        ],
        scratch_shapes=[pltpu.VMEM((tm, D_MODEL), BF16)],
        compiler_params=_cparams(2),
        name="inproj_attn",
    )(x2, ln_w, w_qkv, qn_row, kn_row)


def _inproj_delta_kernel(x_ref, ln_ref, w_ref, wbg_ref, cw_ref, cbuf_ref, alog_ref, dtb_ref,
                         dqkv_ref, bg_ref, cst_ref, h_scr, zbuf, carry, *, tiles_per_seq):
    i = pl.program_id(0)
    j = pl.program_id(1)
    tm = x_ref.shape[0]

    @pl.when(j == 0)
    def _():
        h = _rms_rows(x_ref[...], ln_ref[...]).astype(BF16)
        h_scr[...] = h
        zb = _dot(h, wbg_ref[...])
        lane = lax.broadcasted_iota(jnp.int32, zb.shape, 1)
        a = zb + dtb_ref[...]
        softplus = jnp.maximum(a, 0.0) + jnp.log1p(jnp.exp(-jnp.abs(a)))
        g = -jnp.exp(alog_ref[...]) * softplus
        bg_ref[...] = jnp.where(lane < B_HEADS, jax.nn.sigmoid(zb),
                                jnp.where(lane < 2 * B_HEADS, g, 0.0))

    z = _dot(h_scr[...], w_ref[...])
    first = (i % tiles_per_seq) == 0
    prev = jnp.where(first, cbuf_ref[0], carry[j])
    zbuf[0:SUBLANES, :] = prev
    zbuf[SUBLANES:, :] = z
    tail = z[tm - SUBLANES:, :]
    carry[j] = tail
    cst_ref[0] = tail

    cw = cw_ref[...]
    y = z * cw[CONV_W - 1:CONV_W, :]
    for s in range(1, CONV_W):
        y = y + zbuf[pl.ds(SUBLANES - s, tm), :] * cw[CONV_W - 1 - s:CONV_W - s, :]
    y = _silu(y)

    @pl.when(j < 2)
    def _():
        n = y * lax.rsqrt(_group_sumsq(y, 7) + EPS)
        dqkv_ref[...] = n * jnp.where(j == 0, B_DK ** -0.5, 1.0)

    @pl.when(j == 2)
    def _():
        dqkv_ref[...] = y


def _inproj_delta(x2, ln_w, w_d, w_bg, conv_w, cbuf8, alog_row, dtb_row, tm, seq_len):
    n = x2.shape[0]
    tiles_per_seq = seq_len // tm
    n_tiles = n // tm
    row = lambda i, j: (0, 0)
    kern = functools.partial(_inproj_delta_kernel, tiles_per_seq=tiles_per_seq)
    return pl.pallas_call(
        kern,
        grid=(n_tiles, 3),
        in_specs=[
            pl.BlockSpec((tm, D_MODEL), lambda i, j: (i, 0)),
            pl.BlockSpec((1, D_MODEL), row),
            pl.BlockSpec((D_MODEL, B_QK), lambda i, j: (0, j)),
            pl.BlockSpec((D_MODEL, LANES), row),
            pl.BlockSpec((CONV_W, B_QK), lambda i, j: (0, j)),
            pl.BlockSpec((1, SUBLANES, B_QK), lambda i, j: (i // tiles_per_seq, 0, j)),
            pl.BlockSpec((1, LANES), row),
            pl.BlockSpec((1, LANES), row),
        ],
        out_specs=[
            pl.BlockSpec((tm, B_QK), lambda i, j: (i, j)),
            pl.BlockSpec((tm, LANES), lambda i, j: (i, 0)),
            pl.BlockSpec((1, SUBLANES, B_QK), lambda i, j: (i, 0, j)),
        ],
        out_shape=[
            jax.ShapeDtypeStruct((n, CONV_CH), F32),
            jax.ShapeDtypeStruct((n, LANES), F32),
            jax.ShapeDtypeStruct((n_tiles, SUBLANES, CONV_CH), F32),
        ],
        scratch_shapes=[
            pltpu.VMEM((tm, D_MODEL), BF16),
            pltpu.VMEM((tm + SUBLANES, B_QK), F32),
            pltpu.VMEM((3, SUBLANES, B_QK), F32),
        ],
        compiler_params=_cparams(2),
        name="inproj_delta",
    )(x2, ln_w, w_d, w_bg, conv_w, cbuf8, alog_row, dtb_row)


def _lambda_value(lq1, lk1, lq2, lk2, lam_init):
    d1 = jnp.sum(lq1[...] * lk1[...], axis=-1, keepdims=True)
    d2 = jnp.sum(lq2[...] * lk2[...], axis=-1, keepdims=True)
    return jnp.exp(d1) - jnp.exp(d2) + lam_init


def _split_maps(q):
    lane = lax.broadcasted_iota(jnp.int32, q.shape, 1)
    zero = jnp.zeros_like(q)
    return jnp.where(lane < A_DIM, q, zero), jnp.where(lane >= A_DIM, q, zero)


def _attn_finish(o, sw, lam_init):
    ms = jnp.mean(o * o, axis=-1, keepdims=True)
    return (o * lax.rsqrt(ms + EPS) * sw * (1.0 - lam_init)).astype(BF16)


def _attn_prompt_kernel(q_ref, k_ref, v_ref, lq1, lk1, lq2, lk2, sw_ref, o_ref,
                        kb_scr, vb_scr, s_scr, p_scr, *, tq, lam_init):
    seq = q_ref.shape[1]
    lam = _lambda_value(lq1, lk1, lq2, lk2, lam_init)
    kb_scr[...] = k_ref[0].astype(BF16)
    vb_scr[...] = v_ref[0].astype(BF16)
    ri = lax.broadcasted_iota(jnp.int32, (2 * tq, tq), 0)
    ci = lax.broadcasted_iota(jnp.int32, (2 * tq, tq), 1)
    diag_mask = (ci // CHUNK) <= ((ri & (tq - 1)) // CHUNK)
    n_half = tq // LANES
    for qi in range(seq // tq):
        n_keys = (qi + 1) * tq
        q2 = jnp.concatenate(_split_maps(q_ref[0, qi * tq:(qi + 1) * tq, :]), axis=0)
        m_lane = None
        for j in range(qi + 1):
            s = _dot_nt(q2, kb_scr[j * tq:(j + 1) * tq, :])
            if j == qi:
                s = jnp.where(diag_mask, s, NEG_BIG)
            s_scr[:, j * tq:(j + 1) * tq] = s
            for i in range(n_half):
                half = s[:, i * LANES:(i + 1) * LANES]
                m_lane = half if m_lane is None else jnp.maximum(m_lane, half)
        m_full = jnp.broadcast_to(jnp.max(m_lane, axis=-1, keepdims=True), (2 * tq, LANES))
        l_lane = jnp.zeros((2 * tq, LANES), F32)
        for j in range(qi + 1):
            for i in range(n_half):
                cols = slice(j * tq + i * LANES, j * tq + (i + 1) * LANES)
                p = jnp.exp(s_scr[:, cols] - m_full)
                l_lane = l_lane + p
                p_scr[:, cols] = p.astype(BF16)
        acc = _dot(p_scr[:, :n_keys], vb_scr[:n_keys, :])
        o2 = acc / jnp.sum(l_lane, axis=-1, keepdims=True)
        o_ref[0, qi * tq:(qi + 1) * tq, :] = _attn_finish(o2[:tq] - lam * o2[tq:], sw_ref[...],
                                                          lam_init)


def _attn_prompt(q3, k3, v3, lams, sw_row, lam_init, tq):
    b, l, _ = q3.shape
    vec = lambda bi, h: (0, 0)
    head = pl.BlockSpec((1, l, LANES), lambda bi, h: (bi, 0, h))
    kern = functools.partial(_attn_prompt_kernel, tq=tq, lam_init=lam_init)
    return pl.pallas_call(
        kern,
        grid=(b, A_HEADS),
        in_specs=[
            head, head, head,
            pl.BlockSpec((1, A_DIM), vec), pl.BlockSpec((1, A_DIM), vec),
            pl.BlockSpec((1, A_DIM), vec), pl.BlockSpec((1, A_DIM), vec),
            pl.BlockSpec((1, A_VDIM), vec),
        ],
        out_specs=head,
        out_shape=jax.ShapeDtypeStruct((b, l, A_V), BF16),
        scratch_shapes=[
            pltpu.VMEM((l, LANES), BF16), pltpu.VMEM((l, LANES), BF16),
            pltpu.VMEM((2 * tq, l), F32), pltpu.VMEM((2 * tq, l), BF16),
        ],
        compiler_params=_cparams(2),
        name="attn_prompt",
    )(q3, k3, v3, *lams, sw_row)


def _attn_sample_kernel(q_ref, kc_ref, vc_ref, kn_ref, vn_ref, lq1, lk1, lq2, lk2, sw_ref, o_ref,
                        *, lam_init):
    lam = _lambda_value(lq1, lk1, lq2, lk2, lam_init)
    qmaps = _split_maps(q_ref[0])
    kc = kc_ref[0].astype(BF16)
    vc = vc_ref[0].astype(BF16)
    kn = kn_ref[0].astype(BF16)
    vn = vn_ref[0].astype(BF16)
    outs = []
    for m in range(2):
        sc = _dot_nt(qmaps[m], kc)
        sn = _dot_nt(qmaps[m], kn)
        mx = jnp.maximum(jnp.max(sc, axis=-1, keepdims=True), jnp.max(sn, axis=-1, keepdims=True))
        pc = jnp.exp(sc - mx)
        pn = jnp.exp(sn - mx)
        den = jnp.sum(pc, axis=-1, keepdims=True) + jnp.sum(pn, axis=-1, keepdims=True)
        acc = _dot(pc.astype(BF16), vc) + _dot(pn.astype(BF16), vn)
        outs.append(acc / den)
    o_ref[0] = _attn_finish(outs[0] - lam * outs[1], sw_ref[...], lam_init)


def _attn_sample(q3, kc3, vc3, kn3, vn3, lams, sw_row, lam_init):
    b, l, _ = q3.shape
    past = kc3.shape[1]
    vec = lambda bi, h: (0, 0)
    blk = lambda rows: pl.BlockSpec((1, rows, LANES), lambda bi, h: (bi, 0, h))
    kern = functools.partial(_attn_sample_kernel, lam_init=lam_init)
    return pl.pallas_call(
        kern,
        grid=(b, A_HEADS),
        in_specs=[
            blk(l), blk(past), blk(past), blk(l), blk(l),
            pl.BlockSpec((1, A_DIM), vec), pl.BlockSpec((1, A_DIM), vec),
            pl.BlockSpec((1, A_DIM), vec), pl.BlockSpec((1, A_DIM), vec),
            pl.BlockSpec((1, A_VDIM), vec),
        ],
        out_specs=blk(l),
        out_shape=jax.ShapeDtypeStruct((b, l, A_V), BF16),
        compiler_params=_cparams(2),
        name="attn_sample",
    )(q3, kc3, vc3, kn3, vn3, *lams, sw_row)


def _delta_kernel(q_ref, k_ref, v_ref, bg_ref, s0_ref, nw_ref, o_ref, sout_ref, s_scr,
                  *, c, nblk, hb):
    t = pl.program_id(2)

    @pl.when(t == 0)
    def _():
        s_scr[...] = s0_ref[0]

    bg = bg_ref[0]
    lane = lax.broadcasted_iota(jnp.int32, bg.shape, 1)
    bg_t = bg.T
    sub = lax.broadcasted_iota(jnp.int32, bg_t.shape, 0)
    ri = lax.broadcasted_iota(jnp.int32, (c, c), 0)
    ci = lax.broadcasted_iota(jnp.int32, (c, c), 1)
    causal = ri >= ci
    strict = ri > ci
    ltri = causal.astype(BF16)
    utri = (ri <= ci).astype(BF16)
    eye = jnp.where(ri == ci, 1.0, 0.0)

    items = [(hh, blk) for hh in range(hb) for blk in range(nblk)]
    pre = {}
    gates = []
    for hh in range(hb):
        h = pl.program_id(1) * hb + hh
        gates.append((
            jnp.sum(jnp.where(lane == h, bg, 0.0), axis=-1, keepdims=True),
            jnp.sum(jnp.where(lane == h + B_HEADS, bg, 0.0), axis=-1, keepdims=True),
            jnp.sum(jnp.where(sub == h + B_HEADS, bg_t, 0.0), axis=0, keepdims=True)))
    for hh, blk in items:
        rows = slice(blk * c, (blk + 1) * c)
        cols = slice(hh * LANES, (hh + 1) * LANES)
        bc = gates[hh][0][rows, :]
        g_col = gates[hh][1][rows, :]
        g_row = gates[hh][2][:, rows]
        q = q_ref[0, rows, cols]
        k = k_ref[0, rows, cols]
        v = v_ref[0, rows, cols]
        ghi, glo = _split(jnp.broadcast_to(g_col, (c, LANES)))
        g_cum = _dot(ltri, ghi) + _dot(ltri, glo)
        rhi, rlo = _split(jnp.broadcast_to(g_row, (c, c)))
        g_cum_row = _dot(rhi, utri) + _dot(rlo, utri)
        diff = jnp.where(causal, g_cum[:, :c] - g_cum_row, 0.0)
        decay = jnp.where(causal, jnp.exp(diff), 0.0)
        kb = k.astype(BF16)
        m = jnp.where(strict, bc * _dot_nt(kb, kb) * decay, 0.0)
        pre[hh, blk] = dict(q=q, k=k, v=v, bc=bc, g_cum=g_cum, decay=decay, kb=kb, m=m)

    pmat = {it: eye - pre[it]["m"] for it in items}
    qmat = {it: pre[it]["m"] for it in items}
    for _ in range(int(math.log2(c)) - 1):
        qb = {it: qmat[it].astype(BF16) for it in items}
        qmat = {it: _dot(qb[it], qb[it]) for it in items}
        pmat = {it: pmat[it] + _dot(pmat[it].astype(BF16), qmat[it].astype(BF16)) for it in items}

    for it in items:
        d = pre[it]
        e_g = jnp.exp(d["g_cum"])
        rhs = jnp.concatenate([d["v"] * d["bc"], d["k"] * (d["bc"] * e_g)], axis=1)
        sol = _dot(pmat[it].astype(BF16), rhs.astype(BF16))
        g_last = d["g_cum"][c - 1:c, :]
        d.update(
            u=sol[:, :B_DV], wb=sol[:, B_DV:].astype(BF16),
            a_loc=jnp.where(causal, _dot_nt(d["q"].astype(BF16), d["kb"]) * d["decay"],
                            0.0).astype(BF16),
            q_dec=(d["q"] * e_g).astype(BF16),
            k_tail_t=(d["k"] * jnp.exp(g_last - d["g_cum"])).T.astype(BF16),
            g_tail=jnp.exp(g_last))

    states = [s_scr[hh] for hh in range(hb)]
    outs = {}
    for blk in range(nblk):
        for hh in range(hb):
            d = pre[hh, blk]
            sb = states[hh].astype(BF16)
            vb = (d["u"] - _dot(d["wb"], sb)).astype(BF16)
            outs[hh, blk] = _dot(d["q_dec"], sb) + _dot(d["a_loc"], vb)
            states[hh] = states[hh] * d["g_tail"] + _dot(d["k_tail_t"], vb)

    for hh, blk in items:
        o_ref[0, blk * c:(blk + 1) * c, hh * LANES:(hh + 1) * LANES] = (
            _rms_rows(outs[hh, blk], nw_ref[...]).astype(BF16))
    for hh in range(hb):
        s_scr[hh] = states[hh]

    @pl.when(t == pl.num_programs(2) - 1)
    def _():
        for hh in range(hb):
            sout_ref[0, hh] = states[hh]


def _delta(dqkv3, bg3, s0, nw_row, c, tl, hb):
    b, l, _ = dqkv3.shape
    n_hg = B_HEADS // hb
    kern = functools.partial(_delta_kernel, c=c, nblk=tl // c, hb=hb)
    lane_blk = lambda off: pl.BlockSpec((1, tl, hb * LANES), lambda bi, h, t: (bi, t, h + off))
    state = pl.BlockSpec((1, hb, B_DK, B_DV), lambda bi, h, t: (bi, h, 0, 0))
    return pl.pallas_call(
        kern,
        grid=(b, n_hg, l // tl),
        in_specs=[
            lane_blk(0), lane_blk(n_hg), lane_blk(2 * n_hg),
            pl.BlockSpec((1, tl, LANES), lambda bi, h, t: (bi, t, 0)),
            state,
            pl.BlockSpec((1, B_DV), lambda bi, h, t: (0, 0)),
        ],
        out_specs=[lane_blk(0), state],
        out_shape=[
            jax.ShapeDtypeStruct((b, l, B_V), BF16),
            jax.ShapeDtypeStruct((b, B_HEADS, B_DK, B_DV), F32),
        ],
        scratch_shapes=[pltpu.VMEM((hb, B_DK, B_DV), F32)],
        compiler_params=_cparams(3),
        name="delta_rule",
    )(dqkv3, dqkv3, dqkv3, bg3, s0, nw_row)


def _merge_kernel(x_ref, oa_ref, ob_ref, ln_ref, wdg, wga, wgb, wba, wbb, wout, y_ref):
    x = x_ref[...]
    h = _rms_rows(x, ln_ref[...]).astype(BF16)
    ob = ob_ref[...].astype(F32) * _silu(_dot(h, wdg[...]))
    branch_b = _dot(ob.astype(BF16), wbb[...])
    branch_a = _dot(oa_ref[...], wba[...])
    merged = (jax.nn.sigmoid(_dot(h, wga[...])) * branch_a
              + jax.nn.sigmoid(_dot(h, wgb[...])) * branch_b)
    y_ref[...] = x + _dot(merged.astype(BF16), wout[...])


def _merge(x2, oa2, ob2, ln_w, wdg, wga, wgb, wba, wbb, wout, tm):
    n = x2.shape[0]
    tile = pl.BlockSpec((tm, D_MODEL), lambda i: (i, 0))
    wspec = pl.BlockSpec((D_MODEL, D_MODEL), lambda i: (0, 0), pipeline_mode=pl.Buffered(1))
    return pl.pallas_call(
        _merge_kernel,
        grid=(n // tm,),
        in_specs=[tile, tile, tile, pl.BlockSpec((1, D_MODEL), lambda i: (0, 0))] + [wspec] * 6,
        out_specs=tile,
        out_shape=jax.ShapeDtypeStruct((n, D_MODEL), F32),
        compiler_params=_cparams(1),
        name="merge_outproj",
    )(x2, oa2, ob2, ln_w, wdg, wga, wgb, wba, wbb, wout)


def _ffn_kernel(x_ref, ln_ref, wg_ref, wu_ref, wd_ref, y_ref, h_scr):
    cidx = pl.program_id(1)

    @pl.when(cidx == 0)
    def _():
        x = x_ref[...]
        h_scr[...] = _rms_rows(x, ln_ref[...]).astype(BF16)
        y_ref[...] = x

    h = h_scr[...]
    act = _silu(_dot(h, wg_ref[...])) * _dot(h, wu_ref[...])
    y_ref[...] += _dot(act.astype(BF16), wd_ref[...])


def _ffn(x2, ln_w, w_gate_up, w_down, tm, n_chunks):
    n = x2.shape[0]
    fc = D_FF // n_chunks
    return pl.pallas_call(
        _ffn_kernel,
        grid=(n // tm, n_chunks),
        in_specs=[
            pl.BlockSpec((tm, D_MODEL), lambda i, c: (i, 0)),
            pl.BlockSpec((1, D_MODEL), lambda i, c: (0, 0)),
            pl.BlockSpec((D_MODEL, fc), lambda i, c: (0, c)),
            pl.BlockSpec((D_MODEL, fc), lambda i, c: (0, c + n_chunks)),
            pl.BlockSpec((fc, D_MODEL), lambda i, c: (c, 0)),
        ],
        out_specs=pl.BlockSpec((tm, D_MODEL), lambda i, c: (i, 0)),
        out_shape=jax.ShapeDtypeStruct((n, D_MODEL), F32),
        scratch_shapes=[pltpu.VMEM((tm, D_MODEL), BF16)],
        compiler_params=_cparams(2),
        name="swiglu_ffn",
    )(x2, ln_w, w_gate_up, w_gate_up, w_down)


def _pick_tile(n, target):
    t = min(n, target)
    while n % t:
        t //= 2
    return t


def _prep_weights(ln1_w, w_in, q_norm_w, k_norm_w, lambda_q1, lambda_k1, lambda_q2, lambda_k2,
                  subln_w, w_branch_a, conv_w, a_log, dt_bias, delta_norm_w, w_branch_b,
                  w_out, ln2_w, w_gate_up, w_down):
    o_d = 2 * A_QK + A_V
    o_dg = o_d + CONV_CH
    o_bg = o_dg + B_V
    o_ga = o_bg + 2 * B_HEADS
    o_gb = o_ga + D_MODEL
    pad_lanes = lambda v: jnp.pad(v.astype(F32), (B_HEADS, LANES - 2 * B_HEADS))[None, :]
    return dict(
        ln1=ln1_w[None, :], ln2=ln2_w[None, :],
        w_qkv=w_in[:, :o_d].astype(BF16),
        w_d=w_in[:, o_d:o_dg].astype(BF16),
        w_dg=w_in[:, o_dg:o_bg].astype(BF16),
        w_bg=jnp.pad(w_in[:, o_bg:o_ga], ((0, 0), (0, LANES - 2 * B_HEADS))).astype(BF16),
        w_ga=w_in[:, o_ga:o_gb].astype(BF16),
        w_gb=w_in[:, o_gb:].astype(BF16),
        qn=jnp.tile(q_norm_w, A_QK // A_DIM)[None, :],
        kn=jnp.tile(k_norm_w, A_QK // A_DIM)[None, :],
        lams=tuple(v[None, :] for v in (lambda_q1, lambda_k1, lambda_q2, lambda_k2)),
        subln=subln_w[None, :],
        w_ba=w_branch_a.astype(BF16), w_bb=w_branch_b.astype(BF16),
        conv_w=conv_w, alog=pad_lanes(a_log), dtb=pad_lanes(dt_bias),
        dnorm=delta_norm_w[None, :],
        w_out=w_out.astype(BF16),
        w_gate_up=w_gate_up.astype(BF16), w_down=w_down.astype(BF16),
    )


def _layer(x, past_k, past_v, s0, conv_buf, lam_init, p):
    b, l, _ = x.shape
    n = b * l
    x2 = x.reshape(n, D_MODEL)

    q2, k2, v2 = _inproj_attn(x2, p["ln1"], p["w_qkv"], p["qn"], p["kn"], _pick_tile(n, 1024))

    tm_d = _pick_tile(l, 1024)
    cbuf8 = jnp.pad(conv_buf, ((0, 0), (SUBLANES - (CONV_W - 1), 0), (0, 0)))
    dqkv, bg, cst = _inproj_delta(x2, p["ln1"], p["w_d"], p["w_bg"], p["conv_w"], cbuf8,
                                  p["alog"], p["dtb"], tm_d, l)
    tiles_per_seq = l // tm_d
    new_conv = cst[tiles_per_seq - 1::tiles_per_seq, SUBLANES - (CONV_W - 1):, :]

    q3 = q2.reshape(b, l, A_QK)
    k3 = k2.reshape(b, l, A_QK)
    v3 = v2.reshape(b, l, A_V)
    if past_k is None:
        oa = _attn_prompt(q3, k3, v3, p["lams"], p["subln"], lam_init, _pick_tile(l, 256))
    else:
        past = past_k.shape[1]
        oa = _attn_sample(q3, past_k.reshape(b, past, A_QK), past_v.reshape(b, past, A_V),
                          k3, v3, p["lams"], p["subln"], lam_init)

    c = DELTA_BLOCK if l % DELTA_BLOCK == 0 else l
    tl = c * max(1, min(4, l // c))
    ob, s_new = _delta(dqkv.reshape(b, l, CONV_CH), bg.reshape(b, l, LANES), s0, p["dnorm"], c, tl, 8)

    x1 = _merge(x2, oa.reshape(n, A_V), ob.reshape(n, B_V), p["ln1"], p["w_dg"], p["w_ga"],
                p["w_gb"], p["w_ba"], p["w_bb"], p["w_out"], _pick_tile(n, 512))
    y = _ffn(x1, p["ln2"], p["w_gate_up"], p["w_down"], _pick_tile(n, 512), 2)
    return (y.reshape(b, l, D_MODEL), k3.reshape(b, l, A_HEADS, 2, A_DIM),
            v3.reshape(b, l, A_HEADS, A_VDIM), s_new, new_conv)


def kernel(x_prompt, x_sample, cache_k, cache_v, state_delta, state_conv, ln1_w, w_in, q_norm_w,
           k_norm_w, lambda_q1, lambda_k1, lambda_q2, lambda_k2, subln_w, w_branch_a, conv_w,
           a_log, dt_bias, delta_norm_w, w_branch_b, w_out, ln2_w, w_gate_up, w_down):
    depth = ln1_w.shape[0]
    xp, xs = x_prompt, x_sample
    outs_p, outs_s = [], []
    for layer in range(depth):
        lam_init = 0.8 - 0.6 * math.exp(-0.3 * layer)
        p = _prep_weights(*(w[layer] for w in (
            ln1_w, w_in, q_norm_w, k_norm_w, lambda_q1, lambda_k1, lambda_q2, lambda_k2, subln_w,
            w_branch_a, conv_w, a_log, dt_bias, delta_norm_w, w_branch_b, w_out, ln2_w,
            w_gate_up, w_down)))
        bp = xp.shape[0]
        zero_conv = jnp.zeros((bp, CONV_W - 1, CONV_CH), xp.dtype)
        zero_s = jnp.zeros((bp, B_HEADS, B_DK, B_DV), F32)
        xp, *rest_p = _layer(xp, None, None, zero_s, zero_conv, lam_init, p)
        xs, *rest_s = _layer(xs, cache_k[layer], cache_v[layer], state_delta[layer],
                             state_conv[layer], lam_init, p)
        outs_p.append(rest_p)
        outs_s.append(rest_s)
    stack = lambda outs, idx: jnp.stack([o[idx] for o in outs])
    return (xp, xs,
            stack(outs_p, 0), stack(outs_p, 1), stack(outs_p, 2), stack(outs_p, 3),
            stack(outs_s, 0), stack(outs_s, 1), stack(outs_s, 2), stack(outs_s, 3))
```

```python
import functools
import math

import jax
import jax.numpy as jnp
from jax import lax
from jax.experimental import pallas as pl
from jax.experimental.pallas import tpu as pltpu

F32 = jnp.float32
BF16 = jnp.bfloat16

D_MODEL = 1024
CHUNK = 64
A_HEADS = 8
A_DIM = 64
A_VDIM = 2 * A_DIM
B_HEADS = 8
B_DK = 128
B_DV = 128
CONV_W = 4
DELTA_BLOCK = 64
D_FF = -(-8 * D_MODEL // (3 * 256)) * 256
EPS = 1e-6

A_QK = A_HEADS * 2 * A_DIM
A_V = A_HEADS * A_VDIM
B_QK = B_HEADS * B_DK
B_V = B_HEADS * B_DV
CONV_CH = 2 * B_QK + B_V

LANES = 128
SUBLANES = 8
MXU_DIM = 256
NEG_BIG = -1e30
VMEM_LIMIT = 56 * 1024 * 1024


def _cparams(n_axes):
    return pltpu.CompilerParams(dimension_semantics=("arbitrary",) * n_axes,
                                vmem_limit_bytes=VMEM_LIMIT)


def _dot(a, b):
    return jnp.dot(a, b, preferred_element_type=F32)


def _dot_nt(a, b):
    return lax.dot_general(a, b, (((1,), (1,)), ((), ())), preferred_element_type=F32)


def _split(a):
    hi = a.astype(BF16)
    lo = (a - hi.astype(F32)).astype(BF16)
    return hi, lo


def _rms_rows(x, w):
    ms = jnp.mean(x * x, axis=-1, keepdims=True)
    return x * lax.rsqrt(ms + EPS) * w


def _group_sumsq(z, gshift):
    r = lax.broadcasted_iota(jnp.int32, (MXU_DIM, MXU_DIM), 0) >> gshift
    c = lax.broadcasted_iota(jnp.int32, (MXU_DIM, MXU_DIM), 1) >> gshift
    blk = (r == c).astype(BF16)
    outs = []
    for s in range(z.shape[1] // MXU_DIM):
        zs = z[:, s * MXU_DIM:(s + 1) * MXU_DIM]
        outs.append(_dot((zs * zs).astype(BF16), blk))
    return jnp.concatenate(outs, axis=1)


def _silu(x):
    return x * jax.nn.sigmoid(x)


def _inproj_attn_kernel(x_ref, ln_ref, w_ref, wkt_ref, qn_ref, kn_ref, knc_ref, q_ref, k_ref, v_ref,
                        h_scr, *, k_transposed):
    j = pl.program_id(1)

    @pl.when(j == 0)
    def _():
        h = _rms_rows(x_ref[...], ln_ref[...]).astype(BF16)
        h_scr[...] = h
        z = _dot(h, w_ref[...])
        n = z * lax.rsqrt(_group_sumsq(z, 6) * (1.0 / A_DIM) + EPS) * qn_ref[...]
        q_ref[...] = (n * (A_DIM ** -0.5)).astype(BF16)

    @pl.when(j == 1)
    def _():
        if k_transposed:
            zt = _dot_nt(wkt_ref[...], h_scr[...])
            z3 = zt.reshape(A_QK // A_DIM, A_DIM, zt.shape[1])
            ms = jnp.mean(z3 * z3, axis=1, keepdims=True)
            kn3 = knc_ref[...].reshape(A_QK // A_DIM, A_DIM, 1)
            k_ref[0] = (z3 * lax.rsqrt(ms + EPS) * kn3).reshape(zt.shape)
        else:
            z = _dot(h_scr[...], w_ref[...])
            k_ref[...] = z * lax.rsqrt(_group_sumsq(z, 6) * (1.0 / A_DIM) + EPS) * kn_ref[...]

    @pl.when(j == 2)
    def _():
        v_ref[...] = _dot(h_scr[...], w_ref[...])


def _inproj_attn(x2, ln_w, w_qkv, w_kt, qn_row, kn_row, kn_col, tm, seq_len, k_transposed):
    n = x2.shape[0]
    row = lambda i, j: (0, 0)
    if k_transposed:
        tiles_per_seq = seq_len // tm
        k_spec = pl.BlockSpec((1, A_QK, tm), lambda i, j: (i // tiles_per_seq, 0, i % tiles_per_seq))
        k_shape = jax.ShapeDtypeStruct((n // seq_len, A_QK, seq_len), F32)
    else:
        k_spec = pl.BlockSpec((tm, A_QK), lambda i, j: (i, 0))
        k_shape = jax.ShapeDtypeStruct((n, A_QK), F32)
    return pl.pallas_call(
        functools.partial(_inproj_attn_kernel, k_transposed=k_transposed),
        grid=(n // tm, 3),
        in_specs=[
            pl.BlockSpec((tm, D_MODEL), lambda i, j: (i, 0)),
            pl.BlockSpec((1, D_MODEL), row),
            pl.BlockSpec((D_MODEL, A_QK), lambda i, j: (0, j)),
            pl.BlockSpec((A_QK, D_MODEL), row, pipeline_mode=pl.Buffered(1)),
            pl.BlockSpec((1, A_QK), row),
            pl.BlockSpec((1, A_QK), row),
            pl.BlockSpec((A_QK, 1), row),
        ],
        out_specs=[
            pl.BlockSpec((tm, A_QK), lambda i, j: (i, 0)),
            k_spec,
            pl.BlockSpec((tm, A_V), lambda i, j: (i, 0)),
        ],
        out_shape=[
            jax.ShapeDtypeStruct((n, A_QK), BF16),
            k_shape,
            jax.ShapeDtypeStruct((n, A_V), F32),
        ],
        scratch_shapes=[pltpu.VMEM((tm, D_MODEL), BF16)],
        compiler_params=_cparams(2),
        name="inproj_attn",
    )(x2, ln_w, w_qkv, w_kt, qn_row, kn_row, kn_col)


def _inproj_delta_kernel(x_ref, ln_ref, w_ref, wbg_ref, cw_ref, cbuf_ref, alog_ref, dtb_ref,
                         dqkv_ref, bg_ref, cst_ref, h_scr, zbuf, carry, *, tiles_per_seq):
    i = pl.program_id(0)
    j = pl.program_id(1)
    tm = x_ref.shape[0]

    @pl.when(j == 0)
    def _():
        h = _rms_rows(x_ref[...], ln_ref[...]).astype(BF16)
        h_scr[...] = h
        zb = _dot(h, wbg_ref[...])
        lane = lax.broadcasted_iota(jnp.int32, zb.shape, 1)
        a = zb + dtb_ref[...]
        softplus = jnp.maximum(a, 0.0) + jnp.log1p(jnp.exp(-jnp.abs(a)))
        g = -jnp.exp(alog_ref[...]) * softplus
        bg_ref[...] = jnp.where(lane < B_HEADS, jax.nn.sigmoid(zb),
                                jnp.where(lane < 2 * B_HEADS, g, 0.0))

    z = _dot(h_scr[...], w_ref[...])
    first = (i % tiles_per_seq) == 0
    prev = jnp.where(first, cbuf_ref[0], carry[j])
    zbuf[0:SUBLANES, :] = prev
    zbuf[SUBLANES:, :] = z
    tail = z[tm - SUBLANES:, :]
    carry[j] = tail
    cst_ref[0] = tail

    cw = cw_ref[...]
    y = z * cw[CONV_W - 1:CONV_W, :]
    for s in range(1, CONV_W):
        y = y + zbuf[pl.ds(SUBLANES - s, tm), :] * cw[CONV_W - 1 - s:CONV_W - s, :]
    y = _silu(y)

    @pl.when(j < 2)
    def _():
        n = y * lax.rsqrt(_group_sumsq(y, 7) + EPS)
        dqkv_ref[...] = n * jnp.where(j == 0, B_DK ** -0.5, 1.0)

    @pl.when(j == 2)
    def _():
        dqkv_ref[...] = y


def _inproj_delta(x2, ln_w, w_d, w_bg, conv_w, cbuf8, alog_row, dtb_row, tm, seq_len):
    n = x2.shape[0]
    tiles_per_seq = seq_len // tm
    n_tiles = n // tm
    row = lambda i, j: (0, 0)
    kern = functools.partial(_inproj_delta_kernel, tiles_per_seq=tiles_per_seq)
    return pl.pallas_call(
        kern,
        grid=(n_tiles, 3),
        in_specs=[
            pl.BlockSpec((tm, D_MODEL), lambda i, j: (i, 0)),
            pl.BlockSpec((1, D_MODEL), row),
            pl.BlockSpec((D_MODEL, B_QK), lambda i, j: (0, j)),
            pl.BlockSpec((D_MODEL, LANES), row),
            pl.BlockSpec((CONV_W, B_QK), lambda i, j: (0, j)),
            pl.BlockSpec((1, SUBLANES, B_QK), lambda i, j: (i // tiles_per_seq, 0, j)),
            pl.BlockSpec((1, LANES), row),
            pl.BlockSpec((1, LANES), row),
        ],
        out_specs=[
            pl.BlockSpec((tm, B_QK), lambda i, j: (i, j)),
            pl.BlockSpec((tm, LANES), lambda i, j: (i, 0)),
            pl.BlockSpec((1, SUBLANES, B_QK), lambda i, j: (i, 0, j)),
        ],
        out_shape=[
            jax.ShapeDtypeStruct((n, CONV_CH), F32),
            jax.ShapeDtypeStruct((n, LANES), F32),
            jax.ShapeDtypeStruct((n_tiles, SUBLANES, CONV_CH), F32),
        ],
        scratch_shapes=[
            pltpu.VMEM((tm, D_MODEL), BF16),
            pltpu.VMEM((tm + SUBLANES, B_QK), F32),
            pltpu.VMEM((3, SUBLANES, B_QK), F32),
        ],
        compiler_params=_cparams(2),
        name="inproj_delta",
    )(x2, ln_w, w_d, w_bg, conv_w, cbuf8, alog_row, dtb_row)


def _lambda_value(lq1, lk1, lq2, lk2, lam_init):
    d1 = jnp.sum(lq1[...] * lk1[...], axis=-1, keepdims=True)
    d2 = jnp.sum(lq2[...] * lk2[...], axis=-1, keepdims=True)
    return jnp.exp(d1) - jnp.exp(d2) + lam_init


def _split_maps(q):
    lane = lax.broadcasted_iota(jnp.int32, q.shape, 1)
    zero = jnp.zeros_like(q)
    return jnp.where(lane < A_DIM, q, zero), jnp.where(lane >= A_DIM, q, zero)


def _attn_finish(o, sw, lam_init):
    ms = jnp.mean(o * o, axis=-1, keepdims=True)
    return (o * lax.rsqrt(ms + EPS) * sw * (1.0 - lam_init)).astype(BF16)


def _attn_prompt_kernel(q_ref, k_ref, v_ref, lq1, lk1, lq2, lk2, sw_ref, o_ref,
                        kb_scr, vb_scr, s_scr, p_scr, *, tq, lam_init):
    seq = q_ref.shape[1]
    lam = _lambda_value(lq1, lk1, lq2, lk2, lam_init)
    kb_scr[...] = k_ref[0].astype(BF16)
    vb_scr[...] = v_ref[0].astype(BF16)
    ri = lax.broadcasted_iota(jnp.int32, (2 * tq, tq), 0)
    ci = lax.broadcasted_iota(jnp.int32, (2 * tq, tq), 1)
    diag_mask = (ci // CHUNK) <= ((ri & (tq - 1)) // CHUNK)
    n_half = tq // LANES
    for qi in range(seq // tq):
        n_keys = (qi + 1) * tq
        q2 = jnp.concatenate(_split_maps(q_ref[0, qi * tq:(qi + 1) * tq, :]), axis=0)
        m_lane = None
        for j in range(qi + 1):
            s = _dot(q2, kb_scr[:, j * tq:(j + 1) * tq])
            if j == qi:
                s = jnp.where(diag_mask, s, NEG_BIG)
            s_scr[:, j * tq:(j + 1) * tq] = s
            for i in range(n_half):
                half = s[:, i * LANES:(i + 1) * LANES]
                m_lane = half if m_lane is None else jnp.maximum(m_lane, half)
        m_full = jnp.broadcast_to(jnp.max(m_lane, axis=-1, keepdims=True), (2 * tq, LANES))
        l_lane = jnp.zeros((2 * tq, LANES), F32)
        for j in range(qi + 1):
            for i in range(n_half):
                cols = slice(j * tq + i * LANES, j * tq + (i + 1) * LANES)
                p = jnp.exp(s_scr[:, cols] - m_full)
                l_lane = l_lane + p
                p_scr[:, cols] = p.astype(BF16)
        acc = _dot(p_scr[:, :n_keys], vb_scr[:n_keys, :])
        o2 = acc / jnp.sum(l_lane, axis=-1, keepdims=True)
        o_ref[0, qi * tq:(qi + 1) * tq, :] = _attn_finish(o2[:tq] - lam * o2[tq:], sw_ref[...],
                                                          lam_init)


def _attn_prompt(q3, kt3, v3, lams, sw_row, lam_init, tq):
    b, l, _ = q3.shape
    vec = lambda bi, h: (0, 0)
    head = pl.BlockSpec((1, l, LANES), lambda bi, h: (bi, 0, h))
    kern = functools.partial(_attn_prompt_kernel, tq=tq, lam_init=lam_init)
    return pl.pallas_call(
        kern,
        grid=(b, A_HEADS),
        in_specs=[
            head, pl.BlockSpec((1, 2 * A_DIM, l), lambda bi, h: (bi, h, 0)), head,
            pl.BlockSpec((1, A_DIM), vec), pl.BlockSpec((1, A_DIM), vec),
            pl.BlockSpec((1, A_DIM), vec), pl.BlockSpec((1, A_DIM), vec),
            pl.BlockSpec((1, A_VDIM), vec),
        ],
        out_specs=head,
        out_shape=jax.ShapeDtypeStruct((b, l, A_V), BF16),
        scratch_shapes=[
            pltpu.VMEM((2 * A_DIM, l), BF16), pltpu.VMEM((l, LANES), BF16),
            pltpu.VMEM((2 * tq, l), F32), pltpu.VMEM((2 * tq, l), BF16),
        ],
        compiler_params=_cparams(2),
        name="attn_prompt",
    )(q3, kt3, v3, *lams, sw_row)


def _attn_sample_kernel(q_ref, kc_ref, vc_ref, kn_ref, vn_ref, lq1, lk1, lq2, lk2, sw_ref, o_ref,
                        *, lam_init):
    h = pl.program_id(1)
    l = q_ref.shape[1]
    past = kc_ref.shape[3]
    lam = _lambda_value(lq1, lk1, lq2, lk2, lam_init)
    q2 = jnp.concatenate(_split_maps(q_ref[0]), axis=0)
    vc = vc_ref[0, pl.ds(h, past, stride=A_HEADS), :].astype(BF16)
    sc = _dot(q2, kc_ref[0, 0].astype(BF16))
    sn = _dot_nt(q2, kn_ref[0].astype(BF16))
    mx = jnp.maximum(jnp.max(sc, axis=-1, keepdims=True), jnp.max(sn, axis=-1, keepdims=True))
    pc = jnp.exp(sc - mx)
    pn = jnp.exp(sn - mx)
    den = jnp.sum(pc, axis=-1, keepdims=True) + jnp.sum(pn, axis=-1, keepdims=True)
    o2 = (_dot(pc.astype(BF16), vc) + _dot(pn.astype(BF16), vn_ref[0].astype(BF16))) / den
    o_ref[0] = _attn_finish(o2[:l] - lam * o2[l:], sw_ref[...], lam_init)


def _attn_sample(q3, kct, vc2, kn3, vn3, lams, sw_row, lam_init):
    b, l, _ = q3.shape
    past = kct.shape[3]
    vec = lambda bi, h: (0, 0)
    blk = lambda rows: pl.BlockSpec((1, rows, LANES), lambda bi, h: (bi, 0, h))
    kern = functools.partial(_attn_sample_kernel, lam_init=lam_init)
    return pl.pallas_call(
        kern,
        grid=(b, A_HEADS),
        in_specs=[
            blk(l),
            pl.BlockSpec((1, 1, 2 * A_DIM, past), lambda bi, h: (bi, h, 0, 0)),
            pl.BlockSpec((1, past * A_HEADS, A_VDIM), lambda bi, h: (bi, 0, 0)),
            blk(l), blk(l),
            pl.BlockSpec((1, A_DIM), vec), pl.BlockSpec((1, A_DIM), vec),
            pl.BlockSpec((1, A_DIM), vec), pl.BlockSpec((1, A_DIM), vec),
            pl.BlockSpec((1, A_VDIM), vec),
        ],
        out_specs=blk(l),
        out_shape=jax.ShapeDtypeStruct((b, l, A_V), BF16),
        compiler_params=_cparams(2),
        name="attn_sample",
    )(q3, kct, vc2, kn3, vn3, *lams, sw_row)


def _delta_kernel(q_ref, k_ref, v_ref, bg_ref, s0_ref, nw_ref, o_ref, sout_ref, s_scr,
                  *, c, nblk, hb):
    t = pl.program_id(2)

    @pl.when(t == 0)
    def _():
        s_scr[...] = s0_ref[0]

    bg = bg_ref[0]
    lane = lax.broadcasted_iota(jnp.int32, bg.shape, 1)
    bg_t = bg.T
    sub = lax.broadcasted_iota(jnp.int32, bg_t.shape, 0)
    ri = lax.broadcasted_iota(jnp.int32, (c, c), 0)
    ci = lax.broadcasted_iota(jnp.int32, (c, c), 1)
    causal = ri >= ci
    strict = ri > ci
    ltri = causal.astype(BF16)
    utri = (ri <= ci).astype(BF16)
    eye = jnp.where(ri == ci, 1.0, 0.0)

    items = [(hh, blk) for hh in range(hb) for blk in range(nblk)]
    pre = {}
    gates = []
    for hh in range(hb):
        h = pl.program_id(1) * hb + hh
        gates.append((
            jnp.sum(jnp.where(lane == h, bg, 0.0), axis=-1, keepdims=True),
            jnp.sum(jnp.where(lane == h + B_HEADS, bg, 0.0), axis=-1, keepdims=True),
            jnp.sum(jnp.where(sub == h + B_HEADS, bg_t, 0.0), axis=0, keepdims=True)))
    for hh, blk in items:
        rows = slice(blk * c, (blk + 1) * c)
        cols = slice(hh * LANES, (hh + 1) * LANES)
        bc = gates[hh][0][rows, :]
        g_col = gates[hh][1][rows, :]
        g_row = gates[hh][2][:, rows]
        q = q_ref[0, rows, cols]
        k = k_ref[0, rows, cols]
        v = v_ref[0, rows, cols]
        ghi, glo = _split(jnp.broadcast_to(g_col, (c, LANES)))
        g_cum = _dot(ltri, ghi) + _dot(ltri, glo)
        rhi, rlo = _split(jnp.broadcast_to(g_row, (c, c)))
        g_cum_row = _dot(rhi, utri) + _dot(rlo, utri)
        diff = jnp.where(causal, g_cum[:, :c] - g_cum_row, 0.0)
        decay = jnp.where(causal, jnp.exp(diff), 0.0)
        kb = k.astype(BF16)
        m = jnp.where(strict, bc * _dot_nt(kb, kb) * decay, 0.0)
        pre[hh, blk] = dict(q=q, k=k, v=v, bc=bc, g_cum=g_cum, decay=decay, kb=kb, m=m)

    pmat = {it: eye - pre[it]["m"] for it in items}
    qmat = {it: pre[it]["m"] for it in items}
    for _ in range(int(math.log2(c)) - 1):
        qb = {it: qmat[it].astype(BF16) for it in items}
        qmat = {it: _dot(qb[it], qb[it]) for it in items}
        pmat = {it: pmat[it] + _dot(pmat[it].astype(BF16), qmat[it].astype(BF16)) for it in items}

    for it in items:
        d = pre[it]
        e_g = jnp.exp(d["g_cum"])
        rhs = jnp.concatenate([d["v"] * d["bc"], d["k"] * (d["bc"] * e_g)], axis=1)
        sol = _dot(pmat[it].astype(BF16), rhs.astype(BF16))
        g_last = d["g_cum"][c - 1:c, :]
        d.update(
            u=sol[:, :B_DV], wb=sol[:, B_DV:].astype(BF16),
            a_loc=jnp.where(causal, _dot_nt(d["q"].astype(BF16), d["kb"]) * d["decay"],
                            0.0).astype(BF16),
            q_dec=(d["q"] * e_g).astype(BF16),
            k_tail_t=(d["k"] * jnp.exp(g_last - d["g_cum"])).T.astype(BF16),
            g_tail=jnp.exp(g_last))

    states = [s_scr[hh] for hh in range(hb)]
    outs = {}
    for blk in range(nblk):
        for hh in range(hb):
            d = pre[hh, blk]
            sb = states[hh].astype(BF16)
            vb = (d["u"] - _dot(d["wb"], sb)).astype(BF16)
            outs[hh, blk] = _dot(d["q_dec"], sb) + _dot(d["a_loc"], vb)
            states[hh] = states[hh] * d["g_tail"] + _dot(d["k_tail_t"], vb)

    for hh, blk in items:
        o_ref[0, blk * c:(blk + 1) * c, hh * LANES:(hh + 1) * LANES] = (
            _rms_rows(outs[hh, blk], nw_ref[...]).astype(BF16))
    for hh in range(hb):
        s_scr[hh] = states[hh]

    @pl.when(t == pl.num_programs(2) - 1)
    def _():
        for hh in range(hb):
            sout_ref[0, hh] = states[hh]


def _delta(dqkv3, bg3, s0, nw_row, c, tl, hb):
    b, l, _ = dqkv3.shape
    n_hg = B_HEADS // hb
    kern = functools.partial(_delta_kernel, c=c, nblk=tl // c, hb=hb)
    lane_blk = lambda off: pl.BlockSpec((1, tl, hb * LANES), lambda bi, h, t: (bi, t, h + off))
    state = pl.BlockSpec((1, hb, B_DK, B_DV), lambda bi, h, t: (bi, h, 0, 0))
    return pl.pallas_call(
        kern,
        grid=(b, n_hg, l // tl),
        in_specs=[
            lane_blk(0), lane_blk(n_hg), lane_blk(2 * n_hg),
            pl.BlockSpec((1, tl, LANES), lambda bi, h, t: (bi, t, 0)),
            state,
            pl.BlockSpec((1, B_DV), lambda bi, h, t: (0, 0)),
        ],
        out_specs=[lane_blk(0), state],
        out_shape=[
            jax.ShapeDtypeStruct((b, l, B_V), BF16),
            jax.ShapeDtypeStruct((b, B_HEADS, B_DK, B_DV), F32),
        ],
        scratch_shapes=[pltpu.VMEM((hb, B_DK, B_DV), F32)],
        compiler_params=_cparams(3),
        name="delta_rule",
    )(dqkv3, dqkv3, dqkv3, bg3, s0, nw_row)


def _merge_kernel(x_ref, oa_ref, ob_ref, ln_ref, wdg, wga, wgb, wba, wbb, wout, y_ref):
    x = x_ref[...]
    h = _rms_rows(x, ln_ref[...]).astype(BF16)
    ob = ob_ref[...].astype(F32) * _silu(_dot(h, wdg[...]))
    branch_b = _dot(ob.astype(BF16), wbb[...])
    branch_a = _dot(oa_ref[...], wba[...])
    merged = (jax.nn.sigmoid(_dot(h, wga[...])) * branch_a
              + jax.nn.sigmoid(_dot(h, wgb[...])) * branch_b)
    y_ref[...] = x + _dot(merged.astype(BF16), wout[...])


def _merge(x2, oa2, ob2, ln_w, wdg, wga, wgb, wba, wbb, wout, tm):
    n = x2.shape[0]
    tile = pl.BlockSpec((tm, D_MODEL), lambda i: (i, 0))
    wspec = pl.BlockSpec((D_MODEL, D_MODEL), lambda i: (0, 0), pipeline_mode=pl.Buffered(1))
    return pl.pallas_call(
        _merge_kernel,
        grid=(n // tm,),
        in_specs=[tile, tile, tile, pl.BlockSpec((1, D_MODEL), lambda i: (0, 0))] + [wspec] * 6,
        out_specs=tile,
        out_shape=jax.ShapeDtypeStruct((n, D_MODEL), F32),
        compiler_params=_cparams(1),
        name="merge_outproj",
    )(x2, oa2, ob2, ln_w, wdg, wga, wgb, wba, wbb, wout)


def _ffn_kernel(x_ref, ln_ref, wg_ref, wu_ref, wd_ref, y_ref, h_scr):
    cidx = pl.program_id(1)

    @pl.when(cidx == 0)
    def _():
        x = x_ref[...]
        h_scr[...] = _rms_rows(x, ln_ref[...]).astype(BF16)
        y_ref[...] = x

    h = h_scr[...]
    act = _silu(_dot(h, wg_ref[...])) * _dot(h, wu_ref[...])
    y_ref[...] += _dot(act.astype(BF16), wd_ref[...])


def _ffn(x2, ln_w, w_gate_up, w_down, tm, n_chunks):
    n = x2.shape[0]
    fc = D_FF // n_chunks
    return pl.pallas_call(
        _ffn_kernel,
        grid=(n // tm, n_chunks),
        in_specs=[
            pl.BlockSpec((tm, D_MODEL), lambda i, c: (i, 0)),
            pl.BlockSpec((1, D_MODEL), lambda i, c: (0, 0)),
            pl.BlockSpec((D_MODEL, fc), lambda i, c: (0, c)),
            pl.BlockSpec((D_MODEL, fc), lambda i, c: (0, c + n_chunks)),
            pl.BlockSpec((fc, D_MODEL), lambda i, c: (c, 0)),
        ],
        out_specs=pl.BlockSpec((tm, D_MODEL), lambda i, c: (i, 0)),
        out_shape=jax.ShapeDtypeStruct((n, D_MODEL), F32),
        scratch_shapes=[pltpu.VMEM((tm, D_MODEL), BF16)],
        compiler_params=_cparams(2),
        name="swiglu_ffn",
    )(x2, ln_w, w_gate_up, w_gate_up, w_down)


def _pick_tile(n, target):
    t = min(n, target)
    while n % t:
        t //= 2
    return t


def _prep_weights(ln1_w, w_in, q_norm_w, k_norm_w, lambda_q1, lambda_k1, lambda_q2, lambda_k2,
                  subln_w, w_branch_a, conv_w, a_log, dt_bias, delta_norm_w, w_branch_b,
                  w_out, ln2_w, w_gate_up, w_down):
    o_d = 2 * A_QK + A_V
    o_dg = o_d + CONV_CH
    o_bg = o_dg + B_V
    o_ga = o_bg + 2 * B_HEADS
    o_gb = o_ga + D_MODEL
    pad_lanes = lambda v: jnp.pad(v.astype(F32), (B_HEADS, LANES - 2 * B_HEADS))[None, :]
    return dict(
        ln1=ln1_w[None, :], ln2=ln2_w[None, :],
        w_qkv=w_in[:, :o_d].astype(BF16),
        w_d=w_in[:, o_d:o_dg].astype(BF16),
        w_dg=w_in[:, o_dg:o_bg].astype(BF16),
        w_bg=jnp.pad(w_in[:, o_bg:o_ga], ((0, 0), (0, LANES - 2 * B_HEADS))).astype(BF16),
        w_ga=w_in[:, o_ga:o_gb].astype(BF16),
        w_gb=w_in[:, o_gb:].astype(BF16),
        qn=jnp.tile(q_norm_w, A_QK // A_DIM)[None, :],
        kn=jnp.tile(k_norm_w, A_QK // A_DIM)[None, :],
        kn_col=jnp.tile(k_norm_w, A_QK // A_DIM)[:, None],
        w_kt=w_in[:, A_QK:2 * A_QK].T.astype(BF16),
        lams=tuple(v[None, :] for v in (lambda_q1, lambda_k1, lambda_q2, lambda_k2)),
        subln=subln_w[None, :],
        w_ba=w_branch_a.astype(BF16), w_bb=w_branch_b.astype(BF16),
        conv_w=conv_w, alog=pad_lanes(a_log), dtb=pad_lanes(dt_bias),
        dnorm=delta_norm_w[None, :],
        w_out=w_out.astype(BF16),
        w_gate_up=w_gate_up.astype(BF16), w_down=w_down.astype(BF16),
    )


def _layer(x, past_k, past_v, s0, conv_buf, lam_init, p):
    b, l, _ = x.shape
    n = b * l
    x2 = x.reshape(n, D_MODEL)

    prompt = past_k is None
    tm_a = _pick_tile(l, 1024) if prompt else _pick_tile(n, 1024)
    q2, k2, v2 = _inproj_attn(x2, p["ln1"], p["w_qkv"], p["w_kt"], p["qn"], p["kn"], p["kn_col"],
                              tm_a, l, prompt)

    tm_d = _pick_tile(l, 1024)
    cbuf8 = jnp.pad(conv_buf, ((0, 0), (SUBLANES - (CONV_W - 1), 0), (0, 0)))
    dqkv, bg, cst = _inproj_delta(x2, p["ln1"], p["w_d"], p["w_bg"], p["conv_w"], cbuf8,
                                  p["alog"], p["dtb"], tm_d, l)
    tiles_per_seq = l // tm_d
    new_conv = cst[tiles_per_seq - 1::tiles_per_seq, SUBLANES - (CONV_W - 1):, :]

    q3 = q2.reshape(b, l, A_QK)
    v3 = v2.reshape(b, l, A_V)
    if prompt:
        oa = _attn_prompt(q3, k2, v3, p["lams"], p["subln"], lam_init, _pick_tile(l, 256))
        k_out = jnp.transpose(k2.reshape(b, A_HEADS, 2, A_DIM, l), (0, 4, 1, 2, 3))
    else:
        past = past_k.shape[1]
        k3 = k2.reshape(b, l, A_QK)
        kct = jnp.transpose(past_k, (0, 2, 3, 4, 1)).reshape(b, A_HEADS, 2 * A_DIM, past)
        oa = _attn_sample(q3, kct, past_v.reshape(b, past * A_HEADS, A_VDIM), k3, v3, p["lams"],
                          p["subln"], lam_init)
        k_out = k3.reshape(b, l, A_HEADS, 2, A_DIM)

    c = DELTA_BLOCK if l % DELTA_BLOCK == 0 else l
    tl = c * max(1, min(4, l // c))
    ob, s_new = _delta(dqkv.reshape(b, l, CONV_CH), bg.reshape(b, l, LANES), s0, p["dnorm"], c, tl, 8)

    x1 = _merge(x2, oa.reshape(n, A_V), ob.reshape(n, B_V), p["ln1"], p["w_dg"], p["w_ga"],
                p["w_gb"], p["w_ba"], p["w_bb"], p["w_out"], _pick_tile(n, 512))
    y = _ffn(x1, p["ln2"], p["w_gate_up"], p["w_down"], _pick_tile(n, 512), 2)
    return (y.reshape(b, l, D_MODEL), k_out, v3.reshape(b, l, A_HEADS, A_VDIM), s_new, new_conv)


def kernel(x_prompt, x_sample, cache_k, cache_v, state_delta, state_conv, ln1_w, w_in, q_norm_w,
           k_norm_w, lambda_q1, lambda_k1, lambda_q2, lambda_k2, subln_w, w_branch_a, conv_w,
           a_log, dt_bias, delta_norm_w, w_branch_b, w_out, ln2_w, w_gate_up, w_down):
    depth = ln1_w.shape[0]
    xp, xs = x_prompt, x_sample
    outs_p, outs_s = [], []
    for layer in range(depth):
        lam_init = 0.8 - 0.6 * math.exp(-0.3 * layer)
        p = _prep_weights(*(w[layer] for w in (
            ln1_w, w_in, q_norm_w, k_norm_w, lambda_q1, lambda_k1, lambda_q2, lambda_k2, subln_w,
            w_branch_a, conv_w, a_log, dt_bias, delta_norm_w, w_branch_b, w_out, ln2_w,
            w_gate_up, w_down)))
        bp = xp.shape[0]
        zero_conv = jnp.zeros((bp, CONV_W - 1, CONV_CH), xp.dtype)
        zero_s = jnp.zeros((bp, B_HEADS, B_DK, B_DV), F32)
        xp, *rest_p = _layer(xp, None, None, zero_s, zero_conv, lam_init, p)
        xs, *rest_s = _layer(xs, cache_k[layer], cache_v[layer], state_delta[layer],
                             state_conv[layer], lam_init, p)
        outs_p.append(rest_p)
        outs_s.append(rest_s)
    stack = lambda outs, idx: jnp.stack([o[idx] for o in outs])
    return (xp, xs,
            stack(outs_p, 0), stack(outs_p, 1), stack(outs_p, 2), stack(outs_p, 3),
            stack(outs_s, 0), stack(outs_s, 1), stack(outs_s, 2), stack(outs_s, 3))
```

```python
import functools
import math

import jax
import jax.numpy as jnp
from jax import lax
from jax.experimental import pallas as pl
from jax.experimental.pallas import tpu as pltpu

F32 = jnp.float32
BF16 = jnp.bfloat16

D_MODEL = 1024
CHUNK = 64
A_HEADS = 8
A_DIM = 64
A_VDIM = 2 * A_DIM
B_HEADS = 8
B_DK = 128
B_DV = 128
CONV_W = 4
DELTA_BLOCK = 64
D_FF = -(-8 * D_MODEL // (3 * 256)) * 256
EPS = 1e-6

A_QK = A_HEADS * 2 * A_DIM
A_V = A_HEADS * A_VDIM
B_QK = B_HEADS * B_DK
B_V = B_HEADS * B_DV
CONV_CH = 2 * B_QK + B_V

LANES = 128
SUBLANES = 8
MXU_DIM = 256
NEG_BIG = -1e30
_DONE = object()
Q_SCALE = (A_DIM ** -0.5) * math.log2(math.e)
VMEM_LIMIT = 56 * 1024 * 1024


def _cparams(n_axes):
    return pltpu.CompilerParams(dimension_semantics=("arbitrary",) * n_axes,
                                vmem_limit_bytes=VMEM_LIMIT)


def _dot(a, b):
    return jnp.dot(a, b, preferred_element_type=F32)


def _dot_nt(a, b):
    return lax.dot_general(a, b, (((1,), (1,)), ((), ())), preferred_element_type=F32)


def _split(a):
    hi = a.astype(BF16)
    lo = (a - hi.astype(F32)).astype(BF16)
    return hi, lo


def _rms_rows(x, w):
    ms = jnp.mean(x * x, axis=-1, keepdims=True)
    return x * lax.rsqrt(ms + EPS) * w


def _group_sumsq(z, gshift):
    r = lax.broadcasted_iota(jnp.int32, (MXU_DIM, MXU_DIM), 0) >> gshift
    c = lax.broadcasted_iota(jnp.int32, (MXU_DIM, MXU_DIM), 1) >> gshift
    blk = (r == c).astype(BF16)
    outs = []
    for s in range(z.shape[1] // MXU_DIM):
        zs = z[:, s * MXU_DIM:(s + 1) * MXU_DIM]
        outs.append(_dot((zs * zs).astype(BF16), blk))
    return jnp.concatenate(outs, axis=1)


def _silu(x):
    return x * jax.nn.sigmoid(x)


def _inproj_attn_kernel(x_ref, ln_ref, w_ref, wkt_ref, qn_ref, kn_ref, knc_ref, q_ref, k_ref, v_ref,
                        h_scr, *, k_transposed):
    j = pl.program_id(1)

    @pl.when(j == 0)
    def _():
        h = _rms_rows(x_ref[...], ln_ref[...]).astype(BF16)
        h_scr[...] = h
        z = _dot(h, w_ref[...])
        n = z * lax.rsqrt(_group_sumsq(z, 6) * (1.0 / A_DIM) + EPS) * qn_ref[...]
        q_ref[...] = (n * Q_SCALE).astype(BF16)

    @pl.when(j == 1)
    def _():
        if k_transposed:
            zt = _dot_nt(wkt_ref[...], h_scr[...])
            z3 = zt.reshape(A_QK // A_DIM, A_DIM, zt.shape[1])
            ms = jnp.mean(z3 * z3, axis=1, keepdims=True)
            kn3 = knc_ref[...].reshape(A_QK // A_DIM, A_DIM, 1)
            k_ref[0] = (z3 * lax.rsqrt(ms + EPS) * kn3).reshape(zt.shape)
        else:
            z = _dot(h_scr[...], w_ref[...])
            k_ref[...] = z * lax.rsqrt(_group_sumsq(z, 6) * (1.0 / A_DIM) + EPS) * kn_ref[...]

    @pl.when(j == 2)
    def _():
        v_ref[...] = _dot(h_scr[...], w_ref[...])


def _inproj_attn(x2, ln_w, w_qkv, w_kt, qn_row, kn_row, kn_col, tm, seq_len, k_transposed):
    n = x2.shape[0]
    row = lambda i, j: (0, 0)
    if k_transposed:
        tiles_per_seq = seq_len // tm
        k_spec = pl.BlockSpec((1, A_QK, tm), lambda i, j: (i // tiles_per_seq, 0, i % tiles_per_seq))
        k_shape = jax.ShapeDtypeStruct((n // seq_len, A_QK, seq_len), F32)
    else:
        k_spec = pl.BlockSpec((tm, A_QK), lambda i, j: (i, 0))
        k_shape = jax.ShapeDtypeStruct((n, A_QK), F32)
    return pl.pallas_call(
        functools.partial(_inproj_attn_kernel, k_transposed=k_transposed),
        grid=(n // tm, 3),
        in_specs=[
            pl.BlockSpec((tm, D_MODEL), lambda i, j: (i, 0)),
            pl.BlockSpec((1, D_MODEL), row),
            pl.BlockSpec((D_MODEL, A_QK), lambda i, j: (0, j)),
            pl.BlockSpec((A_QK, D_MODEL), row, pipeline_mode=pl.Buffered(1)),
            pl.BlockSpec((1, A_QK), row),
            pl.BlockSpec((1, A_QK), row),
            pl.BlockSpec((A_QK, 1), row),
        ],
        out_specs=[
            pl.BlockSpec((tm, A_QK), lambda i, j: (i, 0)),
            k_spec,
            pl.BlockSpec((tm, A_V), lambda i, j: (i, 0)),
        ],
        out_shape=[
            jax.ShapeDtypeStruct((n, A_QK), BF16),
            k_shape,
            jax.ShapeDtypeStruct((n, A_V), F32),
        ],
        scratch_shapes=[pltpu.VMEM((tm, D_MODEL), BF16)],
        compiler_params=_cparams(2),
        name="inproj_attn",
    )(x2, ln_w, w_qkv, w_kt, qn_row, kn_row, kn_col)


def _inproj_delta_kernel(x_ref, ln_ref, w_ref, wbg_ref, cw_ref, cbuf_ref, alog_ref, dtb_ref,
                         dqkv_ref, bg_ref, cst_ref, h_scr, zbuf, carry, *, tiles_per_seq):
    i = pl.program_id(0)
    j = pl.program_id(1)
    tm = x_ref.shape[0]

    @pl.when(j == 0)
    def _():
        h = _rms_rows(x_ref[...], ln_ref[...]).astype(BF16)
        h_scr[...] = h
        zb = _dot(h, wbg_ref[...])
        lane = lax.broadcasted_iota(jnp.int32, zb.shape, 1)
        a = zb + dtb_ref[...]
        softplus = jnp.maximum(a, 0.0) + jnp.log1p(jnp.exp(-jnp.abs(a)))
        g = -jnp.exp(alog_ref[...]) * softplus
        bg_ref[...] = jnp.where(lane < B_HEADS, jax.nn.sigmoid(zb),
                                jnp.where(lane < 2 * B_HEADS, g, 0.0))

    z = _dot(h_scr[...], w_ref[...])
    first = (i % tiles_per_seq) == 0
    prev = jnp.where(first, cbuf_ref[0], carry[j])
    zbuf[0:SUBLANES, :] = prev
    zbuf[SUBLANES:, :] = z
    tail = z[tm - SUBLANES:, :]
    carry[j] = tail
    cst_ref[0] = tail

    cw = cw_ref[...]
    y = z * cw[CONV_W - 1:CONV_W, :]
    for s in range(1, CONV_W):
        y = y + zbuf[pl.ds(SUBLANES - s, tm), :] * cw[CONV_W - 1 - s:CONV_W - s, :]
    y = _silu(y)

    @pl.when(j < 2)
    def _():
        n = y * lax.rsqrt(_group_sumsq(y, 7) + EPS)
        dqkv_ref[...] = n * jnp.where(j == 0, B_DK ** -0.5, 1.0)

    @pl.when(j == 2)
    def _():
        dqkv_ref[...] = y


def _inproj_delta(x2, ln_w, w_d, w_bg, conv_w, cbuf8, alog_row, dtb_row, tm, seq_len):
    n = x2.shape[0]
    tiles_per_seq = seq_len // tm
    n_tiles = n // tm
    row = lambda i, j: (0, 0)
    kern = functools.partial(_inproj_delta_kernel, tiles_per_seq=tiles_per_seq)
    return pl.pallas_call(
        kern,
        grid=(n_tiles, 3),
        in_specs=[
            pl.BlockSpec((tm, D_MODEL), lambda i, j: (i, 0)),
            pl.BlockSpec((1, D_MODEL), row),
            pl.BlockSpec((D_MODEL, B_QK), lambda i, j: (0, j)),
            pl.BlockSpec((D_MODEL, LANES), row),
            pl.BlockSpec((CONV_W, B_QK), lambda i, j: (0, j)),
            pl.BlockSpec((1, SUBLANES, B_QK), lambda i, j: (i // tiles_per_seq, 0, j)),
            pl.BlockSpec((1, LANES), row),
            pl.BlockSpec((1, LANES), row),
        ],
        out_specs=[
            pl.BlockSpec((tm, B_QK), lambda i, j: (i, j)),
            pl.BlockSpec((tm, LANES), lambda i, j: (i, 0)),
            pl.BlockSpec((1, SUBLANES, B_QK), lambda i, j: (i, 0, j)),
        ],
        out_shape=[
            jax.ShapeDtypeStruct((n, CONV_CH), F32),
            jax.ShapeDtypeStruct((n, LANES), F32),
            jax.ShapeDtypeStruct((n_tiles, SUBLANES, CONV_CH), F32),
        ],
        scratch_shapes=[
            pltpu.VMEM((tm, D_MODEL), BF16),
            pltpu.VMEM((tm + SUBLANES, B_QK), F32),
            pltpu.VMEM((3, SUBLANES, B_QK), F32),
        ],
        compiler_params=_cparams(2),
        name="inproj_delta",
    )(x2, ln_w, w_d, w_bg, conv_w, cbuf8, alog_row, dtb_row)


def _lambda_value(lq1, lk1, lq2, lk2, lam_init):
    d1 = jnp.sum(lq1[...] * lk1[...], axis=-1, keepdims=True)
    d2 = jnp.sum(lq2[...] * lk2[...], axis=-1, keepdims=True)
    return jnp.exp(d1) - jnp.exp(d2) + lam_init


def _split_maps(q):
    lane = lax.broadcasted_iota(jnp.int32, q.shape, 1)
    zero = jnp.zeros_like(q)
    return jnp.where(lane < A_DIM, q, zero), jnp.where(lane >= A_DIM, q, zero)


def _attn_finish(o, sw, lam_init):
    ms = jnp.mean(o * o, axis=-1, keepdims=True)
    return (o * lax.rsqrt(ms + EPS) * sw * (1.0 - lam_init)).astype(BF16)


def _rows_to_sublanes(x, op):
    return functools.reduce(op, [x[i * SUBLANES:(i + 1) * SUBLANES] for i in range(x.shape[0] // SUBLANES)])


def _attn_prompt_kernel(q_ref, k_ref, v_ref, lq1, lk1, lq2, lk2, swc_ref, o_ref,
                        kb_scr, vt_scr, s_scr, p_scr, *, tq, lam_init):
    seq = q_ref.shape[1]
    lam = _lambda_value(lq1, lk1, lq2, lk2, lam_init)
    kb_scr[...] = k_ref[0].T.astype(BF16)
    vt_scr[0:A_VDIM, :] = v_ref[0].T.astype(BF16)
    vt_scr[A_VDIM:, :] = jnp.ones((vt_scr.shape[0] - A_VDIM, seq), BF16)
    ri = lax.broadcasted_iota(jnp.int32, (tq, 2 * tq), 0)
    ci = lax.broadcasted_iota(jnp.int32, (tq, 2 * tq), 1)
    diag_mask = (ri // CHUNK) <= ((ci & (tq - 1)) // CHUNK)
    n_q = seq // tq
    n_slots = s_scr.shape[0]

    def score_phase(qi):
        q2 = jnp.concatenate(_split_maps(q_ref[0, qi * tq:(qi + 1) * tq, :]), axis=0)
        q2t = q2.astype(F32).T.astype(BF16)
        m8 = None
        for j in range(qi + 1):
            st = _dot(kb_scr[j * tq:(j + 1) * tq, :], q2t)
            if j == qi:
                st = jnp.where(diag_mask, st, NEG_BIG)
            s_scr[qi % n_slots, j * tq:(j + 1) * tq, :] = st
            part = _rows_to_sublanes(st, jnp.maximum)
            m8 = part if m8 is None else jnp.maximum(m8, part)
        return jnp.max(m8, axis=0, keepdims=True)

    def exp_phase(qi, m_row):
        for j in range(qi + 1):
            p = jnp.exp2(s_scr[qi % n_slots, j * tq:(j + 1) * tq, :] - m_row)
            p_scr[qi % n_slots, j * tq:(j + 1) * tq, :] = p.astype(BF16)

    def value_phase(qi):
        n_keys = (qi + 1) * tq
        acc_t = _dot(vt_scr[:, :n_keys], p_scr[qi % n_slots, :n_keys, :])
        o2t = acc_t[:A_VDIM] / acc_t[A_VDIM:A_VDIM + 1]
        ot = o2t[:, :tq] - lam * o2t[:, tq:]
        ms = jnp.mean(ot * ot, axis=0, keepdims=True)
        ot = ot * lax.rsqrt(ms + EPS) * swc_ref[...] * (1.0 - lam_init)
        o_ref[0, qi * tq:(qi + 1) * tq, :] = ot.T.astype(BF16)

    m_row = score_phase(0)
    for qi in range(n_q):
        exp_phase(qi, m_row)
        if qi + 1 < n_q:
            m_row = score_phase(qi + 1)
        value_phase(qi)


def _attn_prompt(q3, kt3, v3, lams, sw_col, lam_init, tq):
    b, l, _ = q3.shape
    vec = lambda bi, h: (0, 0)
    head = pl.BlockSpec((1, l, LANES), lambda bi, h: (bi, 0, h))
    kern = functools.partial(_attn_prompt_kernel, tq=tq, lam_init=lam_init)
    return pl.pallas_call(
        kern,
        grid=(b, A_HEADS),
        in_specs=[
            head, pl.BlockSpec((1, 2 * A_DIM, l), lambda bi, h: (bi, h, 0)), head,
            pl.BlockSpec((1, A_DIM), vec), pl.BlockSpec((1, A_DIM), vec),
            pl.BlockSpec((1, A_DIM), vec), pl.BlockSpec((1, A_DIM), vec),
            pl.BlockSpec((A_VDIM, 1), vec),
        ],
        out_specs=head,
        out_shape=jax.ShapeDtypeStruct((b, l, A_V), BF16),
        scratch_shapes=[
            pltpu.VMEM((l, 2 * A_DIM), BF16), pltpu.VMEM((A_VDIM + 2 * SUBLANES, l), BF16),
            pltpu.VMEM((2, l, 2 * tq), F32), pltpu.VMEM((2, l, 2 * tq), BF16),
        ],
        compiler_params=_cparams(2),
        name="attn_prompt",
    )(q3, kt3, v3, *lams, sw_col)


def _attn_sample_kernel(q_ref, kc_ref, vc_ref, kn_ref, vn_ref, lq1, lk1, lq2, lk2, sw_ref, o_ref,
                        *, lam_init):
    h = pl.program_id(1)
    l = q_ref.shape[1]
    past = kc_ref.shape[3]
    lam = _lambda_value(lq1, lk1, lq2, lk2, lam_init)
    q2 = jnp.concatenate(_split_maps(q_ref[0]), axis=0)
    vc = vc_ref[0, pl.ds(h, past, stride=A_HEADS), :].astype(BF16)
    sc = _dot(q2, kc_ref[0, 0].astype(BF16))
    sn = _dot_nt(q2, kn_ref[0].astype(BF16))
    mx = jnp.maximum(jnp.max(sc, axis=-1, keepdims=True), jnp.max(sn, axis=-1, keepdims=True))
    pc = jnp.exp2(sc - mx)
    pn = jnp.exp2(sn - mx)
    den = jnp.sum(pc, axis=-1, keepdims=True) + jnp.sum(pn, axis=-1, keepdims=True)
    o2 = (_dot(pc.astype(BF16), vc) + _dot(pn.astype(BF16), vn_ref[0].astype(BF16))) / den
    o_ref[0] = _attn_finish(o2[:l] - lam * o2[l:], sw_ref[...], lam_init)


def _attn_sample(q3, kct, vc2, kn3, vn3, lams, sw_row, lam_init):
    b, l, _ = q3.shape
    past = kct.shape[3]
    vec = lambda bi, h: (0, 0)
    blk = lambda rows: pl.BlockSpec((1, rows, LANES), lambda bi, h: (bi, 0, h))
    kern = functools.partial(_attn_sample_kernel, lam_init=lam_init)
    return pl.pallas_call(
        kern,
        grid=(b, A_HEADS),
        in_specs=[
            blk(l),
            pl.BlockSpec((1, 1, 2 * A_DIM, past), lambda bi, h: (bi, h, 0, 0)),
            pl.BlockSpec((1, past * A_HEADS, A_VDIM), lambda bi, h: (bi, 0, 0)),
            blk(l), blk(l),
            pl.BlockSpec((1, A_DIM), vec), pl.BlockSpec((1, A_DIM), vec),
            pl.BlockSpec((1, A_DIM), vec), pl.BlockSpec((1, A_DIM), vec),
            pl.BlockSpec((1, A_VDIM), vec),
        ],
        out_specs=blk(l),
        out_shape=jax.ShapeDtypeStruct((b, l, A_V), BF16),
        compiler_params=_cparams(2),
        name="attn_sample",
    )(q3, kct, vc2, kn3, vn3, *lams, sw_row)


def _delta_kernel(q_ref, k_ref, v_ref, bg_ref, s0_ref, nw_ref, o_ref, sout_ref, s_scr,
                  *, c, nblk, hb, gb):
    t = pl.program_id(2)

    @pl.when(t == 0)
    def _():
        s_scr[...] = s0_ref[0]

    bg = bg_ref[0]
    lane = lax.broadcasted_iota(jnp.int32, bg.shape, 1)
    bg_t = bg.T
    sub = lax.broadcasted_iota(jnp.int32, bg_t.shape, 0)
    ri = lax.broadcasted_iota(jnp.int32, (c, c), 0)
    ci = lax.broadcasted_iota(jnp.int32, (c, c), 1)
    causal = ri >= ci
    strict = ri > ci
    ltri = causal.astype(BF16)
    utri = (ri <= ci).astype(BF16)
    eye = jnp.where(ri == ci, 1.0, 0.0)

    gates = []
    for hh in range(hb):
        h = pl.program_id(1) * hb + hh
        gates.append((
            jnp.sum(jnp.where(lane == h, bg, 0.0), axis=-1, keepdims=True),
            jnp.sum(jnp.where(lane == h + B_HEADS, bg, 0.0), axis=-1, keepdims=True),
            jnp.sum(jnp.where(sub == h + B_HEADS, bg_t, 0.0), axis=0, keepdims=True)))
    pre = {}
    states = [s_scr[hh] for hh in range(hb)]

    def prepare(blocks):
        items = [(hh, blk) for hh in range(hb) for blk in blocks]
        for hh, blk in items:
            rows = slice(blk * c, (blk + 1) * c)
            cols = slice(hh * LANES, (hh + 1) * LANES)
            bc = gates[hh][0][rows, :]
            q = q_ref[0, rows, cols]
            k = k_ref[0, rows, cols]
            v = v_ref[0, rows, cols]
            ghi, glo = _split(jnp.broadcast_to(gates[hh][1][rows, :], (c, LANES)))
            g_cum = _dot(ltri, ghi) + _dot(ltri, glo)
            rhi, rlo = _split(jnp.broadcast_to(gates[hh][2][:, rows], (c, c)))
            g_cum_row = _dot(rhi, utri) + _dot(rlo, utri)
            diff = jnp.where(causal, g_cum[:, :c] - g_cum_row, 0.0)
            decay = jnp.where(causal, jnp.exp(diff), 0.0)
            kb = k.astype(BF16)
            m = jnp.where(strict, bc * _dot_nt(kb, kb) * decay, 0.0)
            pre[hh, blk] = dict(q=q, k=k, v=v, bc=bc, g_cum=g_cum, decay=decay, kb=kb)
            pre[hh, blk]["pq"] = (eye - m, m)
        yield
        for _ in range(int(math.log2(c)) - 1):
            for it in items:
                pm, qm = pre[it]["pq"]
                qb = qm.astype(BF16)
                pre[it]["pq"] = (pm, _dot(qb, qb))
            yield
            for it in items:
                pm, qm = pre[it]["pq"]
                pre[it]["pq"] = (pm + _dot(pm.astype(BF16), qm.astype(BF16)), qm)
            yield
        for it in items:
            d = pre[it]
            e_g = jnp.exp(d["g_cum"])
            rhs = jnp.concatenate([d["v"] * d["bc"], d["k"] * (d["bc"] * e_g)], axis=1)
            sol = _dot(d.pop("pq")[0].astype(BF16), rhs.astype(BF16))
            g_last = d["g_cum"][c - 1:c, :]
            d.update(
                u=sol[:, :B_DV], wb=sol[:, B_DV:].astype(BF16),
                a_loc=jnp.where(causal, _dot_nt(d["q"].astype(BF16), d["kb"]) * d["decay"],
                                0.0).astype(BF16),
                q_dec=(d["q"] * e_g).astype(BF16),
                k_tail_t=(d["k"] * jnp.exp(g_last - d["g_cum"])).T.astype(BF16),
                g_tail=jnp.exp(g_last))
        yield

    def scan(blocks):
        for blk in blocks:
            vbs, o_state = [], []
            for hh in range(hb):
                d = pre[hh, blk]
                sb = states[hh].astype(BF16)
                vbs.append((d["u"] - _dot(d["wb"], sb)).astype(BF16))
                o_state.append(_dot(d["q_dec"], sb))
            yield
            for hh in range(hb):
                d = pre.pop((hh, blk))
                o = o_state[hh] + _dot(d["a_loc"], vbs[hh])
                states[hh] = states[hh] * d["g_tail"] + _dot(d["k_tail_t"], vbs[hh])
                o_ref[0, blk * c:(blk + 1) * c, hh * LANES:(hh + 1) * LANES] = (
                    _rms_rows(o, nw_ref[...]).astype(BF16))
            yield

    def run(*stage_iters):
        live = list(stage_iters)
        while live:
            live = [it for it in live if next(it, _DONE) is not _DONE]

    groups = [list(range(g, min(g + gb, nblk))) for g in range(0, nblk, gb)]
    run(prepare(groups[0]))
    for g in range(1, len(groups)):
        run(prepare(groups[g]), scan(groups[g - 1]))
    run(scan(groups[-1]))
    for hh in range(hb):
        s_scr[hh] = states[hh]

    @pl.when(t == pl.num_programs(2) - 1)
    def _():
        for hh in range(hb):
            sout_ref[0, hh] = states[hh]


def _delta(dqkv3, bg3, s0, nw_row, c, tl, hb, gb):
    b, l, _ = dqkv3.shape
    n_hg = B_HEADS // hb
    kern = functools.partial(_delta_kernel, c=c, nblk=tl // c, hb=hb, gb=gb)
    lane_blk = lambda off: pl.BlockSpec((1, tl, hb * LANES), lambda bi, h, t: (bi, t, h + off))
    state = pl.BlockSpec((1, hb, B_DK, B_DV), lambda bi, h, t: (bi, h, 0, 0))
    return pl.pallas_call(
        kern,
        grid=(b, n_hg, l // tl),
        in_specs=[
            lane_blk(0), lane_blk(n_hg), lane_blk(2 * n_hg),
            pl.BlockSpec((1, tl, LANES), lambda bi, h, t: (bi, t, 0)),
            state,
            pl.BlockSpec((1, B_DV), lambda bi, h, t: (0, 0)),
        ],
        out_specs=[lane_blk(0), state],
        out_shape=[
            jax.ShapeDtypeStruct((b, l, B_V), BF16),
            jax.ShapeDtypeStruct((b, B_HEADS, B_DK, B_DV), F32),
        ],
        scratch_shapes=[pltpu.VMEM((hb, B_DK, B_DV), F32)],
        compiler_params=_cparams(3),
        name="delta_rule",
    )(dqkv3, dqkv3, dqkv3, bg3, s0, nw_row)


def _merge_kernel(x_ref, oa_ref, ob_ref, ln_ref, wdg, wga, wgb, wba, wbb, wout, y_ref):
    x = x_ref[...]
    h = _rms_rows(x, ln_ref[...]).astype(BF16)
    ob = ob_ref[...].astype(F32) * _silu(_dot(h, wdg[...]))
    branch_b = _dot(ob.astype(BF16), wbb[...])
    branch_a = _dot(oa_ref[...], wba[...])
    merged = (jax.nn.sigmoid(_dot(h, wga[...])) * branch_a
              + jax.nn.sigmoid(_dot(h, wgb[...])) * branch_b)
    y_ref[...] = x + _dot(merged.astype(BF16), wout[...])


def _merge(x2, oa2, ob2, ln_w, wdg, wga, wgb, wba, wbb, wout, tm):
    n = x2.shape[0]
    tile = pl.BlockSpec((tm, D_MODEL), lambda i: (i, 0))
    wspec = pl.BlockSpec((D_MODEL, D_MODEL), lambda i: (0, 0), pipeline_mode=pl.Buffered(1))
    return pl.pallas_call(
        _merge_kernel,
        grid=(n // tm,),
        in_specs=[tile, tile, tile, pl.BlockSpec((1, D_MODEL), lambda i: (0, 0))] + [wspec] * 6,
        out_specs=tile,
        out_shape=jax.ShapeDtypeStruct((n, D_MODEL), F32),
        compiler_params=_cparams(1),
        name="merge_outproj",
    )(x2, oa2, ob2, ln_w, wdg, wga, wgb, wba, wbb, wout)


def _ffn_kernel(x_ref, ln_ref, wg_ref, wu_ref, wd_ref, y_ref, h_scr):
    cidx = pl.program_id(1)

    @pl.when(cidx == 0)
    def _():
        x = x_ref[...]
        h_scr[...] = _rms_rows(x, ln_ref[...]).astype(BF16)
        y_ref[...] = x

    h = h_scr[...]
    act = _silu(_dot(h, wg_ref[...])) * _dot(h, wu_ref[...])
    y_ref[...] += _dot(act.astype(BF16), wd_ref[...])


def _ffn(x2, ln_w, w_gate_up, w_down, tm, n_chunks):
    n = x2.shape[0]
    fc = D_FF // n_chunks
    return pl.pallas_call(
        _ffn_kernel,
        grid=(n // tm, n_chunks),
        in_specs=[
            pl.BlockSpec((tm, D_MODEL), lambda i, c: (i, 0)),
            pl.BlockSpec((1, D_MODEL), lambda i, c: (0, 0)),
            pl.BlockSpec((D_MODEL, fc), lambda i, c: (0, c)),
            pl.BlockSpec((D_MODEL, fc), lambda i, c: (0, c + n_chunks)),
            pl.BlockSpec((fc, D_MODEL), lambda i, c: (c, 0)),
        ],
        out_specs=pl.BlockSpec((tm, D_MODEL), lambda i, c: (i, 0)),
        out_shape=jax.ShapeDtypeStruct((n, D_MODEL), F32),
        scratch_shapes=[pltpu.VMEM((tm, D_MODEL), BF16)],
        compiler_params=_cparams(2),
        name="swiglu_ffn",
    )(x2, ln_w, w_gate_up, w_gate_up, w_down)


def _pick_tile(n, target):
    t = min(n, target)
    while n % t:
        t //= 2
    return t


def _prep_weights(ln1_w, w_in, q_norm_w, k_norm_w, lambda_q1, lambda_k1, lambda_q2, lambda_k2,
                  subln_w, w_branch_a, conv_w, a_log, dt_bias, delta_norm_w, w_branch_b,
                  w_out, ln2_w, w_gate_up, w_down):
    o_d = 2 * A_QK + A_V
    o_dg = o_d + CONV_CH
    o_bg = o_dg + B_V
    o_ga = o_bg + 2 * B_HEADS
    o_gb = o_ga + D_MODEL
    pad_lanes = lambda v: jnp.pad(v.astype(F32), (B_HEADS, LANES - 2 * B_HEADS))[None, :]
    return dict(
        ln1=ln1_w[None, :], ln2=ln2_w[None, :],
        w_qkv=w_in[:, :o_d].astype(BF16),
        w_d=w_in[:, o_d:o_dg].astype(BF16),
        w_dg=w_in[:, o_dg:o_bg].astype(BF16),
        w_bg=jnp.pad(w_in[:, o_bg:o_ga], ((0, 0), (0, LANES - 2 * B_HEADS))).astype(BF16),
        w_ga=w_in[:, o_ga:o_gb].astype(BF16),
        w_gb=w_in[:, o_gb:].astype(BF16),
        qn=jnp.tile(q_norm_w, A_QK // A_DIM)[None, :],
        kn=jnp.tile(k_norm_w, A_QK // A_DIM)[None, :],
        kn_col=jnp.tile(k_norm_w, A_QK // A_DIM)[:, None],
        w_kt=w_in[:, A_QK:2 * A_QK].T.astype(BF16),
        lams=tuple(v[None, :] for v in (lambda_q1, lambda_k1, lambda_q2, lambda_k2)),
        subln=subln_w[None, :], subln_col=subln_w[:, None],
        w_ba=w_branch_a.astype(BF16), w_bb=w_branch_b.astype(BF16),
        conv_w=conv_w, alog=pad_lanes(a_log), dtb=pad_lanes(dt_bias),
        dnorm=delta_norm_w[None, :],
        w_out=w_out.astype(BF16),
        w_gate_up=w_gate_up.astype(BF16), w_down=w_down.astype(BF16),
    )


def _layer(x, past_k, past_v, s0, conv_buf, lam_init, p):
    b, l, _ = x.shape
    n = b * l
    x2 = x.reshape(n, D_MODEL)

    prompt = past_k is None
    tm_a = _pick_tile(l, 1024) if prompt else _pick_tile(n, 1024)
    q2, k2, v2 = _inproj_attn(x2, p["ln1"], p["w_qkv"], p["w_kt"], p["qn"], p["kn"], p["kn_col"],
                              tm_a, l, prompt)

    tm_d = _pick_tile(l, 1024)
    cbuf8 = jnp.pad(conv_buf, ((0, 0), (SUBLANES - (CONV_W - 1), 0), (0, 0)))
    dqkv, bg, cst = _inproj_delta(x2, p["ln1"], p["w_d"], p["w_bg"], p["conv_w"], cbuf8,
                                  p["alog"], p["dtb"], tm_d, l)
    tiles_per_seq = l // tm_d
    new_conv = cst[tiles_per_seq - 1::tiles_per_seq, SUBLANES - (CONV_W - 1):, :]

    q3 = q2.reshape(b, l, A_QK)
    v3 = v2.reshape(b, l, A_V)
    if prompt:
        oa = _attn_prompt(q3, k2, v3, p["lams"], p["subln_col"], lam_init, _pick_tile(l, 256))
        k_out = jnp.transpose(k2.reshape(b, A_HEADS, 2, A_DIM, l), (0, 4, 1, 2, 3))
    else:
        past = past_k.shape[1]
        k3 = k2.reshape(b, l, A_QK)
        kct = jnp.transpose(past_k, (0, 2, 3, 4, 1)).reshape(b, A_HEADS, 2 * A_DIM, past)
        oa = _attn_sample(q3, kct, past_v.reshape(b, past * A_HEADS, A_VDIM), k3, v3, p["lams"],
                          p["subln"], lam_init)
        k_out = k3.reshape(b, l, A_HEADS, 2, A_DIM)

    c = DELTA_BLOCK if l % DELTA_BLOCK == 0 else l
    tl = c * max(1, min(8, l // c))
    ob, s_new = _delta(dqkv.reshape(b, l, CONV_CH), bg.reshape(b, l, LANES), s0, p["dnorm"], c, tl,
                       B_HEADS, 4)

    x1 = _merge(x2, oa.reshape(n, A_V), ob.reshape(n, B_V), p["ln1"], p["w_dg"], p["w_ga"],
                p["w_gb"], p["w_ba"], p["w_bb"], p["w_out"], _pick_tile(n, 512))
    y = _ffn(x1, p["ln2"], p["w_gate_up"], p["w_down"], _pick_tile(n, 512), 2)
    return (y.reshape(b, l, D_MODEL), k_out, v3.reshape(b, l, A_HEADS, A_VDIM), s_new, new_conv)


def kernel(x_prompt, x_sample, cache_k, cache_v, state_delta, state_conv, ln1_w, w_in, q_norm_w,
           k_norm_w, lambda_q1, lambda_k1, lambda_q2, lambda_k2, subln_w, w_branch_a, conv_w,
           a_log, dt_bias, delta_norm_w, w_branch_b, w_out, ln2_w, w_gate_up, w_down):
    depth = ln1_w.shape[0]
    xp, xs = x_prompt, x_sample
    outs_p, outs_s = [], []
    for layer in range(depth):
        lam_init = 0.8 - 0.6 * math.exp(-0.3 * layer)
        p = _prep_weights(*(w[layer] for w in (
            ln1_w, w_in, q_norm_w, k_norm_w, lambda_q1, lambda_k1, lambda_q2, lambda_k2, subln_w,
            w_branch_a, conv_w, a_log, dt_bias, delta_norm_w, w_branch_b, w_out, ln2_w,
            w_gate_up, w_down)))
        bp = xp.shape[0]
        zero_conv = jnp.zeros((bp, CONV_W - 1, CONV_CH), xp.dtype)
        zero_s = jnp.zeros((bp, B_HEADS, B_DK, B_DV), F32)
        xp, *rest_p = _layer(xp, None, None, zero_s, zero_conv, lam_init, p)
        xs, *rest_s = _layer(xs, cache_k[layer], cache_v[layer], state_delta[layer],
                             state_conv[layer], lam_init, p)
        outs_p.append(rest_p)
        outs_s.append(rest_s)
    stack = lambda outs, idx: jnp.stack([o[idx] for o in outs])
    return (xp, xs,
            stack(outs_p, 0), stack(outs_p, 1), stack(outs_p, 2), stack(outs_p, 3),
            stack(outs_s, 0), stack(outs_s, 1), stack(outs_s, 2), stack(outs_s, 3))
```

```python
import functools
import math

import jax
import jax.numpy as jnp
from jax import lax
from jax.experimental import pallas as pl
from jax.experimental.pallas import tpu as pltpu

F32 = jnp.float32
BF16 = jnp.bfloat16

D_MODEL = 1024
CHUNK = 64
A_HEADS = 8
A_DIM = 64
A_VDIM = 2 * A_DIM
B_HEADS = 8
B_DK = 128
B_DV = 128
CONV_W = 4
DELTA_BLOCK = 64
D_FF = -(-8 * D_MODEL // (3 * 256)) * 256
EPS = 1e-6

A_QK = A_HEADS * 2 * A_DIM
A_V = A_HEADS * A_VDIM
B_QK = B_HEADS * B_DK
B_V = B_HEADS * B_DV
CONV_CH = 2 * B_QK + B_V

LANES = 128
SUBLANES = 8
MXU_DIM = 256
NEG_BIG = -1e30
_DONE = object()
Q_SCALE = (A_DIM ** -0.5) * math.log2(math.e)
VMEM_LIMIT = 56 * 1024 * 1024


def _cparams(n_axes):
    return pltpu.CompilerParams(dimension_semantics=("arbitrary",) * n_axes,
                                vmem_limit_bytes=VMEM_LIMIT)


def _dot(a, b):
    return jnp.dot(a, b, preferred_element_type=F32)


def _dot_nt(a, b):
    return lax.dot_general(a, b, (((1,), (1,)), ((), ())), preferred_element_type=F32)


def _split(a):
    hi = a.astype(BF16)
    lo = (a - hi.astype(F32)).astype(BF16)
    return hi, lo


def _rms_rows(x, w):
    ms = jnp.mean(x * x, axis=-1, keepdims=True)
    return x * lax.rsqrt(ms + EPS) * w


def _group_sumsq(z, gshift):
    r = lax.broadcasted_iota(jnp.int32, (MXU_DIM, MXU_DIM), 0) >> gshift
    c = lax.broadcasted_iota(jnp.int32, (MXU_DIM, MXU_DIM), 1) >> gshift
    blk = (r == c).astype(BF16)
    outs = []
    for s in range(z.shape[1] // MXU_DIM):
        zs = z[:, s * MXU_DIM:(s + 1) * MXU_DIM]
        outs.append(_dot((zs * zs).astype(BF16), blk))
    return jnp.concatenate(outs, axis=1)


def _sigmoid(x):
    return 0.5 * jnp.tanh(0.5 * x) + 0.5


def _silu(x):
    half = 0.5 * x
    return half * jnp.tanh(half) + half


def _inproj_attn_kernel(x_ref, ln_ref, w_ref, wkt_ref, qn_ref, kn_ref, knc_ref, q_ref, k_ref, v_ref,
                        h_scr, *, k_transposed):
    j = pl.program_id(1)

    @pl.when(j == 0)
    def _():
        h = _rms_rows(x_ref[...], ln_ref[...]).astype(BF16)
        h_scr[...] = h
        z = _dot(h, w_ref[...])
        n = z * lax.rsqrt(_group_sumsq(z, 6) * (1.0 / A_DIM) + EPS) * qn_ref[...]
        q_ref[...] = (n * Q_SCALE).astype(BF16)

    @pl.when(j == 1)
    def _():
        if k_transposed:
            zt = _dot_nt(wkt_ref[...], h_scr[...])
            z3 = zt.reshape(A_QK // A_DIM, A_DIM, zt.shape[1])
            ms = jnp.mean(z3 * z3, axis=1, keepdims=True)
            kn3 = knc_ref[...].reshape(A_QK // A_DIM, A_DIM, 1)
            k_ref[0] = (z3 * lax.rsqrt(ms + EPS) * kn3).reshape(zt.shape)
        else:
            z = _dot(h_scr[...], w_ref[...])
            k_ref[...] = z * lax.rsqrt(_group_sumsq(z, 6) * (1.0 / A_DIM) + EPS) * kn_ref[...]

    @pl.when(j == 2)
    def _():
        v_ref[...] = _dot(h_scr[...], w_ref[...])


def _inproj_attn(x2, ln_w, w_qkv, w_kt, qn_row, kn_row, kn_col, tm, seq_len, k_transposed):
    n = x2.shape[0]
    row = lambda i, j: (0, 0)
    if k_transposed:
        tiles_per_seq = seq_len // tm
        k_spec = pl.BlockSpec((1, A_QK, tm), lambda i, j: (i // tiles_per_seq, 0, i % tiles_per_seq))
        k_shape = jax.ShapeDtypeStruct((n // seq_len, A_QK, seq_len), F32)
    else:
        k_spec = pl.BlockSpec((tm, A_QK), lambda i, j: (i, 0))
        k_shape = jax.ShapeDtypeStruct((n, A_QK), F32)
    return pl.pallas_call(
        functools.partial(_inproj_attn_kernel, k_transposed=k_transposed),
        grid=(n // tm, 3),
        in_specs=[
            pl.BlockSpec((tm, D_MODEL), lambda i, j: (i, 0)),
            pl.BlockSpec((1, D_MODEL), row),
            pl.BlockSpec((D_MODEL, A_QK), lambda i, j: (0, j)),
            pl.BlockSpec((A_QK, D_MODEL), row, pipeline_mode=pl.Buffered(1)),
            pl.BlockSpec((1, A_QK), row),
            pl.BlockSpec((1, A_QK), row),
            pl.BlockSpec((A_QK, 1), row),
        ],
        out_specs=[
            pl.BlockSpec((tm, A_QK), lambda i, j: (i, 0)),
            k_spec,
            pl.BlockSpec((tm, A_V), lambda i, j: (i, 0)),
        ],
        out_shape=[
            jax.ShapeDtypeStruct((n, A_QK), BF16),
            k_shape,
            jax.ShapeDtypeStruct((n, A_V), F32),
        ],
        scratch_shapes=[pltpu.VMEM((tm, D_MODEL), BF16)],
        compiler_params=_cparams(2),
        name="inproj_attn",
    )(x2, ln_w, w_qkv, w_kt, qn_row, kn_row, kn_col)


def _inproj_delta_kernel(x_ref, ln_ref, w_ref, wbg_ref, cw_ref, cbuf_ref, alog_ref, dtb_ref,
                         dqkv_ref, bg_ref, cst_ref, h_scr, carry, *, tiles_per_seq):
    i = pl.program_id(0)
    j = pl.program_id(1)
    tm = x_ref.shape[0]

    @pl.when(j == 0)
    def _():
        h = _rms_rows(x_ref[...], ln_ref[...]).astype(BF16)
        h_scr[...] = h
        zb = _dot(h, wbg_ref[...])
        lane = lax.broadcasted_iota(jnp.int32, zb.shape, 1)
        a = zb + dtb_ref[...]
        softplus = jnp.maximum(a, 0.0) + jnp.log1p(jnp.exp(-jnp.abs(a)))
        g = -jnp.exp(alog_ref[...]) * softplus
        bg_ref[...] = jnp.where(lane < B_HEADS, _sigmoid(zb),
                                jnp.where(lane < 2 * B_HEADS, g, 0.0))

    z = _dot(h_scr[...], w_ref[...])
    first = (i % tiles_per_seq) == 0
    prev = jnp.where(first, cbuf_ref[0], carry[j])
    tail = z[tm - SUBLANES:, :]
    carry[j] = tail
    cst_ref[0] = tail

    cw = cw_ref[...]
    zfull = jnp.concatenate([prev, z], axis=0)
    y = z * cw[CONV_W - 1:CONV_W, :]
    for s in range(1, CONV_W):
        shifted = pltpu.roll(zfull, s, axis=0)[SUBLANES:, :]
        y = y + shifted * cw[CONV_W - 1 - s:CONV_W - s, :]
    y = _silu(y)

    @pl.when(j < 2)
    def _():
        n = y * lax.rsqrt(_group_sumsq(y, 7) + EPS)
        dqkv_ref[...] = n * jnp.where(j == 0, B_DK ** -0.5, 1.0)

    @pl.when(j == 2)
    def _():
        dqkv_ref[...] = y


def _inproj_delta(x2, ln_w, w_d, w_bg, conv_w, cbuf8, alog_row, dtb_row, tm, seq_len):
    n = x2.shape[0]
    tiles_per_seq = seq_len // tm
    n_tiles = n // tm
    row = lambda i, j: (0, 0)
    kern = functools.partial(_inproj_delta_kernel, tiles_per_seq=tiles_per_seq)
    return pl.pallas_call(
        kern,
        grid=(n_tiles, 3),
        in_specs=[
            pl.BlockSpec((tm, D_MODEL), lambda i, j: (i, 0)),
            pl.BlockSpec((1, D_MODEL), row),
            pl.BlockSpec((D_MODEL, B_QK), lambda i, j: (0, j)),
            pl.BlockSpec((D_MODEL, LANES), row),
            pl.BlockSpec((CONV_W, B_QK), lambda i, j: (0, j)),
            pl.BlockSpec((1, SUBLANES, B_QK), lambda i, j: (i // tiles_per_seq, 0, j)),
            pl.BlockSpec((1, LANES), row),
            pl.BlockSpec((1, LANES), row),
        ],
        out_specs=[
            pl.BlockSpec((tm, B_QK), lambda i, j: (i, j)),
            pl.BlockSpec((tm, LANES), lambda i, j: (i, 0)),
            pl.BlockSpec((1, SUBLANES, B_QK), lambda i, j: (i, 0, j)),
        ],
        out_shape=[
            jax.ShapeDtypeStruct((n, CONV_CH), F32),
            jax.ShapeDtypeStruct((n, LANES), F32),
            jax.ShapeDtypeStruct((n_tiles, SUBLANES, CONV_CH), F32),
        ],
        scratch_shapes=[
            pltpu.VMEM((tm, D_MODEL), BF16),
            pltpu.VMEM((3, SUBLANES, B_QK), F32),
        ],
        compiler_params=_cparams(2),
        name="inproj_delta",
    )(x2, ln_w, w_d, w_bg, conv_w, cbuf8, alog_row, dtb_row)


def _lambda_value(lq1, lk1, lq2, lk2, lam_init):
    d1 = jnp.sum(lq1[...] * lk1[...], axis=-1, keepdims=True)
    d2 = jnp.sum(lq2[...] * lk2[...], axis=-1, keepdims=True)
    return jnp.exp(d1) - jnp.exp(d2) + lam_init


def _split_maps(q):
    lane = lax.broadcasted_iota(jnp.int32, q.shape, 1)
    zero = jnp.zeros_like(q)
    return jnp.where(lane < A_DIM, q, zero), jnp.where(lane >= A_DIM, q, zero)


def _attn_finish(o, sw, lam_init):
    ms = jnp.mean(o * o, axis=-1, keepdims=True)
    return (o * lax.rsqrt(ms + EPS) * sw * (1.0 - lam_init)).astype(BF16)


def _rows_to_sublanes(x, op):
    return functools.reduce(op, [x[i * SUBLANES:(i + 1) * SUBLANES] for i in range(x.shape[0] // SUBLANES)])


def _attn_prompt_kernel(q_ref, k_ref, v_ref, lq1, lk1, lq2, lk2, swc_ref, o_ref,
                        kb_scr, vt_scr, s_scr, p_scr, *, tq, lam_init):
    seq = q_ref.shape[1]
    lam = _lambda_value(lq1, lk1, lq2, lk2, lam_init)
    kb_scr[...] = k_ref[0].T.astype(BF16)
    vt_scr[0:A_VDIM, :] = v_ref[0].T.astype(BF16)
    vt_scr[A_VDIM:, :] = jnp.ones((vt_scr.shape[0] - A_VDIM, seq), BF16)
    ri = lax.broadcasted_iota(jnp.int32, (tq, 2 * tq), 0)
    ci = lax.broadcasted_iota(jnp.int32, (tq, 2 * tq), 1)
    diag_mask = (ri // CHUNK) <= ((ci & (tq - 1)) // CHUNK)
    n_q = seq // tq
    n_slots = s_scr.shape[0]

    def score_phase(qi):
        q2 = jnp.concatenate(_split_maps(q_ref[0, qi * tq:(qi + 1) * tq, :]), axis=0)
        q2t = q2.astype(F32).T.astype(BF16)
        m8 = None
        for j in range(qi + 1):
            st = _dot(kb_scr[j * tq:(j + 1) * tq, :], q2t)
            if j == qi:
                st = jnp.where(diag_mask, st, NEG_BIG)
            s_scr[qi % n_slots, j * tq:(j + 1) * tq, :] = st
            part = _rows_to_sublanes(st, jnp.maximum)
            m8 = part if m8 is None else jnp.maximum(m8, part)
        return jnp.max(m8, axis=0, keepdims=True)

    def exp_phase(qi, m_row):
        for j in range(qi + 1):
            p = jnp.exp2(s_scr[qi % n_slots, j * tq:(j + 1) * tq, :] - m_row)
            p_scr[qi % n_slots, j * tq:(j + 1) * tq, :] = p.astype(BF16)

    def value_phase(qi):
        n_keys = (qi + 1) * tq
        acc_t = _dot(vt_scr[:, :n_keys], p_scr[qi % n_slots, :n_keys, :])
        o2t = acc_t[:A_VDIM] / acc_t[A_VDIM:A_VDIM + 1]
        ot = o2t[:, :tq] - lam * o2t[:, tq:]
        ms = jnp.mean(ot * ot, axis=0, keepdims=True)
        ot = ot * lax.rsqrt(ms + EPS) * swc_ref[...] * (1.0 - lam_init)
        o_ref[0, qi * tq:(qi + 1) * tq, :] = ot.T.astype(BF16)

    m_row = score_phase(0)
    for qi in range(n_q):
        exp_phase(qi, m_row)
        if qi + 1 < n_q:
            m_row = score_phase(qi + 1)
        value_phase(qi)


def _attn_prompt(q3, kt3, v3, lams, sw_col, lam_init, tq):
    b, l, _ = q3.shape
    vec = lambda bi, h: (0, 0)
    head = pl.BlockSpec((1, l, LANES), lambda bi, h: (bi, 0, h))
    kern = functools.partial(_attn_prompt_kernel, tq=tq, lam_init=lam_init)
    return pl.pallas_call(
        kern,
        grid=(b, A_HEADS),
        in_specs=[
            head, pl.BlockSpec((1, 2 * A_DIM, l), lambda bi, h: (bi, h, 0)), head,
            pl.BlockSpec((1, A_DIM), vec), pl.BlockSpec((1, A_DIM), vec),
            pl.BlockSpec((1, A_DIM), vec), pl.BlockSpec((1, A_DIM), vec),
            pl.BlockSpec((A_VDIM, 1), vec),
        ],
        out_specs=head,
        out_shape=jax.ShapeDtypeStruct((b, l, A_V), BF16),
        scratch_shapes=[
            pltpu.VMEM((l, 2 * A_DIM), BF16), pltpu.VMEM((A_VDIM + 2 * SUBLANES, l), BF16),
            pltpu.VMEM((2, l, 2 * tq), F32), pltpu.VMEM((2, l, 2 * tq), BF16),
        ],
        compiler_params=_cparams(2),
        name="attn_prompt",
    )(q3, kt3, v3, *lams, sw_col)


def _attn_sample_kernel(q_ref, kc_ref, vc_ref, kn_ref, vn_ref, lq1, lk1, lq2, lk2, sw_ref, o_ref,
                        *, lam_init):
    h = pl.program_id(1)
    l = q_ref.shape[1]
    past = kc_ref.shape[3]
    lam = _lambda_value(lq1, lk1, lq2, lk2, lam_init)
    q2 = jnp.concatenate(_split_maps(q_ref[0]), axis=0)
    vc = vc_ref[0, pl.ds(h, past, stride=A_HEADS), :].astype(BF16)
    sc = _dot(q2, kc_ref[0, 0].astype(BF16))
    sn = _dot_nt(q2, kn_ref[0].astype(BF16))
    mx = jnp.maximum(jnp.max(sc, axis=-1, keepdims=True), jnp.max(sn, axis=-1, keepdims=True))
    pc = jnp.exp2(sc - mx)
    pn = jnp.exp2(sn - mx)
    den = jnp.sum(pc, axis=-1, keepdims=True) + jnp.sum(pn, axis=-1, keepdims=True)
    o2 = (_dot(pc.astype(BF16), vc) + _dot(pn.astype(BF16), vn_ref[0].astype(BF16))) / den
    o_ref[0] = _attn_finish(o2[:l] - lam * o2[l:], sw_ref[...], lam_init)


def _attn_sample(q3, kct, vc2, kn3, vn3, lams, sw_row, lam_init):
    b, l, _ = q3.shape
    past = kct.shape[3]
    vec = lambda bi, h: (0, 0)
    blk = lambda rows: pl.BlockSpec((1, rows, LANES), lambda bi, h: (bi, 0, h))
    kern = functools.partial(_attn_sample_kernel, lam_init=lam_init)
    return pl.pallas_call(
        kern,
        grid=(b, A_HEADS),
        in_specs=[
            blk(l),
            pl.BlockSpec((1, 1, 2 * A_DIM, past), lambda bi, h: (bi, h, 0, 0)),
            pl.BlockSpec((1, past * A_HEADS, A_VDIM), lambda bi, h: (bi, 0, 0)),
            blk(l), blk(l),
            pl.BlockSpec((1, A_DIM), vec), pl.BlockSpec((1, A_DIM), vec),
            pl.BlockSpec((1, A_DIM), vec), pl.BlockSpec((1, A_DIM), vec),
            pl.BlockSpec((1, A_VDIM), vec),
        ],
        out_specs=blk(l),
        out_shape=jax.ShapeDtypeStruct((b, l, A_V), BF16),
        compiler_params=_cparams(2),
        name="attn_sample",
    )(q3, kct, vc2, kn3, vn3, *lams, sw_row)


def _delta_kernel(q_ref, k_ref, v_ref, bg_ref, s0_ref, nw_ref, o_ref, sout_ref, s_scr,
                  *, c, nblk, hb, gb):
    t = pl.program_id(2)

    @pl.when(t == 0)
    def _():
        s_scr[...] = s0_ref[0]

    bg = bg_ref[0]
    lane = lax.broadcasted_iota(jnp.int32, bg.shape, 1)
    bg_t = bg.T
    sub = lax.broadcasted_iota(jnp.int32, bg_t.shape, 0)
    ri = lax.broadcasted_iota(jnp.int32, (c, c), 0)
    ci = lax.broadcasted_iota(jnp.int32, (c, c), 1)
    causal = ri >= ci
    strict = ri > ci
    ltri = causal.astype(BF16)
    utri = (ri <= ci).astype(BF16)
    eye = jnp.where(ri == ci, 1.0, 0.0)

    gates = []
    for hh in range(hb):
        h = pl.program_id(1) * hb + hh
        gates.append((
            jnp.sum(jnp.where(lane == h, bg, 0.0), axis=-1, keepdims=True),
            jnp.sum(jnp.where(lane == h + B_HEADS, bg, 0.0), axis=-1, keepdims=True),
            jnp.sum(jnp.where(sub == h + B_HEADS, bg_t, 0.0), axis=0, keepdims=True)))
    pre = {}
    states = [s_scr[hh] for hh in range(hb)]

    def prepare(blocks):
        items = [(hh, blk) for hh in range(hb) for blk in blocks]
        for hh, blk in items:
            rows = slice(blk * c, (blk + 1) * c)
            cols = slice(hh * LANES, (hh + 1) * LANES)
            bc = gates[hh][0][rows, :]
            q = q_ref[0, rows, cols]
            k = k_ref[0, rows, cols]
            v = v_ref[0, rows, cols]
            ghi, glo = _split(jnp.broadcast_to(gates[hh][1][rows, :], (c, LANES)))
            g_cum = _dot(ltri, ghi) + _dot(ltri, glo)
            rhi, rlo = _split(jnp.broadcast_to(gates[hh][2][:, rows], (c, c)))
            g_cum_row = _dot(rhi, utri) + _dot(rlo, utri)
            diff = jnp.where(causal, g_cum[:, :c] - g_cum_row, 0.0)
            decay = jnp.where(causal, jnp.exp(diff), 0.0)
            kb = k.astype(BF16)
            m = jnp.where(strict, bc * _dot_nt(kb, kb) * decay, 0.0)
            pre[hh, blk] = dict(q=q, k=k, v=v, bc=bc, g_cum=g_cum, decay=decay, kb=kb)
            pre[hh, blk]["pq"] = (eye - m, m)
        yield
        for _ in range(int(math.log2(c)) - 1):
            for it in items:
                pm, qm = pre[it]["pq"]
                qb = qm.astype(BF16)
                pre[it]["pq"] = (pm, _dot(qb, qb))
            yield
            for it in items:
                pm, qm = pre[it]["pq"]
                pre[it]["pq"] = (pm + _dot(pm.astype(BF16), qm.astype(BF16)), qm)
            yield
        for it in items:
            d = pre[it]
            e_g = jnp.exp(d["g_cum"])
            rhs = jnp.concatenate([d["v"] * d["bc"], d["k"] * (d["bc"] * e_g)], axis=1)
            sol = _dot(d.pop("pq")[0].astype(BF16), rhs.astype(BF16))
            g_last = d["g_cum"][c - 1:c, :]
            d.update(
                u=sol[:, :B_DV], wb=sol[:, B_DV:].astype(BF16),
                a_loc=jnp.where(causal, _dot_nt(d["q"].astype(BF16), d["kb"]) * d["decay"],
                                0.0).astype(BF16),
                q_dec=(d["q"] * e_g).astype(BF16),
                k_tail_t=(d["k"] * jnp.exp(g_last - d["g_cum"])).T.astype(BF16),
                g_tail=jnp.exp(g_last))
        yield

    def scan(blocks):
        for blk in blocks:
            vbs, o_state = [], []
            for hh in range(hb):
                d = pre[hh, blk]
                sb = states[hh].astype(BF16)
                vbs.append((d["u"] - _dot(d["wb"], sb)).astype(BF16))
                o_state.append(_dot(d["q_dec"], sb))
            yield
            for hh in range(hb):
                d = pre.pop((hh, blk))
                o = o_state[hh] + _dot(d["a_loc"], vbs[hh])
                states[hh] = states[hh] * d["g_tail"] + _dot(d["k_tail_t"], vbs[hh])
                o_ref[0, blk * c:(blk + 1) * c, hh * LANES:(hh + 1) * LANES] = (
                    _rms_rows(o, nw_ref[...]).astype(BF16))
            yield

    def run(*stage_iters):
        live = list(stage_iters)
        while live:
            live = [it for it in live if next(it, _DONE) is not _DONE]

    groups = [list(range(g, min(g + gb, nblk))) for g in range(0, nblk, gb)]
    run(prepare(groups[0]))
    for g in range(1, len(groups)):
        run(prepare(groups[g]), scan(groups[g - 1]))
    run(scan(groups[-1]))
    for hh in range(hb):
        s_scr[hh] = states[hh]

    @pl.when(t == pl.num_programs(2) - 1)
    def _():
        for hh in range(hb):
            sout_ref[0, hh] = states[hh]


def _delta(dqkv3, bg3, s0, nw_row, c, tl, hb, gb):
    b, l, _ = dqkv3.shape
    n_hg = B_HEADS // hb
    kern = functools.partial(_delta_kernel, c=c, nblk=tl // c, hb=hb, gb=gb)
    lane_blk = lambda off: pl.BlockSpec((1, tl, hb * LANES), lambda bi, h, t: (bi, t, h + off))
    state = pl.BlockSpec((1, hb, B_DK, B_DV), lambda bi, h, t: (bi, h, 0, 0))
    return pl.pallas_call(
        kern,
        grid=(b, n_hg, l // tl),
        in_specs=[
            lane_blk(0), lane_blk(n_hg), lane_blk(2 * n_hg),
            pl.BlockSpec((1, tl, LANES), lambda bi, h, t: (bi, t, 0)),
            state,
            pl.BlockSpec((1, B_DV), lambda bi, h, t: (0, 0)),
        ],
        out_specs=[lane_blk(0), state],
        out_shape=[
            jax.ShapeDtypeStruct((b, l, B_V), BF16),
            jax.ShapeDtypeStruct((b, B_HEADS, B_DK, B_DV), F32),
        ],
        scratch_shapes=[pltpu.VMEM((hb, B_DK, B_DV), F32)],
        compiler_params=_cparams(3),
        name="delta_rule",
    )(dqkv3, dqkv3, dqkv3, bg3, s0, nw_row)


def _merge_kernel(x_ref, oa_ref, ob_ref, ln_ref, wdg, wga, wgb, wba, wbb, wout, y_ref):
    x = x_ref[...]
    h = _rms_rows(x, ln_ref[...]).astype(BF16)
    ob = ob_ref[...].astype(F32) * _silu(_dot(h, wdg[...]))
    branch_b = _dot(ob.astype(BF16), wbb[...])
    branch_a = _dot(oa_ref[...], wba[...])
    merged = (_sigmoid(_dot(h, wga[...])) * branch_a
              + _sigmoid(_dot(h, wgb[...])) * branch_b)
    y_ref[...] = x + _dot(merged.astype(BF16), wout[...])


def _merge(x2, oa2, ob2, ln_w, wdg, wga, wgb, wba, wbb, wout, tm):
    n = x2.shape[0]
    tile = pl.BlockSpec((tm, D_MODEL), lambda i: (i, 0))
    wspec = pl.BlockSpec((D_MODEL, D_MODEL), lambda i: (0, 0), pipeline_mode=pl.Buffered(1))
    return pl.pallas_call(
        _merge_kernel,
        grid=(n // tm,),
        in_specs=[tile, tile, tile, pl.BlockSpec((1, D_MODEL), lambda i: (0, 0))] + [wspec] * 6,
        out_specs=tile,
        out_shape=jax.ShapeDtypeStruct((n, D_MODEL), F32),
        compiler_params=_cparams(1),
        name="merge_outproj",
    )(x2, oa2, ob2, ln_w, wdg, wga, wgb, wba, wbb, wout)


def _ffn_kernel(x_ref, ln_ref, wgu_ref, wd_ref, y_ref, *, n_chunks):
    x = x_ref[...]
    h = _rms_rows(x, ln_ref[...]).astype(BF16)
    n_tiles = D_FF // MXU_DIM
    edges = [MXU_DIM * ((n_tiles * c + n_chunks - 1) // n_chunks) for c in range(n_chunks + 1)]
    acts = []
    for lo, hi in zip(edges[:-1], edges[1:]):
        gate = _dot(h, wgu_ref[:, lo:hi])
        up = _dot(h, wgu_ref[:, D_FF + lo:D_FF + hi])
        acts.append((_silu(gate) * up).astype(BF16))
    y = x
    for (lo, hi), act in zip(zip(edges[:-1], edges[1:]), acts):
        y = y + _dot(act, wd_ref[lo:hi, :])
    y_ref[...] = y


def _ffn(x2, ln_w, w_gate_up, w_down, tm, n_chunks):
    n = x2.shape[0]
    const = lambda i: (0, 0)
    return pl.pallas_call(
        functools.partial(_ffn_kernel, n_chunks=n_chunks),
        grid=(n // tm,),
        in_specs=[
            pl.BlockSpec((tm, D_MODEL), lambda i: (i, 0)),
            pl.BlockSpec((1, D_MODEL), const),
            pl.BlockSpec((D_MODEL, 2 * D_FF), const, pipeline_mode=pl.Buffered(1)),
            pl.BlockSpec((D_FF, D_MODEL), const, pipeline_mode=pl.Buffered(1)),
        ],
        out_specs=pl.BlockSpec((tm, D_MODEL), lambda i: (i, 0)),
        out_shape=jax.ShapeDtypeStruct((n, D_MODEL), F32),
        compiler_params=_cparams(1),
        name="swiglu_ffn",
    )(x2, ln_w, w_gate_up, w_down)


def _pick_tile(n, target):
    t = min(n, target)
    while n % t:
        t //= 2
    return t


def _prep_weights(ln1_w, w_in, q_norm_w, k_norm_w, lambda_q1, lambda_k1, lambda_q2, lambda_k2,
                  subln_w, w_branch_a, conv_w, a_log, dt_bias, delta_norm_w, w_branch_b,
                  w_out, ln2_w, w_gate_up, w_down):
    o_d = 2 * A_QK + A_V
    o_dg = o_d + CONV_CH
    o_bg = o_dg + B_V
    o_ga = o_bg + 2 * B_HEADS
    o_gb = o_ga + D_MODEL
    pad_lanes = lambda v: jnp.pad(v.astype(F32), (B_HEADS, LANES - 2 * B_HEADS))[None, :]
    return dict(
        ln1=ln1_w[None, :], ln2=ln2_w[None, :],
        w_qkv=w_in[:, :o_d].astype(BF16),
        w_d=w_in[:, o_d:o_dg].astype(BF16),
        w_dg=w_in[:, o_dg:o_bg].astype(BF16),
        w_bg=jnp.pad(w_in[:, o_bg:o_ga], ((0, 0), (0, LANES - 2 * B_HEADS))).astype(BF16),
        w_ga=w_in[:, o_ga:o_gb].astype(BF16),
        w_gb=w_in[:, o_gb:].astype(BF16),
        qn=jnp.tile(q_norm_w, A_QK // A_DIM)[None, :],
        kn=jnp.tile(k_norm_w, A_QK // A_DIM)[None, :],
        kn_col=jnp.tile(k_norm_w, A_QK // A_DIM)[:, None],
        w_kt=w_in[:, A_QK:2 * A_QK].T.astype(BF16),
        lams=tuple(v[None, :] for v in (lambda_q1, lambda_k1, lambda_q2, lambda_k2)),
        subln=subln_w[None, :], subln_col=subln_w[:, None],
        w_ba=w_branch_a.astype(BF16), w_bb=w_branch_b.astype(BF16),
        conv_w=conv_w, alog=pad_lanes(a_log), dtb=pad_lanes(dt_bias),
        dnorm=delta_norm_w[None, :],
        w_out=w_out.astype(BF16),
        w_gate_up=w_gate_up.astype(BF16), w_down=w_down.astype(BF16),
    )


def _layer(x, past_k, past_v, s0, conv_buf, lam_init, p):
    b, l, _ = x.shape
    n = b * l
    x2 = x.reshape(n, D_MODEL)

    prompt = past_k is None
    tm_a = _pick_tile(l, 1024) if prompt else _pick_tile(n, 1024)
    q2, k2, v2 = _inproj_attn(x2, p["ln1"], p["w_qkv"], p["w_kt"], p["qn"], p["kn"], p["kn_col"],
                              tm_a, l, prompt)

    tm_d = _pick_tile(l, 1024)
    cbuf8 = jnp.pad(conv_buf, ((0, 0), (SUBLANES - (CONV_W - 1), 0), (0, 0)))
    dqkv, bg, cst = _inproj_delta(x2, p["ln1"], p["w_d"], p["w_bg"], p["conv_w"], cbuf8,
                                  p["alog"], p["dtb"], tm_d, l)
    tiles_per_seq = l // tm_d
    new_conv = cst[tiles_per_seq - 1::tiles_per_seq, SUBLANES - (CONV_W - 1):, :]

    q3 = q2.reshape(b, l, A_QK)
    v3 = v2.reshape(b, l, A_V)
    if prompt:
        oa = _attn_prompt(q3, k2, v3, p["lams"], p["subln_col"], lam_init, _pick_tile(l, 256))
        k_out = jnp.transpose(k2.reshape(b, A_HEADS, 2, A_DIM, l), (0, 4, 1, 2, 3))
    else:
        past = past_k.shape[1]
        k3 = k2.reshape(b, l, A_QK)
        kct = jnp.transpose(past_k, (0, 2, 3, 4, 1)).reshape(b, A_HEADS, 2 * A_DIM, past)
        oa = _attn_sample(q3, kct, past_v.reshape(b, past * A_HEADS, A_VDIM), k3, v3, p["lams"],
                          p["subln"], lam_init)
        k_out = k3.reshape(b, l, A_HEADS, 2, A_DIM)

    c = DELTA_BLOCK if l % DELTA_BLOCK == 0 else l
    tl = c * max(1, min(8, l // c))
    ob, s_new = _delta(dqkv.reshape(b, l, CONV_CH), bg.reshape(b, l, LANES), s0, p["dnorm"], c, tl,
                       B_HEADS, 4)

    x1 = _merge(x2, oa.reshape(n, A_V), ob.reshape(n, B_V), p["ln1"], p["w_dg"], p["w_ga"],
                p["w_gb"], p["w_ba"], p["w_bb"], p["w_out"], _pick_tile(n, 512))
    y = _ffn(x1, p["ln2"], p["w_gate_up"], p["w_down"], _pick_tile(n, 512), 2)
    return (y.reshape(b, l, D_MODEL), k_out, v3.reshape(b, l, A_HEADS, A_VDIM), s_new, new_conv)


def kernel(x_prompt, x_sample, cache_k, cache_v, state_delta, state_conv, ln1_w, w_in, q_norm_w,
           k_norm_w, lambda_q1, lambda_k1, lambda_q2, lambda_k2, subln_w, w_branch_a, conv_w,
           a_log, dt_bias, delta_norm_w, w_branch_b, w_out, ln2_w, w_gate_up, w_down):
    depth = ln1_w.shape[0]
    xp, xs = x_prompt, x_sample
    outs_p, outs_s = [], []
    for layer in range(depth):
        lam_init = 0.8 - 0.6 * math.exp(-0.3 * layer)
        p = _prep_weights(*(w[layer] for w in (
            ln1_w, w_in, q_norm_w, k_norm_w, lambda_q1, lambda_k1, lambda_q2, lambda_k2, subln_w,
            w_branch_a, conv_w, a_log, dt_bias, delta_norm_w, w_branch_b, w_out, ln2_w,
            w_gate_up, w_down)))
        bp = xp.shape[0]
        zero_conv = jnp.zeros((bp, CONV_W - 1, CONV_CH), xp.dtype)
        zero_s = jnp.zeros((bp, B_HEADS, B_DK, B_DV), F32)
        xp, *rest_p = _layer(xp, None, None, zero_s, zero_conv, lam_init, p)
        xs, *rest_s = _layer(xs, cache_k[layer], cache_v[layer], state_delta[layer],
                             state_conv[layer], lam_init, p)
        outs_p.append(rest_p)
        outs_s.append(rest_s)
    stack = lambda outs, idx: jnp.stack([o[idx] for o in outs])
    return (xp, xs,
            stack(outs_p, 0), stack(outs_p, 1), stack(outs_p, 2), stack(outs_p, 3),
            stack(outs_s, 0), stack(outs_s, 1), stack(outs_s, 2), stack(outs_s, 3))
```

```python
import functools
import math

import jax
import jax.numpy as jnp
from jax import lax
from jax.experimental import pallas as pl
from jax.experimental.pallas import tpu as pltpu

F32 = jnp.float32
BF16 = jnp.bfloat16

D_MODEL = 1024
CHUNK = 64
A_HEADS = 8
A_DIM = 64
A_VDIM = 2 * A_DIM
B_HEADS = 8
B_DK = 128
B_DV = 128
CONV_W = 4
DELTA_BLOCK = 64
D_FF = -(-8 * D_MODEL // (3 * 256)) * 256
EPS = 1e-6

A_QK = A_HEADS * 2 * A_DIM
A_V = A_HEADS * A_VDIM
B_QK = B_HEADS * B_DK
B_V = B_HEADS * B_DV
CONV_CH = 2 * B_QK + B_V

LANES = 128
SUBLANES = 8
MXU_DIM = 256
NEG_BIG = -1e30
_DONE = object()
Q_SCALE = (A_DIM ** -0.5) * math.log2(math.e)
VMEM_LIMIT = 56 * 1024 * 1024


def _cparams(n_axes):
    return pltpu.CompilerParams(dimension_semantics=("arbitrary",) * n_axes,
                                vmem_limit_bytes=VMEM_LIMIT)


def _dot(a, b):
    return jnp.dot(a, b, preferred_element_type=F32)


def _dot_nt(a, b):
    return lax.dot_general(a, b, (((1,), (1,)), ((), ())), preferred_element_type=F32)


def _split(a):
    hi = a.astype(BF16)
    lo = (a - hi.astype(F32)).astype(BF16)
    return hi, lo


def _rms_rows(x, w):
    ms = jnp.mean(x * x, axis=-1, keepdims=True)
    return x * lax.rsqrt(ms + EPS) * w


def _group_sumsq(z, gshift):
    r = lax.broadcasted_iota(jnp.int32, (MXU_DIM, MXU_DIM), 0) >> gshift
    c = lax.broadcasted_iota(jnp.int32, (MXU_DIM, MXU_DIM), 1) >> gshift
    blk = (r == c).astype(BF16)
    outs = []
    for s in range(z.shape[1] // MXU_DIM):
        zs = z[:, s * MXU_DIM:(s + 1) * MXU_DIM]
        outs.append(_dot((zs * zs).astype(BF16), blk))
    return jnp.concatenate(outs, axis=1)


def _sigmoid(x):
    return 0.5 * jnp.tanh(0.5 * x) + 0.5


def _silu(x):
    half = 0.5 * x
    return half * jnp.tanh(half) + half


def _inproj_attn_kernel(x_ref, ln_ref, w_ref, wkt_ref, qn_ref, kn_ref, knc_ref, q_ref, k_ref, v_ref,
                        *, k_transposed):
    h = _rms_rows(x_ref[...], ln_ref[...]).astype(BF16)
    zq = _dot(h, w_ref[:, 0:A_QK])
    if k_transposed:
        zk = _dot_nt(wkt_ref[...], h)
    else:
        zk = _dot(h, w_ref[:, A_QK:2 * A_QK])
    v_ref[...] = _dot(h, w_ref[:, 2 * A_QK:])

    n = zq * lax.rsqrt(_group_sumsq(zq, 6) * (1.0 / A_DIM) + EPS) * qn_ref[...]
    q_ref[...] = (n * Q_SCALE).astype(BF16)
    if k_transposed:
        z3 = zk.reshape(A_QK // A_DIM, A_DIM, zk.shape[1])
        ms = jnp.mean(z3 * z3, axis=1, keepdims=True)
        kn3 = knc_ref[...].reshape(A_QK // A_DIM, A_DIM, 1)
        k_ref[0] = (z3 * lax.rsqrt(ms + EPS) * kn3).reshape(zk.shape)
    else:
        k_ref[...] = zk * lax.rsqrt(_group_sumsq(zk, 6) * (1.0 / A_DIM) + EPS) * kn_ref[...]


def _inproj_attn(x2, ln_w, w_qkv, w_kt, qn_row, kn_row, kn_col, tm, seq_len, k_transposed):
    n = x2.shape[0]
    const = lambda i: (0, 0)
    if k_transposed:
        tiles_per_seq = seq_len // tm
        k_spec = pl.BlockSpec((1, A_QK, tm), lambda i: (i // tiles_per_seq, 0, i % tiles_per_seq))
        k_shape = jax.ShapeDtypeStruct((n // seq_len, A_QK, seq_len), F32)
    else:
        k_spec = pl.BlockSpec((tm, A_QK), lambda i: (i, 0))
        k_shape = jax.ShapeDtypeStruct((n, A_QK), F32)
    return pl.pallas_call(
        functools.partial(_inproj_attn_kernel, k_transposed=k_transposed),
        grid=(n // tm,),
        in_specs=[
            pl.BlockSpec((tm, D_MODEL), lambda i: (i, 0)),
            pl.BlockSpec((1, D_MODEL), const),
            pl.BlockSpec((D_MODEL, 2 * A_QK + A_V), const, pipeline_mode=pl.Buffered(1)),
            pl.BlockSpec((A_QK, D_MODEL), const, pipeline_mode=pl.Buffered(1)),
            pl.BlockSpec((1, A_QK), const),
            pl.BlockSpec((1, A_QK), const),
            pl.BlockSpec((A_QK, 1), const),
        ],
        out_specs=[
            pl.BlockSpec((tm, A_QK), lambda i: (i, 0)),
            k_spec,
            pl.BlockSpec((tm, A_V), lambda i: (i, 0)),
        ],
        out_shape=[
            jax.ShapeDtypeStruct((n, A_QK), BF16),
            k_shape,
            jax.ShapeDtypeStruct((n, A_V), F32),
        ],
        compiler_params=_cparams(1),
        name="inproj_attn",
    )(x2, ln_w, w_qkv, w_kt, qn_row, kn_row, kn_col)


def _inproj_delta_kernel(x_ref, ln_ref, w_ref, wbg_ref, cw_ref, cbuf_ref, alog_ref, dtb_ref,
                         dqkv_ref, bg_ref, cst_ref, carry, *, tiles_per_seq, seqs_per_tile):
    i = pl.program_id(0)
    tm = x_ref.shape[0]
    h = _rms_rows(x_ref[...], ln_ref[...]).astype(BF16)
    zs = [_dot(h, w_ref[:, j * B_QK:(j + 1) * B_QK]) for j in range(3)]
    zb = _dot(h, wbg_ref[...])

    lane = lax.broadcasted_iota(jnp.int32, zb.shape, 1)
    a = zb + dtb_ref[...]
    softplus = jnp.maximum(a, 0.0) + jnp.log1p(jnp.exp(-jnp.abs(a)))
    g = -jnp.exp(alog_ref[...]) * softplus
    bg_ref[...] = jnp.where(lane < B_HEADS, _sigmoid(zb), jnp.where(lane < 2 * B_HEADS, g, 0.0))

    first = (i % tiles_per_seq) == 0
    seg = tm // seqs_per_tile
    for j, zfull_tile in enumerate(zs):
        cols = slice(j * B_QK, (j + 1) * B_QK)
        cw = cw_ref[:, cols]
        for sq in range(seqs_per_tile):
            rows = slice(sq * seg, (sq + 1) * seg)
            z = zfull_tile[rows, :]
            prev = jnp.where(first, cbuf_ref[sq, :, cols], carry[:, cols])
            tail = z[seg - SUBLANES:, :]
            if sq == seqs_per_tile - 1:
                carry[:, cols] = tail
            cst_ref[sq, :, cols] = tail
            zfull = jnp.concatenate([prev, z], axis=0)
            y = z * cw[CONV_W - 1:CONV_W, :]
            for s in range(1, CONV_W):
                shifted = pltpu.roll(zfull, s, axis=0)[SUBLANES:, :]
                y = y + shifted * cw[CONV_W - 1 - s:CONV_W - s, :]
            y = _silu(y)
            if j < 2:
                y = y * lax.rsqrt(_group_sumsq(y, 7) + EPS)
            if j == 0:
                y = y * (B_DK ** -0.5)
            dqkv_ref[rows, cols] = y


def _inproj_delta(x2, ln_w, w_d, w_bg, conv_w, cbuf8, alog_row, dtb_row, tm, seq_len):
    n = x2.shape[0]
    tiles_per_seq = max(1, seq_len // tm)
    spt = max(1, tm // seq_len)
    n_tiles = n // tm
    const = lambda i: (0, 0)
    kern = functools.partial(_inproj_delta_kernel, tiles_per_seq=tiles_per_seq, seqs_per_tile=spt)
    return pl.pallas_call(
        kern,
        grid=(n_tiles,),
        in_specs=[
            pl.BlockSpec((tm, D_MODEL), lambda i: (i, 0)),
            pl.BlockSpec((1, D_MODEL), const),
            pl.BlockSpec((D_MODEL, CONV_CH), const, pipeline_mode=pl.Buffered(1)),
            pl.BlockSpec((D_MODEL, LANES), const, pipeline_mode=pl.Buffered(1)),
            pl.BlockSpec((CONV_W, CONV_CH), const),
            pl.BlockSpec((spt, SUBLANES, CONV_CH), lambda i: (i // tiles_per_seq, 0, 0)),
            pl.BlockSpec((1, LANES), const),
            pl.BlockSpec((1, LANES), const),
        ],
        out_specs=[
            pl.BlockSpec((tm, CONV_CH), lambda i: (i, 0)),
            pl.BlockSpec((tm, LANES), lambda i: (i, 0)),
            pl.BlockSpec((spt, SUBLANES, CONV_CH), lambda i: (i, 0, 0)),
        ],
        out_shape=[
            jax.ShapeDtypeStruct((n, CONV_CH), F32),
            jax.ShapeDtypeStruct((n, LANES), F32),
            jax.ShapeDtypeStruct((n_tiles * spt, SUBLANES, CONV_CH), F32),
        ],
        scratch_shapes=[pltpu.VMEM((SUBLANES, CONV_CH), F32)],
        compiler_params=_cparams(1),
        name="inproj_delta",
    )(x2, ln_w, w_d, w_bg, conv_w, cbuf8, alog_row, dtb_row)


def _lambda_value(lq1, lk1, lq2, lk2, lam_init):
    d1 = jnp.sum(lq1[...] * lk1[...], axis=-1, keepdims=True)
    d2 = jnp.sum(lq2[...] * lk2[...], axis=-1, keepdims=True)
    return jnp.exp(d1) - jnp.exp(d2) + lam_init


def _split_maps(q):
    lane = lax.broadcasted_iota(jnp.int32, q.shape, 1)
    zero = jnp.zeros_like(q)
    return jnp.where(lane < A_DIM, q, zero), jnp.where(lane >= A_DIM, q, zero)


def _attn_finish(o, sw, lam_init):
    ms = jnp.mean(o * o, axis=-1, keepdims=True)
    return (o * lax.rsqrt(ms + EPS) * sw * (1.0 - lam_init)).astype(BF16)


def _rows_to_sublanes(x, op):
    return functools.reduce(op, [x[i * SUBLANES:(i + 1) * SUBLANES] for i in range(x.shape[0] // SUBLANES)])


def _attn_prompt_kernel(q_ref, k_ref, v_ref, lq1, lk1, lq2, lk2, swc_ref, o_ref,
                        kb_scr, vt_scr, s_scr, p_scr, *, tq, lam_init):
    seq = q_ref.shape[1]
    lam = _lambda_value(lq1, lk1, lq2, lk2, lam_init)
    kb_scr[...] = k_ref[0].T.astype(BF16)
    vt_scr[0:A_VDIM, :] = v_ref[0].T.astype(BF16)
    vt_scr[A_VDIM:, :] = jnp.ones((vt_scr.shape[0] - A_VDIM, seq), BF16)
    ri = lax.broadcasted_iota(jnp.int32, (tq, 2 * tq), 0)
    ci = lax.broadcasted_iota(jnp.int32, (tq, 2 * tq), 1)
    diag_mask = (ri // CHUNK) <= ((ci & (tq - 1)) // CHUNK)
    n_q = seq // tq
    n_slots = s_scr.shape[0]

    def score_phase(qi):
        q2 = jnp.concatenate(_split_maps(q_ref[0, qi * tq:(qi + 1) * tq, :]), axis=0)
        q2t = q2.astype(F32).T.astype(BF16)
        m8 = None
        for j in range(qi + 1):
            st = _dot(kb_scr[j * tq:(j + 1) * tq, :], q2t)
            if j == qi:
                st = jnp.where(diag_mask, st, NEG_BIG)
            s_scr[qi % n_slots, j * tq:(j + 1) * tq, :] = st
            part = _rows_to_sublanes(st, jnp.maximum)
            m8 = part if m8 is None else jnp.maximum(m8, part)
        return jnp.max(m8, axis=0, keepdims=True)

    def exp_phase(qi, m_row):
        for j in range(qi + 1):
            p = jnp.exp2(s_scr[qi % n_slots, j * tq:(j + 1) * tq, :] - m_row)
            p_scr[qi % n_slots, j * tq:(j + 1) * tq, :] = p.astype(BF16)

    def value_phase(qi):
        n_keys = (qi + 1) * tq
        acc_t = _dot(vt_scr[:, :n_keys], p_scr[qi % n_slots, :n_keys, :])
        o2t = acc_t[:A_VDIM] / acc_t[A_VDIM:A_VDIM + 1]
        ot = o2t[:, :tq] - lam * o2t[:, tq:]
        ms = jnp.mean(ot * ot, axis=0, keepdims=True)
        ot = ot * lax.rsqrt(ms + EPS) * swc_ref[...] * (1.0 - lam_init)
        o_ref[0, qi * tq:(qi + 1) * tq, :] = ot.T.astype(BF16)

    m_row = score_phase(0)
    for qi in range(n_q):
        exp_phase(qi, m_row)
        if qi + 1 < n_q:
            m_row = score_phase(qi + 1)
        value_phase(qi)


def _attn_prompt(q3, kt3, v3, lams, sw_col, lam_init, tq):
    b, l, _ = q3.shape
    vec = lambda bi, h: (0, 0)
    head = pl.BlockSpec((1, l, LANES), lambda bi, h: (bi, 0, h))
    kern = functools.partial(_attn_prompt_kernel, tq=tq, lam_init=lam_init)
    return pl.pallas_call(
        kern,
        grid=(b, A_HEADS),
        in_specs=[
            head, pl.BlockSpec((1, 2 * A_DIM, l), lambda bi, h: (bi, h, 0)), head,
            pl.BlockSpec((1, A_DIM), vec), pl.BlockSpec((1, A_DIM), vec),
            pl.BlockSpec((1, A_DIM), vec), pl.BlockSpec((1, A_DIM), vec),
            pl.BlockSpec((A_VDIM, 1), vec),
        ],
        out_specs=head,
        out_shape=jax.ShapeDtypeStruct((b, l, A_V), BF16),
        scratch_shapes=[
            pltpu.VMEM((l, 2 * A_DIM), BF16), pltpu.VMEM((A_VDIM + 2 * SUBLANES, l), BF16),
            pltpu.VMEM((2, l, 2 * tq), F32), pltpu.VMEM((2, l, 2 * tq), BF16),
        ],
        compiler_params=_cparams(2),
        name="attn_prompt",
    )(q3, kt3, v3, *lams, sw_col)


def _attn_sample_kernel(q_ref, kc_ref, vc_ref, kn_ref, vn_ref, lq1, lk1, lq2, lk2, sw_ref, o_ref,
                        *, lam_init):
    h = pl.program_id(1)
    l = q_ref.shape[1]
    past = kc_ref.shape[3]
    lam = _lambda_value(lq1, lk1, lq2, lk2, lam_init)
    q2 = jnp.concatenate(_split_maps(q_ref[0]), axis=0)
    vc = vc_ref[0, pl.ds(h, past, stride=A_HEADS), :].astype(BF16)
    sc = _dot(q2, kc_ref[0, 0].astype(BF16))
    sn = _dot_nt(q2, kn_ref[0].astype(BF16))
    mx = jnp.maximum(jnp.max(sc, axis=-1, keepdims=True), jnp.max(sn, axis=-1, keepdims=True))
    pc = jnp.exp2(sc - mx)
    pn = jnp.exp2(sn - mx)
    den = jnp.sum(pc, axis=-1, keepdims=True) + jnp.sum(pn, axis=-1, keepdims=True)
    o2 = (_dot(pc.astype(BF16), vc) + _dot(pn.astype(BF16), vn_ref[0].astype(BF16))) / den
    o_ref[0] = _attn_finish(o2[:l] - lam * o2[l:], sw_ref[...], lam_init)


def _attn_sample(q3, kct, vc2, kn3, vn3, lams, sw_row, lam_init):
    b, l, _ = q3.shape
    past = kct.shape[3]
    vec = lambda bi, h: (0, 0)
    blk = lambda rows: pl.BlockSpec((1, rows, LANES), lambda bi, h: (bi, 0, h))
    kern = functools.partial(_attn_sample_kernel, lam_init=lam_init)
    return pl.pallas_call(
        kern,
        grid=(b, A_HEADS),
        in_specs=[
            blk(l),
            pl.BlockSpec((1, 1, 2 * A_DIM, past), lambda bi, h: (bi, h, 0, 0)),
            pl.BlockSpec((1, past * A_HEADS, A_VDIM), lambda bi, h: (bi, 0, 0)),
            blk(l), blk(l),
            pl.BlockSpec((1, A_DIM), vec), pl.BlockSpec((1, A_DIM), vec),
            pl.BlockSpec((1, A_DIM), vec), pl.BlockSpec((1, A_DIM), vec),
            pl.BlockSpec((1, A_VDIM), vec),
        ],
        out_specs=blk(l),
        out_shape=jax.ShapeDtypeStruct((b, l, A_V), BF16),
        compiler_params=_cparams(2),
        name="attn_sample",
    )(q3, kct, vc2, kn3, vn3, *lams, sw_row)


def _delta_kernel(q_ref, k_ref, v_ref, bg_ref, s0_ref, nw_ref, o_ref, sout_ref, s_scr,
                  *, c, nblk, hb, gb):
    t = pl.program_id(2)

    @pl.when(t == 0)
    def _():
        s_scr[...] = s0_ref[0]

    bg = bg_ref[0]
    lane = lax.broadcasted_iota(jnp.int32, bg.shape, 1)
    bg_t = bg.T
    sub = lax.broadcasted_iota(jnp.int32, bg_t.shape, 0)
    ri = lax.broadcasted_iota(jnp.int32, (c, c), 0)
    ci = lax.broadcasted_iota(jnp.int32, (c, c), 1)
    causal = ri >= ci
    strict = ri > ci
    ltri = causal.astype(BF16)
    utri = (ri <= ci).astype(BF16)
    eye = jnp.where(ri == ci, 1.0, 0.0)

    gates = []
    for hh in range(hb):
        h = pl.program_id(1) * hb + hh
        gates.append((
            jnp.sum(jnp.where(lane == h, bg, 0.0), axis=-1, keepdims=True),
            jnp.sum(jnp.where(lane == h + B_HEADS, bg, 0.0), axis=-1, keepdims=True),
            jnp.sum(jnp.where(sub == h + B_HEADS, bg_t, 0.0), axis=0, keepdims=True)))
    pre = {}
    states = [s_scr[hh] for hh in range(hb)]

    def prepare(blocks):
        items = [(hh, blk) for hh in range(hb) for blk in blocks]
        for hh, blk in items:
            rows = slice(blk * c, (blk + 1) * c)
            cols = slice(hh * LANES, (hh + 1) * LANES)
            bc = gates[hh][0][rows, :]
            q = q_ref[0, rows, cols]
            k = k_ref[0, rows, cols]
            v = v_ref[0, rows, cols]
            ghi, glo = _split(jnp.broadcast_to(gates[hh][1][rows, :], (c, LANES)))
            g_cum = _dot(ltri, ghi) + _dot(ltri, glo)
            rhi, rlo = _split(jnp.broadcast_to(gates[hh][2][:, rows], (c, c)))
            g_cum_row = _dot(rhi, utri) + _dot(rlo, utri)
            diff = jnp.where(causal, g_cum[:, :c] - g_cum_row, 0.0)
            decay = jnp.where(causal, jnp.exp(diff), 0.0)
            kb = k.astype(BF16)
            m = jnp.where(strict, bc * _dot_nt(kb, kb) * decay, 0.0)
            pre[hh, blk] = dict(q=q, k=k, v=v, bc=bc, g_cum=g_cum, decay=decay, kb=kb)
            pre[hh, blk]["pq"] = (eye - m, m)
        yield
        for _ in range(int(math.log2(c)) - 1):
            for it in items:
                pm, qm = pre[it]["pq"]
                qb = qm.astype(BF16)
                pre[it]["pq"] = (pm, _dot(qb, qb))
            yield
            for it in items:
                pm, qm = pre[it]["pq"]
                pre[it]["pq"] = (pm + _dot(pm.astype(BF16), qm.astype(BF16)), qm)
            yield
        for it in items:
            d = pre[it]
            e_g = jnp.exp(d["g_cum"])
            rhs = jnp.concatenate([d["v"] * d["bc"], d["k"] * (d["bc"] * e_g)], axis=1)
            sol = _dot(d.pop("pq")[0].astype(BF16), rhs.astype(BF16))
            g_last = d["g_cum"][c - 1:c, :]
            d.update(
                u=sol[:, :B_DV], wb=sol[:, B_DV:].astype(BF16),
                a_loc=jnp.where(causal, _dot_nt(d["q"].astype(BF16), d["kb"]) * d["decay"],
                                0.0).astype(BF16),
                q_dec=(d["q"] * e_g).astype(BF16),
                k_tail_t=(d["k"] * jnp.exp(g_last - d["g_cum"])).T.astype(BF16),
                g_tail=jnp.exp(g_last))
        yield

    def scan(blocks):
        for blk in blocks:
            vbs, o_state = [], []
            for hh in range(hb):
                d = pre[hh, blk]
                sb = states[hh].astype(BF16)
                vbs.append((d["u"] - _dot(d["wb"], sb)).astype(BF16))
                o_state.append(_dot(d["q_dec"], sb))
            yield
            for hh in range(hb):
                d = pre.pop((hh, blk))
                o = o_state[hh] + _dot(d["a_loc"], vbs[hh])
                states[hh] = states[hh] * d["g_tail"] + _dot(d["k_tail_t"], vbs[hh])
                o_ref[0, blk * c:(blk + 1) * c, hh * LANES:(hh + 1) * LANES] = (
                    _rms_rows(o, nw_ref[...]).astype(BF16))
            yield

    def run(*stage_iters):
        live = list(stage_iters)
        while live:
            live = [it for it in live if next(it, _DONE) is not _DONE]

    groups = [list(range(g, min(g + gb, nblk))) for g in range(0, nblk, gb)]
    run(prepare(groups[0]))
    for g in range(1, len(groups)):
        run(prepare(groups[g]), scan(groups[g - 1]))
    run(scan(groups[-1]))
    for hh in range(hb):
        s_scr[hh] = states[hh]

    @pl.when(t == pl.num_programs(2) - 1)
    def _():
        for hh in range(hb):
            sout_ref[0, hh] = states[hh]


def _delta(dqkv3, bg3, s0, nw_row, c, tl, hb, gb):
    b, l, _ = dqkv3.shape
    n_hg = B_HEADS // hb
    kern = functools.partial(_delta_kernel, c=c, nblk=tl // c, hb=hb, gb=gb)
    lane_blk = lambda off: pl.BlockSpec((1, tl, hb * LANES), lambda bi, h, t: (bi, t, h + off))
    state = pl.BlockSpec((1, hb, B_DK, B_DV), lambda bi, h, t: (bi, h, 0, 0))
    return pl.pallas_call(
        kern,
        grid=(b, n_hg, l // tl),
        in_specs=[
            lane_blk(0), lane_blk(n_hg), lane_blk(2 * n_hg),
            pl.BlockSpec((1, tl, LANES), lambda bi, h, t: (bi, t, 0)),
            state,
            pl.BlockSpec((1, B_DV), lambda bi, h, t: (0, 0)),
        ],
        out_specs=[lane_blk(0), state],
        out_shape=[
            jax.ShapeDtypeStruct((b, l, B_V), BF16),
            jax.ShapeDtypeStruct((b, B_HEADS, B_DK, B_DV), F32),
        ],
        scratch_shapes=[pltpu.VMEM((hb, B_DK, B_DV), F32)],
        compiler_params=_cparams(3),
        name="delta_rule",
    )(dqkv3, dqkv3, dqkv3, bg3, s0, nw_row)


def _merge_kernel(x_ref, oa_ref, ob_ref, ln_ref, wdg, wga, wgb, wba, wbb, wout, y_ref):
    x = x_ref[...]
    h = _rms_rows(x, ln_ref[...]).astype(BF16)
    ob = ob_ref[...].astype(F32) * _silu(_dot(h, wdg[...]))
    branch_b = _dot(ob.astype(BF16), wbb[...])
    branch_a = _dot(oa_ref[...], wba[...])
    merged = (_sigmoid(_dot(h, wga[...])) * branch_a
              + _sigmoid(_dot(h, wgb[...])) * branch_b)
    y_ref[...] = x + _dot(merged.astype(BF16), wout[...])


def _merge(x2, oa2, ob2, ln_w, wdg, wga, wgb, wba, wbb, wout, tm):
    n = x2.shape[0]
    tile = pl.BlockSpec((tm, D_MODEL), lambda i: (i, 0))
    wspec = pl.BlockSpec((D_MODEL, D_MODEL), lambda i: (0, 0), pipeline_mode=pl.Buffered(1))
    return pl.pallas_call(
        _merge_kernel,
        grid=(n // tm,),
        in_specs=[tile, tile, tile, pl.BlockSpec((1, D_MODEL), lambda i: (0, 0))] + [wspec] * 6,
        out_specs=tile,
        out_shape=jax.ShapeDtypeStruct((n, D_MODEL), F32),
        compiler_params=_cparams(1),
        name="merge_outproj",
    )(x2, oa2, ob2, ln_w, wdg, wga, wgb, wba, wbb, wout)


def _ffn_kernel(x_ref, ln_ref, wgu_ref, wd_ref, y_ref, *, n_chunks):
    x = x_ref[...]
    h = _rms_rows(x, ln_ref[...]).astype(BF16)
    n_tiles = D_FF // MXU_DIM
    edges = [MXU_DIM * ((n_tiles * c + n_chunks - 1) // n_chunks) for c in range(n_chunks + 1)]
    acts = []
    for lo, hi in zip(edges[:-1], edges[1:]):
        gate = _dot(h, wgu_ref[:, lo:hi])
        up = _dot(h, wgu_ref[:, D_FF + lo:D_FF + hi])
        acts.append((_silu(gate) * up).astype(BF16))
    y = x
    for (lo, hi), act in zip(zip(edges[:-1], edges[1:]), acts):
        y = y + _dot(act, wd_ref[lo:hi, :])
    y_ref[...] = y


def _ffn(x2, ln_w, w_gate_up, w_down, tm, n_chunks):
    n = x2.shape[0]
    const = lambda i: (0, 0)
    return pl.pallas_call(
        functools.partial(_ffn_kernel, n_chunks=n_chunks),
        grid=(n // tm,),
        in_specs=[
            pl.BlockSpec((tm, D_MODEL), lambda i: (i, 0)),
            pl.BlockSpec((1, D_MODEL), const),
            pl.BlockSpec((D_MODEL, 2 * D_FF), const, pipeline_mode=pl.Buffered(1)),
            pl.BlockSpec((D_FF, D_MODEL), const, pipeline_mode=pl.Buffered(1)),
        ],
        out_specs=pl.BlockSpec((tm, D_MODEL), lambda i: (i, 0)),
        out_shape=jax.ShapeDtypeStruct((n, D_MODEL), F32),
        compiler_params=_cparams(1),
        name="swiglu_ffn",
    )(x2, ln_w, w_gate_up, w_down)


def _pick_tile(n, target):
    t = min(n, target)
    while n % t:
        t //= 2
    return t


def _prep_weights(ln1_w, w_in, q_norm_w, k_norm_w, lambda_q1, lambda_k1, lambda_q2, lambda_k2,
                  subln_w, w_branch_a, conv_w, a_log, dt_bias, delta_norm_w, w_branch_b,
                  w_out, ln2_w, w_gate_up, w_down):
    o_d = 2 * A_QK + A_V
    o_dg = o_d + CONV_CH
    o_bg = o_dg + B_V
    o_ga = o_bg + 2 * B_HEADS
    o_gb = o_ga + D_MODEL
    pad_lanes = lambda v: jnp.pad(v.astype(F32), (B_HEADS, LANES - 2 * B_HEADS))[None, :]
    return dict(
        ln1=ln1_w[None, :], ln2=ln2_w[None, :],
        w_qkv=w_in[:, :o_d].astype(BF16),
        w_d=w_in[:, o_d:o_dg].astype(BF16),
        w_dg=w_in[:, o_dg:o_bg].astype(BF16),
        w_bg=jnp.pad(w_in[:, o_bg:o_ga], ((0, 0), (0, LANES - 2 * B_HEADS))).astype(BF16),
        w_ga=w_in[:, o_ga:o_gb].astype(BF16),
        w_gb=w_in[:, o_gb:].astype(BF16),
        qn=jnp.tile(q_norm_w, A_QK // A_DIM)[None, :],
        kn=jnp.tile(k_norm_w, A_QK // A_DIM)[None, :],
        kn_col=jnp.tile(k_norm_w, A_QK // A_DIM)[:, None],
        w_kt=w_in[:, A_QK:2 * A_QK].T.astype(BF16),
        lams=tuple(v[None, :] for v in (lambda_q1, lambda_k1, lambda_q2, lambda_k2)),
        subln=subln_w[None, :], subln_col=subln_w[:, None],
        w_ba=w_branch_a.astype(BF16), w_bb=w_branch_b.astype(BF16),
        conv_w=conv_w, alog=pad_lanes(a_log), dtb=pad_lanes(dt_bias),
        dnorm=delta_norm_w[None, :],
        w_out=w_out.astype(BF16),
        w_gate_up=w_gate_up.astype(BF16), w_down=w_down.astype(BF16),
    )


def _layer(x, past_k, past_v, s0, conv_buf, lam_init, p):
    b, l, _ = x.shape
    n = b * l
    x2 = x.reshape(n, D_MODEL)

    prompt = past_k is None
    tm_a = _pick_tile(l, 1024) if prompt else _pick_tile(n, 1024)
    q2, k2, v2 = _inproj_attn(x2, p["ln1"], p["w_qkv"], p["w_kt"], p["qn"], p["kn"], p["kn_col"],
                              tm_a, l, prompt)

    tm_d = _pick_tile(l, 1024) if l >= 512 else l * _pick_tile(b, max(1, 512 // l))
    cbuf8 = jnp.pad(conv_buf, ((0, 0), (SUBLANES - (CONV_W - 1), 0), (0, 0)))
    dqkv, bg, cst = _inproj_delta(x2, p["ln1"], p["w_d"], p["w_bg"], p["conv_w"], cbuf8,
                                  p["alog"], p["dtb"], tm_d, l)
    tiles_per_seq = max(1, l // tm_d)
    new_conv = cst[tiles_per_seq - 1::tiles_per_seq, SUBLANES - (CONV_W - 1):, :]

    q3 = q2.reshape(b, l, A_QK)
    v3 = v2.reshape(b, l, A_V)
    if prompt:
        oa = _attn_prompt(q3, k2, v3, p["lams"], p["subln_col"], lam_init, _pick_tile(l, 256))
        k_out = jnp.transpose(k2.reshape(b, A_HEADS, 2, A_DIM, l), (0, 4, 1, 2, 3))
    else:
        past = past_k.shape[1]
        k3 = k2.reshape(b, l, A_QK)
        kct = jnp.transpose(past_k, (0, 2, 3, 4, 1)).reshape(b, A_HEADS, 2 * A_DIM, past)
        oa = _attn_sample(q3, kct, past_v.reshape(b, past * A_HEADS, A_VDIM), k3, v3, p["lams"],
                          p["subln"], lam_init)
        k_out = k3.reshape(b, l, A_HEADS, 2, A_DIM)

    c = DELTA_BLOCK if l % DELTA_BLOCK == 0 else l
    tl = c * max(1, min(8, l // c))
    ob, s_new = _delta(dqkv.reshape(b, l, CONV_CH), bg.reshape(b, l, LANES), s0, p["dnorm"], c, tl,
                       B_HEADS, 4)

    x1 = _merge(x2, oa.reshape(n, A_V), ob.reshape(n, B_V), p["ln1"], p["w_dg"], p["w_ga"],
                p["w_gb"], p["w_ba"], p["w_bb"], p["w_out"], _pick_tile(n, 512))
    y = _ffn(x1, p["ln2"], p["w_gate_up"], p["w_down"], _pick_tile(n, 512), 2)
    return (y.reshape(b, l, D_MODEL), k_out, v3.reshape(b, l, A_HEADS, A_VDIM), s_new, new_conv)


def kernel(x_prompt, x_sample, cache_k, cache_v, state_delta, state_conv, ln1_w, w_in, q_norm_w,
           k_norm_w, lambda_q1, lambda_k1, lambda_q2, lambda_k2, subln_w, w_branch_a, conv_w,
           a_log, dt_bias, delta_norm_w, w_branch_b, w_out, ln2_w, w_gate_up, w_down):
    depth = ln1_w.shape[0]
    xp, xs = x_prompt, x_sample
    outs_p, outs_s = [], []
    for layer in range(depth):
        lam_init = 0.8 - 0.6 * math.exp(-0.3 * layer)
        p = _prep_weights(*(w[layer] for w in (
            ln1_w, w_in, q_norm_w, k_norm_w, lambda_q1, lambda_k1, lambda_q2, lambda_k2, subln_w,
            w_branch_a, conv_w, a_log, dt_bias, delta_norm_w, w_branch_b, w_out, ln2_w,
            w_gate_up, w_down)))
        bp = xp.shape[0]
        zero_conv = jnp.zeros((bp, CONV_W - 1, CONV_CH), xp.dtype)
        zero_s = jnp.zeros((bp, B_HEADS, B_DK, B_DV), F32)
        xp, *rest_p = _layer(xp, None, None, zero_s, zero_conv, lam_init, p)
        xs, *rest_s = _layer(xs, cache_k[layer], cache_v[layer], state_delta[layer],
                             state_conv[layer], lam_init, p)
        outs_p.append(rest_p)
        outs_s.append(rest_s)
    stack = lambda outs, idx: jnp.stack([o[idx] for o in outs])
    return (xp, xs,
            stack(outs_p, 0), stack(outs_p, 1), stack(outs_p, 2), stack(outs_p, 3),
            stack(outs_s, 0), stack(outs_s, 1), stack(outs_s, 2), stack(outs_s, 3))
```

```python
import functools
import math

import jax
import jax.numpy as jnp
from jax import lax
from jax.experimental import pallas as pl
from jax.experimental.pallas import tpu as pltpu

F32 = jnp.float32
BF16 = jnp.bfloat16

D_MODEL = 1024
CHUNK = 64
A_HEADS = 8
A_DIM = 64
A_VDIM = 2 * A_DIM
B_HEADS = 8
B_DK = 128
B_DV = 128
CONV_W = 4
DELTA_BLOCK = 64
D_FF = -(-8 * D_MODEL // (3 * 256)) * 256
EPS = 1e-6

A_QK = A_HEADS * 2 * A_DIM
A_V = A_HEADS * A_VDIM
B_QK = B_HEADS * B_DK
B_V = B_HEADS * B_DV
CONV_CH = 2 * B_QK + B_V

LANES = 128
SUBLANES = 8
MXU_DIM = 256
NEG_BIG = -1e30
_DONE = object()
BOUND_SLACK = 1.0 + 2.0 ** -6
DEN_RANGE = (2.0 ** -64, 2.0 ** 64)
Q_SCALE = (A_DIM ** -0.5) * math.log2(math.e)
VMEM_LIMIT = 56 * 1024 * 1024


def _cparams(n_axes):
    return pltpu.CompilerParams(dimension_semantics=("arbitrary",) * n_axes,
                                vmem_limit_bytes=VMEM_LIMIT)


def _dot(a, b):
    return jnp.dot(a, b, preferred_element_type=F32)


def _dot_nt(a, b):
    return lax.dot_general(a, b, (((1,), (1,)), ((), ())), preferred_element_type=F32)


def _split(a):
    hi = a.astype(BF16)
    lo = (a - hi.astype(F32)).astype(BF16)
    return hi, lo


def _rms_rows(x, w):
    ms = jnp.mean(x * x, axis=-1, keepdims=True)
    return x * lax.rsqrt(ms + EPS) * w


def _group_sumsq(z, gshift):
    r = lax.broadcasted_iota(jnp.int32, (MXU_DIM, MXU_DIM), 0) >> gshift
    c = lax.broadcasted_iota(jnp.int32, (MXU_DIM, MXU_DIM), 1) >> gshift
    blk = (r == c).astype(BF16)
    outs = []
    for s in range(z.shape[1] // MXU_DIM):
        zs = z[:, s * MXU_DIM:(s + 1) * MXU_DIM]
        outs.append(_dot((zs * zs).astype(BF16), blk))
    return jnp.concatenate(outs, axis=1)


def _sigmoid(x):
    return 0.5 * jnp.tanh(0.5 * x) + 0.5


def _silu(x):
    half = 0.5 * x
    return half * jnp.tanh(half) + half


def _inproj_attn_kernel(x_ref, ln_ref, w_ref, wkt_ref, qn_ref, kn_ref, knc_ref, q_ref, k_ref, v_ref,
                        *, k_transposed):
    h = _rms_rows(x_ref[...], ln_ref[...]).astype(BF16)
    zq = _dot(h, w_ref[:, 0:A_QK])
    if k_transposed:
        zk = _dot_nt(wkt_ref[...], h)
    else:
        zk = _dot(h, w_ref[:, A_QK:2 * A_QK])
    v_ref[...] = _dot(h, w_ref[:, 2 * A_QK:])

    n = zq * lax.rsqrt(_group_sumsq(zq, 6) * (1.0 / A_DIM) + EPS) * qn_ref[...]
    q_ref[...] = (n * Q_SCALE).astype(BF16)
    if k_transposed:
        z3 = zk.reshape(A_QK // A_DIM, A_DIM, zk.shape[1])
        ms = jnp.mean(z3 * z3, axis=1, keepdims=True)
        kn3 = knc_ref[...].reshape(A_QK // A_DIM, A_DIM, 1)
        k_ref[0] = (z3 * lax.rsqrt(ms + EPS) * kn3).reshape(zk.shape)
    else:
        k_ref[...] = zk * lax.rsqrt(_group_sumsq(zk, 6) * (1.0 / A_DIM) + EPS) * kn_ref[...]


def _inproj_attn(x2, ln_w, w_qkv, w_kt, qn_row, kn_row, kn_col, tm, seq_len, k_transposed):
    n = x2.shape[0]
    const = lambda i: (0, 0)
    if k_transposed:
        tiles_per_seq = seq_len // tm
        k_spec = pl.BlockSpec((1, A_QK, tm), lambda i: (i // tiles_per_seq, 0, i % tiles_per_seq))
        k_shape = jax.ShapeDtypeStruct((n // seq_len, A_QK, seq_len), F32)
    else:
        k_spec = pl.BlockSpec((tm, A_QK), lambda i: (i, 0))
        k_shape = jax.ShapeDtypeStruct((n, A_QK), F32)
    return pl.pallas_call(
        functools.partial(_inproj_attn_kernel, k_transposed=k_transposed),
        grid=(n // tm,),
        in_specs=[
            pl.BlockSpec((tm, D_MODEL), lambda i: (i, 0)),
            pl.BlockSpec((1, D_MODEL), const),
            pl.BlockSpec((D_MODEL, 2 * A_QK + A_V), const, pipeline_mode=pl.Buffered(1)),
            pl.BlockSpec((A_QK, D_MODEL), const, pipeline_mode=pl.Buffered(1)),
            pl.BlockSpec((1, A_QK), const),
            pl.BlockSpec((1, A_QK), const),
            pl.BlockSpec((A_QK, 1), const),
        ],
        out_specs=[
            pl.BlockSpec((tm, A_QK), lambda i: (i, 0)),
            k_spec,
            pl.BlockSpec((tm, A_V), lambda i: (i, 0)),
        ],
        out_shape=[
            jax.ShapeDtypeStruct((n, A_QK), BF16),
            k_shape,
            jax.ShapeDtypeStruct((n, A_V), F32),
        ],
        compiler_params=_cparams(1),
        name="inproj_attn",
    )(x2, ln_w, w_qkv, w_kt, qn_row, kn_row, kn_col)


def _inproj_delta_kernel(x_ref, ln_ref, w_ref, wbg_ref, cw_ref, cbuf_ref, alog_ref, dtb_ref,
                         dqkv_ref, bg_ref, cst_ref, carry, *, tiles_per_seq, seqs_per_tile):
    i = pl.program_id(0)
    tm = x_ref.shape[0]
    h = _rms_rows(x_ref[...], ln_ref[...]).astype(BF16)
    zs = [_dot(h, w_ref[:, j * B_QK:(j + 1) * B_QK]) for j in range(3)]
    zb = _dot(h, wbg_ref[...])

    lane = lax.broadcasted_iota(jnp.int32, zb.shape, 1)
    a = zb + dtb_ref[...]
    softplus = jnp.maximum(a, 0.0) + jnp.log1p(jnp.exp(-jnp.abs(a)))
    g = -jnp.exp(alog_ref[...]) * softplus
    bg_ref[...] = jnp.where(lane < B_HEADS, _sigmoid(zb), jnp.where(lane < 2 * B_HEADS, g, 0.0))

    first = (i % tiles_per_seq) == 0
    seg = tm // seqs_per_tile
    for j, zfull_tile in enumerate(zs):
        cols = slice(j * B_QK, (j + 1) * B_QK)
        cw = cw_ref[:, cols]
        for sq in range(seqs_per_tile):
            rows = slice(sq * seg, (sq + 1) * seg)
            z = zfull_tile[rows, :]
            prev = jnp.where(first, cbuf_ref[sq, :, cols], carry[:, cols])
            tail = z[seg - SUBLANES:, :]
            if sq == seqs_per_tile - 1:
                carry[:, cols] = tail
            cst_ref[sq, :, cols] = tail
            zfull = jnp.concatenate([prev, z], axis=0)
            y = z * cw[CONV_W - 1:CONV_W, :]
            for s in range(1, CONV_W):
                shifted = pltpu.roll(zfull, s, axis=0)[SUBLANES:, :]
                y = y + shifted * cw[CONV_W - 1 - s:CONV_W - s, :]
            y = _silu(y)
            if j < 2:
                y = y * lax.rsqrt(_group_sumsq(y, 7) + EPS)
            if j == 0:
                y = y * (B_DK ** -0.5)
            dqkv_ref[rows, cols] = y


def _inproj_delta(x2, ln_w, w_d, w_bg, conv_w, cbuf8, alog_row, dtb_row, tm, seq_len):
    n = x2.shape[0]
    tiles_per_seq = max(1, seq_len // tm)
    spt = max(1, tm // seq_len)
    n_tiles = n // tm
    const = lambda i: (0, 0)
    kern = functools.partial(_inproj_delta_kernel, tiles_per_seq=tiles_per_seq, seqs_per_tile=spt)
    return pl.pallas_call(
        kern,
        grid=(n_tiles,),
        in_specs=[
            pl.BlockSpec((tm, D_MODEL), lambda i: (i, 0)),
            pl.BlockSpec((1, D_MODEL), const),
            pl.BlockSpec((D_MODEL, CONV_CH), const, pipeline_mode=pl.Buffered(1)),
            pl.BlockSpec((D_MODEL, LANES), const, pipeline_mode=pl.Buffered(1)),
            pl.BlockSpec((CONV_W, CONV_CH), const),
            pl.BlockSpec((spt, SUBLANES, CONV_CH), lambda i: (i // tiles_per_seq, 0, 0)),
            pl.BlockSpec((1, LANES), const),
            pl.BlockSpec((1, LANES), const),
        ],
        out_specs=[
            pl.BlockSpec((tm, CONV_CH), lambda i: (i, 0)),
            pl.BlockSpec((tm, LANES), lambda i: (i, 0)),
            pl.BlockSpec((spt, SUBLANES, CONV_CH), lambda i: (i, 0, 0)),
        ],
        out_shape=[
            jax.ShapeDtypeStruct((n, CONV_CH), F32),
            jax.ShapeDtypeStruct((n, LANES), F32),
            jax.ShapeDtypeStruct((n_tiles * spt, SUBLANES, CONV_CH), F32),
        ],
        scratch_shapes=[pltpu.VMEM((SUBLANES, CONV_CH), F32)],
        compiler_params=_cparams(1),
        name="inproj_delta",
    )(x2, ln_w, w_d, w_bg, conv_w, cbuf8, alog_row, dtb_row)


def _lambda_value(lq1, lk1, lq2, lk2, lam_init):
    d1 = jnp.sum(lq1[...] * lk1[...], axis=-1, keepdims=True)
    d2 = jnp.sum(lq2[...] * lk2[...], axis=-1, keepdims=True)
    return jnp.exp(d1) - jnp.exp(d2) + lam_init


def _split_maps(q):
    lane = lax.broadcasted_iota(jnp.int32, q.shape, 1)
    zero = jnp.zeros_like(q)
    return jnp.where(lane < A_DIM, q, zero), jnp.where(lane >= A_DIM, q, zero)


def _attn_finish(o, sw, lam_init):
    ms = jnp.mean(o * o, axis=-1, keepdims=True)
    return (o * lax.rsqrt(ms + EPS) * sw * (1.0 - lam_init)).astype(BF16)


def _rows_to_sublanes(x, op):
    return functools.reduce(op, [x[i * SUBLANES:(i + 1) * SUBLANES] for i in range(x.shape[0] // SUBLANES)])


def _attn_prompt_kernel(q_ref, k_ref, v_ref, lq1, lk1, lq2, lk2, swc_ref, o_ref,
                        kb_scr, vt_scr, s_scr, p_scr, *, tq, lam_init):
    seq = q_ref.shape[1]
    lam = _lambda_value(lq1, lk1, lq2, lk2, lam_init)
    kb_scr[...] = k_ref[0].T.astype(BF16)
    vt_scr[0:A_VDIM, :] = v_ref[0].T.astype(BF16)
    vt_scr[A_VDIM:, :] = jnp.ones((vt_scr.shape[0] - A_VDIM, seq), BF16)
    ri = lax.broadcasted_iota(jnp.int32, (tq, 2 * tq), 0)
    ci = lax.broadcasted_iota(jnp.int32, (tq, 2 * tq), 1)
    diag_mask = (ri // CHUNK) <= ((ci & (tq - 1)) // CHUNK)
    n_q = seq // tq
    n_slots = s_scr.shape[0]

    def stacked_queries(qi):
        q2 = jnp.concatenate(_split_maps(q_ref[0, qi * tq:(qi + 1) * tq, :]), axis=0)
        return q2.astype(F32).T

    kt = k_ref[0]
    k_sq = kt * kt
    kmax = [jnp.max(jnp.sum(k_sq[m * A_DIM:(m + 1) * A_DIM], axis=0, keepdims=True),
                    axis=1, keepdims=True) for m in range(2)]
    kmax_row = jnp.where(lax.broadcasted_iota(jnp.int32, (1, 2 * tq), 1) < tq, kmax[0], kmax[1])

    def bounded_phase(qi):
        qf = stacked_queries(qi)
        bound = jnp.sqrt(jnp.sum(qf * qf, axis=0, keepdims=True) * kmax_row) * BOUND_SLACK + 1.0
        q2t = qf.astype(BF16)
        for j in range(qi + 1):
            p = jnp.exp2(_dot(kb_scr[j * tq:(j + 1) * tq, :], q2t) - bound)
            if j == qi:
                p = jnp.where(diag_mask, p, 0.0)
            p_scr[qi % n_slots, j * tq:(j + 1) * tq, :] = p.astype(BF16)

    def score_phase(qi):
        q2t = stacked_queries(qi).astype(BF16)
        m8 = None
        for j in range(qi + 1):
            st = _dot(kb_scr[j * tq:(j + 1) * tq, :], q2t)
            if j == qi:
                st = jnp.where(diag_mask, st, NEG_BIG)
            s_scr[qi % n_slots, j * tq:(j + 1) * tq, :] = st
            part = _rows_to_sublanes(st, jnp.maximum)
            m8 = part if m8 is None else jnp.maximum(m8, part)
        return jnp.max(m8, axis=0, keepdims=True)

    def exp_phase(qi, m_row):
        for j in range(qi + 1):
            p = jnp.exp2(s_scr[qi % n_slots, j * tq:(j + 1) * tq, :] - m_row)
            p_scr[qi % n_slots, j * tq:(j + 1) * tq, :] = p.astype(BF16)

    def value_phase(qi):
        n_keys = (qi + 1) * tq
        acc_t = _dot(vt_scr[:, :n_keys], p_scr[qi % n_slots, :n_keys, :])
        o2t = acc_t[:A_VDIM] / acc_t[A_VDIM:A_VDIM + 1]
        ot = o2t[:, :tq] - lam * o2t[:, tq:]
        ms = jnp.mean(ot * ot, axis=0, keepdims=True)
        ot = ot * lax.rsqrt(ms + EPS) * swc_ref[...] * (1.0 - lam_init)
        o_ref[0, qi * tq:(qi + 1) * tq, :] = ot.T.astype(BF16)
        return acc_t[A_VDIM:A_VDIM + 1]

    bounded_phase(0)
    den_lo = den_hi = None
    for qi in range(n_q):
        if qi + 1 < n_q:
            bounded_phase(qi + 1)
        den = value_phase(qi)
        den_lo = den if den_lo is None else jnp.minimum(den_lo, den)
        den_hi = den if den_hi is None else jnp.maximum(den_hi, den)
    trusted = jnp.logical_and(jnp.min(den_lo) >= DEN_RANGE[0], jnp.max(den_hi) <= DEN_RANGE[1])

    @pl.when(jnp.logical_not(trusted))
    def _():
        m_row = score_phase(0)
        for qi in range(n_q):
            exp_phase(qi, m_row)
            if qi + 1 < n_q:
                m_row = score_phase(qi + 1)
            value_phase(qi)


def _attn_prompt(q3, kt3, v3, lams, sw_col, lam_init, tq):
    b, l, _ = q3.shape
    vec = lambda bi, h: (0, 0)
    head = pl.BlockSpec((1, l, LANES), lambda bi, h: (bi, 0, h))
    kern = functools.partial(_attn_prompt_kernel, tq=tq, lam_init=lam_init)
    return pl.pallas_call(
        kern,
        grid=(b, A_HEADS),
        in_specs=[
            head, pl.BlockSpec((1, 2 * A_DIM, l), lambda bi, h: (bi, h, 0)), head,
            pl.BlockSpec((1, A_DIM), vec), pl.BlockSpec((1, A_DIM), vec),
            pl.BlockSpec((1, A_DIM), vec), pl.BlockSpec((1, A_DIM), vec),
            pl.BlockSpec((A_VDIM, 1), vec),
        ],
        out_specs=head,
        out_shape=jax.ShapeDtypeStruct((b, l, A_V), BF16),
        scratch_shapes=[
            pltpu.VMEM((l, 2 * A_DIM), BF16), pltpu.VMEM((A_VDIM + 2 * SUBLANES, l), BF16),
            pltpu.VMEM((2, l, 2 * tq), F32), pltpu.VMEM((2, l, 2 * tq), BF16),
        ],
        compiler_params=_cparams(2),
        name="attn_prompt",
    )(q3, kt3, v3, *lams, sw_col)


def _attn_sample_kernel(q_ref, kc_ref, vc_ref, kn_ref, vn_ref, lq1, lk1, lq2, lk2, sw_ref, o_ref,
                        *, lam_init):
    h = pl.program_id(1)
    l = q_ref.shape[1]
    past = kc_ref.shape[3]
    lam = _lambda_value(lq1, lk1, lq2, lk2, lam_init)
    q2 = jnp.concatenate(_split_maps(q_ref[0]), axis=0)
    vc = vc_ref[0, pl.ds(h, past, stride=A_HEADS), :].astype(BF16)
    sc = _dot(q2, kc_ref[0, 0].astype(BF16))
    sn = _dot_nt(q2, kn_ref[0].astype(BF16))
    mx = jnp.maximum(jnp.max(sc, axis=-1, keepdims=True), jnp.max(sn, axis=-1, keepdims=True))
    pc = jnp.exp2(sc - mx)
    pn = jnp.exp2(sn - mx)
    den = jnp.sum(pc, axis=-1, keepdims=True) + jnp.sum(pn, axis=-1, keepdims=True)
    o2 = (_dot(pc.astype(BF16), vc) + _dot(pn.astype(BF16), vn_ref[0].astype(BF16))) / den
    o_ref[0] = _attn_finish(o2[:l] - lam * o2[l:], sw_ref[...], lam_init)


def _attn_sample(q3, kct, vc2, kn3, vn3, lams, sw_row, lam_init):
    b, l, _ = q3.shape
    past = kct.shape[3]
    vec = lambda bi, h: (0, 0)
    blk = lambda rows: pl.BlockSpec((1, rows, LANES), lambda bi, h: (bi, 0, h))
    kern = functools.partial(_attn_sample_kernel, lam_init=lam_init)
    return pl.pallas_call(
        kern,
        grid=(b, A_HEADS),
        in_specs=[
            blk(l),
            pl.BlockSpec((1, 1, 2 * A_DIM, past), lambda bi, h: (bi, h, 0, 0)),
            pl.BlockSpec((1, past * A_HEADS, A_VDIM), lambda bi, h: (bi, 0, 0)),
            blk(l), blk(l),
            pl.BlockSpec((1, A_DIM), vec), pl.BlockSpec((1, A_DIM), vec),
            pl.BlockSpec((1, A_DIM), vec), pl.BlockSpec((1, A_DIM), vec),
            pl.BlockSpec((1, A_VDIM), vec),
        ],
        out_specs=blk(l),
        out_shape=jax.ShapeDtypeStruct((b, l, A_V), BF16),
        compiler_params=_cparams(2),
        name="attn_sample",
    )(q3, kct, vc2, kn3, vn3, *lams, sw_row)


def _delta_kernel(q_ref, k_ref, v_ref, bg_ref, s0_ref, nw_ref, o_ref, sout_ref, s_scr,
                  *, c, nblk, hb, gb):
    t = pl.program_id(2)

    @pl.when(t == 0)
    def _():
        s_scr[...] = s0_ref[0]

    bg = bg_ref[0]
    lane = lax.broadcasted_iota(jnp.int32, bg.shape, 1)
    bg_t = bg.T
    sub = lax.broadcasted_iota(jnp.int32, bg_t.shape, 0)
    ri = lax.broadcasted_iota(jnp.int32, (c, c), 0)
    ci = lax.broadcasted_iota(jnp.int32, (c, c), 1)
    causal = ri >= ci
    strict = ri > ci
    ltri = causal.astype(BF16)
    utri = (ri <= ci).astype(BF16)
    eye = jnp.where(ri == ci, 1.0, 0.0)

    gates = []
    for hh in range(hb):
        h = pl.program_id(1) * hb + hh
        gates.append((
            jnp.sum(jnp.where(lane == h, bg, 0.0), axis=-1, keepdims=True),
            jnp.sum(jnp.where(lane == h + B_HEADS, bg, 0.0), axis=-1, keepdims=True),
            jnp.sum(jnp.where(sub == h + B_HEADS, bg_t, 0.0), axis=0, keepdims=True)))
    pre = {}
    states = [s_scr[hh] for hh in range(hb)]

    def prepare(blocks):
        items = [(hh, blk) for hh in range(hb) for blk in blocks]
        for hh, blk in items:
            rows = slice(blk * c, (blk + 1) * c)
            cols = slice(hh * LANES, (hh + 1) * LANES)
            bc = gates[hh][0][rows, :]
            q = q_ref[0, rows, cols]
            k = k_ref[0, rows, cols]
            v = v_ref[0, rows, cols]
            ghi, glo = _split(jnp.broadcast_to(gates[hh][1][rows, :], (c, LANES)))
            g_cum = _dot(ltri, ghi) + _dot(ltri, glo)
            rhi, rlo = _split(jnp.broadcast_to(gates[hh][2][:, rows], (c, c)))
            g_cum_row = _dot(rhi, utri) + _dot(rlo, utri)
            diff = jnp.where(causal, g_cum[:, :c] - g_cum_row, 0.0)
            decay = jnp.where(causal, jnp.exp(diff), 0.0)
            kb = k.astype(BF16)
            m = jnp.where(strict, bc * _dot_nt(kb, kb) * decay, 0.0)
            pre[hh, blk] = dict(q=q, k=k, v=v, bc=bc, g_cum=g_cum, decay=decay, kb=kb)
            pre[hh, blk]["pq"] = (eye - m, m)
        yield
        for _ in range(int(math.log2(c)) - 1):
            for it in items:
                pm, qm = pre[it]["pq"]
                qb = qm.astype(BF16)
                pre[it]["pq"] = (pm, _dot(qb, qb))
            yield
            for it in items:
                pm, qm = pre[it]["pq"]
                pre[it]["pq"] = (pm + _dot(pm.astype(BF16), qm.astype(BF16)), qm)
            yield
        for it in items:
            d = pre[it]
            e_g = jnp.exp(d["g_cum"])
            rhs = jnp.concatenate([d["v"] * d["bc"], d["k"] * (d["bc"] * e_g)], axis=1)
            sol = _dot(d.pop("pq")[0].astype(BF16), rhs.astype(BF16))
            g_last = d["g_cum"][c - 1:c, :]
            d.update(
                u=sol[:, :B_DV], wb=sol[:, B_DV:].astype(BF16),
                a_loc=jnp.where(causal, _dot_nt(d["q"].astype(BF16), d["kb"]) * d["decay"],
                                0.0).astype(BF16),
                q_dec=(d["q"] * e_g).astype(BF16),
                k_tail_t=(d["k"] * jnp.exp(g_last - d["g_cum"])).T.astype(BF16),
                g_tail=jnp.exp(g_last))
        yield

    def scan(blocks):
        for blk in blocks:
            vbs, o_state = [], []
            for hh in range(hb):
                d = pre[hh, blk]
                sb = states[hh].astype(BF16)
                vbs.append((d["u"] - _dot(d["wb"], sb)).astype(BF16))
                o_state.append(_dot(d["q_dec"], sb))
            yield
            for hh in range(hb):
                d = pre.pop((hh, blk))
                o = o_state[hh] + _dot(d["a_loc"], vbs[hh])
                states[hh] = states[hh] * d["g_tail"] + _dot(d["k_tail_t"], vbs[hh])
                o_ref[0, blk * c:(blk + 1) * c, hh * LANES:(hh + 1) * LANES] = (
                    _rms_rows(o, nw_ref[...]).astype(BF16))
            yield

    def run(*stage_iters):
        live = list(stage_iters)
        while live:
            live = [it for it in live if next(it, _DONE) is not _DONE]

    groups = [list(range(g, min(g + gb, nblk))) for g in range(0, nblk, gb)]
    run(prepare(groups[0]))
    for g in range(1, len(groups)):
        run(prepare(groups[g]), scan(groups[g - 1]))
    run(scan(groups[-1]))
    for hh in range(hb):
        s_scr[hh] = states[hh]

    @pl.when(t == pl.num_programs(2) - 1)
    def _():
        for hh in range(hb):
            sout_ref[0, hh] = states[hh]


def _delta(dqkv3, bg3, s0, nw_row, c, tl, hb, gb):
    b, l, _ = dqkv3.shape
    n_hg = B_HEADS // hb
    kern = functools.partial(_delta_kernel, c=c, nblk=tl // c, hb=hb, gb=gb)
    lane_blk = lambda off: pl.BlockSpec((1, tl, hb * LANES), lambda bi, h, t: (bi, t, h + off))
    state = pl.BlockSpec((1, hb, B_DK, B_DV), lambda bi, h, t: (bi, h, 0, 0))
    return pl.pallas_call(
        kern,
        grid=(b, n_hg, l // tl),
        in_specs=[
            lane_blk(0), lane_blk(n_hg), lane_blk(2 * n_hg),
            pl.BlockSpec((1, tl, LANES), lambda bi, h, t: (bi, t, 0)),
            state,
            pl.BlockSpec((1, B_DV), lambda bi, h, t: (0, 0)),
        ],
        out_specs=[lane_blk(0), state],
        out_shape=[
            jax.ShapeDtypeStruct((b, l, B_V), BF16),
            jax.ShapeDtypeStruct((b, B_HEADS, B_DK, B_DV), F32),
        ],
        scratch_shapes=[pltpu.VMEM((hb, B_DK, B_DV), F32)],
        compiler_params=_cparams(3),
        name="delta_rule",
    )(dqkv3, dqkv3, dqkv3, bg3, s0, nw_row)


def _merge_kernel(x_ref, oa_ref, ob_ref, ln_ref, wdg, wga, wgb, wba, wbb, wout, y_ref):
    x = x_ref[...]
    h = _rms_rows(x, ln_ref[...]).astype(BF16)
    ob = ob_ref[...].astype(F32) * _silu(_dot(h, wdg[...]))
    branch_b = _dot(ob.astype(BF16), wbb[...])
    branch_a = _dot(oa_ref[...], wba[...])
    merged = (_sigmoid(_dot(h, wga[...])) * branch_a
              + _sigmoid(_dot(h, wgb[...])) * branch_b)
    y_ref[...] = x + _dot(merged.astype(BF16), wout[...])


def _merge(x2, oa2, ob2, ln_w, wdg, wga, wgb, wba, wbb, wout, tm):
    n = x2.shape[0]
    tile = pl.BlockSpec((tm, D_MODEL), lambda i: (i, 0))
    wspec = pl.BlockSpec((D_MODEL, D_MODEL), lambda i: (0, 0), pipeline_mode=pl.Buffered(1))
    return pl.pallas_call(
        _merge_kernel,
        grid=(n // tm,),
        in_specs=[tile, tile, tile, pl.BlockSpec((1, D_MODEL), lambda i: (0, 0))] + [wspec] * 6,
        out_specs=tile,
        out_shape=jax.ShapeDtypeStruct((n, D_MODEL), F32),
        compiler_params=_cparams(1),
        name="merge_outproj",
    )(x2, oa2, ob2, ln_w, wdg, wga, wgb, wba, wbb, wout)


def _ffn_kernel(x_ref, ln_ref, wgu_ref, wd_ref, y_ref, *, n_chunks):
    x = x_ref[...]
    h = _rms_rows(x, ln_ref[...]).astype(BF16)
    n_tiles = D_FF // MXU_DIM
    edges = [MXU_DIM * ((n_tiles * c + n_chunks - 1) // n_chunks) for c in range(n_chunks + 1)]
    acts = []
    for lo, hi in zip(edges[:-1], edges[1:]):
        gate = _dot(h, wgu_ref[:, lo:hi])
        up = _dot(h, wgu_ref[:, D_FF + lo:D_FF + hi])
        acts.append((_silu(gate) * up).astype(BF16))
    y = x
    for (lo, hi), act in zip(zip(edges[:-1], edges[1:]), acts):
        y = y + _dot(act, wd_ref[lo:hi, :])
    y_ref[...] = y


def _ffn(x2, ln_w, w_gate_up, w_down, tm, n_chunks):
    n = x2.shape[0]
    const = lambda i: (0, 0)
    return pl.pallas_call(
        functools.partial(_ffn_kernel, n_chunks=n_chunks),
        grid=(n // tm,),
        in_specs=[
            pl.BlockSpec((tm, D_MODEL), lambda i: (i, 0)),
            pl.BlockSpec((1, D_MODEL), const),
            pl.BlockSpec((D_MODEL, 2 * D_FF), const, pipeline_mode=pl.Buffered(1)),
            pl.BlockSpec((D_FF, D_MODEL), const, pipeline_mode=pl.Buffered(1)),
        ],
        out_specs=pl.BlockSpec((tm, D_MODEL), lambda i: (i, 0)),
        out_shape=jax.ShapeDtypeStruct((n, D_MODEL), F32),
        compiler_params=_cparams(1),
        name="swiglu_ffn",
    )(x2, ln_w, w_gate_up, w_down)


def _pick_tile(n, target):
    t = min(n, target)
    while n % t:
        t //= 2
    return t


def _prep_weights(ln1_w, w_in, q_norm_w, k_norm_w, lambda_q1, lambda_k1, lambda_q2, lambda_k2,
                  subln_w, w_branch_a, conv_w, a_log, dt_bias, delta_norm_w, w_branch_b,
                  w_out, ln2_w, w_gate_up, w_down):
    o_d = 2 * A_QK + A_V
    o_dg = o_d + CONV_CH
    o_bg = o_dg + B_V
    o_ga = o_bg + 2 * B_HEADS
    o_gb = o_ga + D_MODEL
    pad_lanes = lambda v: jnp.pad(v.astype(F32), (B_HEADS, LANES - 2 * B_HEADS))[None, :]
    return dict(
        ln1=ln1_w[None, :], ln2=ln2_w[None, :],
        w_qkv=w_in[:, :o_d].astype(BF16),
        w_d=w_in[:, o_d:o_dg].astype(BF16),
        w_dg=w_in[:, o_dg:o_bg].astype(BF16),
        w_bg=jnp.pad(w_in[:, o_bg:o_ga], ((0, 0), (0, LANES - 2 * B_HEADS))).astype(BF16),
        w_ga=w_in[:, o_ga:o_gb].astype(BF16),
        w_gb=w_in[:, o_gb:].astype(BF16),
        qn=jnp.tile(q_norm_w, A_QK // A_DIM)[None, :],
        kn=jnp.tile(k_norm_w, A_QK // A_DIM)[None, :],
        kn_col=jnp.tile(k_norm_w, A_QK // A_DIM)[:, None],
        w_kt=w_in[:, A_QK:2 * A_QK].T.astype(BF16),
        lams=tuple(v[None, :] for v in (lambda_q1, lambda_k1, lambda_q2, lambda_k2)),
        subln=subln_w[None, :], subln_col=subln_w[:, None],
        w_ba=w_branch_a.astype(BF16), w_bb=w_branch_b.astype(BF16),
        conv_w=conv_w, alog=pad_lanes(a_log), dtb=pad_lanes(dt_bias),
        dnorm=delta_norm_w[None, :],
        w_out=w_out.astype(BF16),
        w_gate_up=w_gate_up.astype(BF16), w_down=w_down.astype(BF16),
    )


def _layer(x, past_k, past_v, s0, conv_buf, lam_init, p):
    b, l, _ = x.shape
    n = b * l
    x2 = x.reshape(n, D_MODEL)

    prompt = past_k is None
    tm_a = _pick_tile(l, 1024) if prompt else _pick_tile(n, 1024)
    q2, k2, v2 = _inproj_attn(x2, p["ln1"], p["w_qkv"], p["w_kt"], p["qn"], p["kn"], p["kn_col"],
                              tm_a, l, prompt)

    tm_d = _pick_tile(l, 1024) if l >= 512 else l * _pick_tile(b, max(1, 512 // l))
    cbuf8 = jnp.pad(conv_buf, ((0, 0), (SUBLANES - (CONV_W - 1), 0), (0, 0)))
    dqkv, bg, cst = _inproj_delta(x2, p["ln1"], p["w_d"], p["w_bg"], p["conv_w"], cbuf8,
                                  p["alog"], p["dtb"], tm_d, l)
    tiles_per_seq = max(1, l // tm_d)
    new_conv = cst[tiles_per_seq - 1::tiles_per_seq, SUBLANES - (CONV_W - 1):, :]

    q3 = q2.reshape(b, l, A_QK)
    v3 = v2.reshape(b, l, A_V)
    if prompt:
        oa = _attn_prompt(q3, k2, v3, p["lams"], p["subln_col"], lam_init, _pick_tile(l, 256))
        k_out = jnp.transpose(k2.reshape(b, A_HEADS, 2, A_DIM, l), (0, 4, 1, 2, 3))
    else:
        past = past_k.shape[1]
        k3 = k2.reshape(b, l, A_QK)
        kct = jnp.transpose(past_k, (0, 2, 3, 4, 1)).reshape(b, A_HEADS, 2 * A_DIM, past)
        oa = _attn_sample(q3, kct, past_v.reshape(b, past * A_HEADS, A_VDIM), k3, v3, p["lams"],
                          p["subln"], lam_init)
        k_out = k3.reshape(b, l, A_HEADS, 2, A_DIM)

    c = DELTA_BLOCK if l % DELTA_BLOCK == 0 else l
    tl = c * max(1, min(8, l // c))
    ob, s_new = _delta(dqkv.reshape(b, l, CONV_CH), bg.reshape(b, l, LANES), s0, p["dnorm"], c, tl,
                       B_HEADS, 4)

    x1 = _merge(x2, oa.reshape(n, A_V), ob.reshape(n, B_V), p["ln1"], p["w_dg"], p["w_ga"],
                p["w_gb"], p["w_ba"], p["w_bb"], p["w_out"], _pick_tile(n, 512))
    y = _ffn(x1, p["ln2"], p["w_gate_up"], p["w_down"], _pick_tile(n, 512), 2)
    return (y.reshape(b, l, D_MODEL), k_out, v3.reshape(b, l, A_HEADS, A_VDIM), s_new, new_conv)


def kernel(x_prompt, x_sample, cache_k, cache_v, state_delta, state_conv, ln1_w, w_in, q_norm_w,
           k_norm_w, lambda_q1, lambda_k1, lambda_q2, lambda_k2, subln_w, w_branch_a, conv_w,
           a_log, dt_bias, delta_norm_w, w_branch_b, w_out, ln2_w, w_gate_up, w_down):
    depth = ln1_w.shape[0]
    xp, xs = x_prompt, x_sample
    outs_p, outs_s = [], []
    for layer in range(depth):
        lam_init = 0.8 - 0.6 * math.exp(-0.3 * layer)
        p = _prep_weights(*(w[layer] for w in (
            ln1_w, w_in, q_norm_w, k_norm_w, lambda_q1, lambda_k1, lambda_q2, lambda_k2, subln_w,
            w_branch_a, conv_w, a_log, dt_bias, delta_norm_w, w_branch_b, w_out, ln2_w,
            w_gate_up, w_down)))
        bp = xp.shape[0]
        zero_conv = jnp.zeros((bp, CONV_W - 1, CONV_CH), xp.dtype)
        zero_s = jnp.zeros((bp, B_HEADS, B_DK, B_DV), F32)
        xp, *rest_p = _layer(xp, None, None, zero_s, zero_conv, lam_init, p)
        xs, *rest_s = _layer(xs, cache_k[layer], cache_v[layer], state_delta[layer],
                             state_conv[layer], lam_init, p)
        outs_p.append(rest_p)
        outs_s.append(rest_s)
    stack = lambda outs, idx: jnp.stack([o[idx] for o in outs])
    return (xp, xs,
            stack(outs_p, 0), stack(outs_p, 1), stack(outs_p, 2), stack(outs_p, 3),
            stack(outs_s, 0), stack(outs_s, 1), stack(outs_s, 2), stack(outs_s, 3))
```

```python
import functools
import math

import jax
import jax.numpy as jnp
from jax import lax
from jax.experimental import pallas as pl
from jax.experimental.pallas import tpu as pltpu

F32 = jnp.float32
BF16 = jnp.bfloat16

D_MODEL = 1024
CHUNK = 64
A_HEADS = 8
A_DIM = 64
A_VDIM = 2 * A_DIM
B_HEADS = 8
B_DK = 128
B_DV = 128
CONV_W = 4
DELTA_BLOCK = 64
D_FF = -(-8 * D_MODEL // (3 * 256)) * 256
EPS = 1e-6

A_QK = A_HEADS * 2 * A_DIM
A_V = A_HEADS * A_VDIM
B_QK = B_HEADS * B_DK
B_V = B_HEADS * B_DV
CONV_CH = 2 * B_QK + B_V

LANES = 128
SUBLANES = 8
MXU_DIM = 256
NEG_BIG = -1e30
_DONE = object()
BOUND_SLACK = 1.0 + 2.0 ** -6
DEN_RANGE = (2.0 ** -64, 2.0 ** 64)
Q_SCALE = (A_DIM ** -0.5) * math.log2(math.e)
VMEM_LIMIT = 56 * 1024 * 1024


def _cparams(n_axes):
    return pltpu.CompilerParams(dimension_semantics=("arbitrary",) * n_axes,
                                vmem_limit_bytes=VMEM_LIMIT)


def _dot(a, b):
    return jnp.dot(a, b, preferred_element_type=F32)


def _dot_nt(a, b):
    return lax.dot_general(a, b, (((1,), (1,)), ((), ())), preferred_element_type=F32)


def _split(a):
    hi = a.astype(BF16)
    lo = (a - hi.astype(F32)).astype(BF16)
    return hi, lo


def _rms_rows(x, w):
    ms = jnp.mean(x * x, axis=-1, keepdims=True)
    return x * lax.rsqrt(ms + EPS) * w


def _group_sumsq(z, gshift, weight=1.0):
    r = lax.broadcasted_iota(jnp.int32, (MXU_DIM, MXU_DIM), 0) >> gshift
    c = lax.broadcasted_iota(jnp.int32, (MXU_DIM, MXU_DIM), 1) >> gshift
    blk = jnp.where(r == c, weight, 0.0).astype(BF16)
    outs = []
    for s in range(z.shape[1] // MXU_DIM):
        zs = z[:, s * MXU_DIM:(s + 1) * MXU_DIM]
        outs.append(_dot((zs * zs).astype(BF16), blk))
    return jnp.concatenate(outs, axis=1)


def _sigmoid(x):
    return 0.5 * jnp.tanh(0.5 * x) + 0.5


def _silu(x):
    half = 0.5 * x
    return half * jnp.tanh(half) + half


def _inproj_attn_kernel(x_ref, ln_ref, w_ref, wkt_ref, qn_ref, kn_ref, knc_ref, q_ref, k_ref, v_ref,
                        *, k_transposed):
    h = _rms_rows(x_ref[...], ln_ref[...]).astype(BF16)
    zq = _dot(h, w_ref[:, 0:A_QK])
    if k_transposed:
        zk = _dot_nt(wkt_ref[...], h)
    else:
        zk = _dot(h, w_ref[:, A_QK:2 * A_QK])
    v_ref[...] = _dot(h, w_ref[:, 2 * A_QK:])

    n = zq * lax.rsqrt(_group_sumsq(zq, 6) * (1.0 / A_DIM) + EPS) * qn_ref[...]
    q_ref[...] = (n * Q_SCALE).astype(BF16)
    if k_transposed:
        z3 = zk.reshape(A_QK // A_DIM, A_DIM, zk.shape[1])
        ms = jnp.mean(z3 * z3, axis=1, keepdims=True)
        kn3 = knc_ref[...].reshape(A_QK // A_DIM, A_DIM, 1)
        k_ref[0] = (z3 * lax.rsqrt(ms + EPS) * kn3).reshape(zk.shape)
    else:
        k_ref[...] = zk * lax.rsqrt(_group_sumsq(zk, 6) * (1.0 / A_DIM) + EPS) * kn_ref[...]


def _inproj_attn(x2, ln_w, w_qkv, w_kt, qn_row, kn_row, kn_col, tm, seq_len, k_transposed):
    n = x2.shape[0]
    const = lambda i: (0, 0)
    if k_transposed:
        tiles_per_seq = seq_len // tm
        k_spec = pl.BlockSpec((1, A_QK, tm), lambda i: (i // tiles_per_seq, 0, i % tiles_per_seq))
        k_shape = jax.ShapeDtypeStruct((n // seq_len, A_QK, seq_len), F32)
    else:
        k_spec = pl.BlockSpec((tm, A_QK), lambda i: (i, 0))
        k_shape = jax.ShapeDtypeStruct((n, A_QK), F32)
    return pl.pallas_call(
        functools.partial(_inproj_attn_kernel, k_transposed=k_transposed),
        grid=(n // tm,),
        in_specs=[
            pl.BlockSpec((tm, D_MODEL), lambda i: (i, 0)),
            pl.BlockSpec((1, D_MODEL), const),
            pl.BlockSpec((D_MODEL, 2 * A_QK + A_V), const, pipeline_mode=pl.Buffered(1)),
            pl.BlockSpec((A_QK, D_MODEL), const, pipeline_mode=pl.Buffered(1)),
            pl.BlockSpec((1, A_QK), const),
            pl.BlockSpec((1, A_QK), const),
            pl.BlockSpec((A_QK, 1), const),
        ],
        out_specs=[
            pl.BlockSpec((tm, A_QK), lambda i: (i, 0)),
            k_spec,
            pl.BlockSpec((tm, A_V), lambda i: (i, 0)),
        ],
        out_shape=[
            jax.ShapeDtypeStruct((n, A_QK), BF16),
            k_shape,
            jax.ShapeDtypeStruct((n, A_V), F32),
        ],
        compiler_params=_cparams(1),
        name="inproj_attn",
    )(x2, ln_w, w_qkv, w_kt, qn_row, kn_row, kn_col)


def _inproj_delta_kernel(x_ref, ln_ref, w_ref, wbg_ref, cw_ref, cbuf_ref, alog_ref, dtb_ref,
                         dqkv_ref, bg_ref, cst_ref, carry, *, tiles_per_seq, seqs_per_tile):
    i = pl.program_id(0)
    tm = x_ref.shape[0]
    h = _rms_rows(x_ref[...], ln_ref[...]).astype(BF16)
    zs = [_dot(h, w_ref[:, j * B_QK:(j + 1) * B_QK]) for j in range(3)]
    zb = _dot(h, wbg_ref[...])

    lane = lax.broadcasted_iota(jnp.int32, zb.shape, 1)
    a = zb + dtb_ref[...]
    softplus = jnp.maximum(a, 0.0) + jnp.log1p(jnp.exp(-jnp.abs(a)))
    g = -jnp.exp(alog_ref[...]) * softplus
    bg_ref[...] = jnp.where(lane < B_HEADS, _sigmoid(zb), jnp.where(lane < 2 * B_HEADS, g, 0.0))

    first = (i % tiles_per_seq) == 0
    seg = tm // seqs_per_tile
    for j, zfull_tile in enumerate(zs):
        cols = slice(j * B_QK, (j + 1) * B_QK)
        cw = cw_ref[:, cols] * 0.5
        for sq in range(seqs_per_tile):
            rows = slice(sq * seg, (sq + 1) * seg)
            z = zfull_tile[rows, :]
            prev = jnp.where(first, cbuf_ref[sq, :, cols], carry[:, cols])
            tail = z[seg - SUBLANES:, :]
            if sq == seqs_per_tile - 1:
                carry[:, cols] = tail
            cst_ref[sq, :, cols] = tail
            z3 = jnp.concatenate([prev, z], axis=0).reshape(seg // SUBLANES + 1, SUBLANES, B_QK)
            sub = lax.broadcasted_iota(jnp.int32, (1, SUBLANES, 1), 1)
            y = z * cw[CONV_W - 1:CONV_W, :]
            for s in range(1, CONV_W):
                rot = pltpu.roll(z3, s, axis=1)
                shifted = jnp.where(sub < s, rot[:-1], rot[1:]).reshape(seg, B_QK)
                y = y + shifted * cw[CONV_W - 1 - s:CONV_W - s, :]
            y = y * jnp.tanh(y) + y
            if j == 0:
                y = y * lax.rsqrt(_group_sumsq(y, 7, float(B_DK)) + EPS * B_DK)
            elif j == 1:
                y = y * lax.rsqrt(_group_sumsq(y, 7) + EPS)
            dqkv_ref[rows, cols] = y


def _inproj_delta(x2, ln_w, w_d, w_bg, conv_w, cbuf8, alog_row, dtb_row, tm, seq_len):
    n = x2.shape[0]
    tiles_per_seq = max(1, seq_len // tm)
    spt = max(1, tm // seq_len)
    n_tiles = n // tm
    const = lambda i: (0, 0)
    kern = functools.partial(_inproj_delta_kernel, tiles_per_seq=tiles_per_seq, seqs_per_tile=spt)
    return pl.pallas_call(
        kern,
        grid=(n_tiles,),
        in_specs=[
            pl.BlockSpec((tm, D_MODEL), lambda i: (i, 0)),
            pl.BlockSpec((1, D_MODEL), const),
            pl.BlockSpec((D_MODEL, CONV_CH), const, pipeline_mode=pl.Buffered(1)),
            pl.BlockSpec((D_MODEL, LANES), const, pipeline_mode=pl.Buffered(1)),
            pl.BlockSpec((CONV_W, CONV_CH), const),
            pl.BlockSpec((spt, SUBLANES, CONV_CH), lambda i: (i // tiles_per_seq, 0, 0)),
            pl.BlockSpec((1, LANES), const),
            pl.BlockSpec((1, LANES), const),
        ],
        out_specs=[
            pl.BlockSpec((tm, CONV_CH), lambda i: (i, 0)),
            pl.BlockSpec((tm, LANES), lambda i: (i, 0)),
            pl.BlockSpec((spt, SUBLANES, CONV_CH), lambda i: (i, 0, 0)),
        ],
        out_shape=[
            jax.ShapeDtypeStruct((n, CONV_CH), F32),
            jax.ShapeDtypeStruct((n, LANES), F32),
            jax.ShapeDtypeStruct((n_tiles * spt, SUBLANES, CONV_CH), F32),
        ],
        scratch_shapes=[pltpu.VMEM((SUBLANES, CONV_CH), F32)],
        compiler_params=_cparams(1),
        name="inproj_delta",
    )(x2, ln_w, w_d, w_bg, conv_w, cbuf8, alog_row, dtb_row)


def _lambda_value(lq1, lk1, lq2, lk2, lam_init):
    d1 = jnp.sum(lq1[...] * lk1[...], axis=-1, keepdims=True)
    d2 = jnp.sum(lq2[...] * lk2[...], axis=-1, keepdims=True)
    return jnp.exp(d1) - jnp.exp(d2) + lam_init


def _split_maps(q):
    lane = lax.broadcasted_iota(jnp.int32, q.shape, 1)
    zero = jnp.zeros_like(q)
    return jnp.where(lane < A_DIM, q, zero), jnp.where(lane >= A_DIM, q, zero)


def _attn_finish(o, sw, lam_init):
    ms = jnp.mean(o * o, axis=-1, keepdims=True)
    return (o * lax.rsqrt(ms + EPS) * sw * (1.0 - lam_init)).astype(BF16)


def _rows_to_sublanes(x, op):
    return functools.reduce(op, [x[i * SUBLANES:(i + 1) * SUBLANES] for i in range(x.shape[0] // SUBLANES)])


def _attn_prompt_kernel(q_ref, k_ref, v_ref, lq1, lk1, lq2, lk2, swc_ref, o_ref,
                        kb_scr, vt_scr, s_scr, p_scr, *, tq, lam_init):
    seq = q_ref.shape[1]
    lam = _lambda_value(lq1, lk1, lq2, lk2, lam_init)
    kb_scr[...] = k_ref[0].T.astype(BF16)
    vt_scr[0:A_VDIM, :] = v_ref[0].T.astype(BF16)
    vt_scr[A_VDIM:, :] = jnp.ones((vt_scr.shape[0] - A_VDIM, seq), BF16)
    ri = lax.broadcasted_iota(jnp.int32, (tq, 2 * tq), 0)
    ci = lax.broadcasted_iota(jnp.int32, (tq, 2 * tq), 1)
    diag_mask = (ri // CHUNK) <= ((ci & (tq - 1)) // CHUNK)
    n_q = seq // tq
    n_slots = s_scr.shape[0]

    def stacked_queries(qi):
        q2 = jnp.concatenate(_split_maps(q_ref[0, qi * tq:(qi + 1) * tq, :]), axis=0)
        return q2.astype(F32).T

    kt = k_ref[0]
    k_sq = kt * kt
    kmax = [jnp.max(jnp.sum(k_sq[m * A_DIM:(m + 1) * A_DIM], axis=0, keepdims=True),
                    axis=1, keepdims=True) for m in range(2)]
    kmax_row = jnp.where(lax.broadcasted_iota(jnp.int32, (1, 2 * tq), 1) < tq, kmax[0], kmax[1])

    def bounded_phase(qi):
        qf = stacked_queries(qi)
        bound = jnp.sqrt(jnp.sum(qf * qf, axis=0, keepdims=True) * kmax_row) * BOUND_SLACK + 1.0
        q2t = qf.astype(BF16)
        for j in range(qi + 1):
            p = jnp.exp2(_dot(kb_scr[j * tq:(j + 1) * tq, :], q2t) - bound)
            if j == qi:
                p = jnp.where(diag_mask, p, 0.0)
            p_scr[qi % n_slots, j * tq:(j + 1) * tq, :] = p.astype(BF16)

    def score_phase(qi):
        q2t = stacked_queries(qi).astype(BF16)
        m8 = None
        for j in range(qi + 1):
            st = _dot(kb_scr[j * tq:(j + 1) * tq, :], q2t)
            if j == qi:
                st = jnp.where(diag_mask, st, NEG_BIG)
            s_scr[qi % n_slots, j * tq:(j + 1) * tq, :] = st
            part = _rows_to_sublanes(st, jnp.maximum)
            m8 = part if m8 is None else jnp.maximum(m8, part)
        return jnp.max(m8, axis=0, keepdims=True)

    def exp_phase(qi, m_row):
        for j in range(qi + 1):
            p = jnp.exp2(s_scr[qi % n_slots, j * tq:(j + 1) * tq, :] - m_row)
            p_scr[qi % n_slots, j * tq:(j + 1) * tq, :] = p.astype(BF16)

    def value_phase(qi):
        n_keys = (qi + 1) * tq
        acc_t = _dot(vt_scr[:, :n_keys], p_scr[qi % n_slots, :n_keys, :])
        o2t = acc_t[:A_VDIM] / acc_t[A_VDIM:A_VDIM + 1]
        ot = o2t[:, :tq] - lam * o2t[:, tq:]
        ms = jnp.mean(ot * ot, axis=0, keepdims=True)
        ot = ot * lax.rsqrt(ms + EPS) * swc_ref[...] * (1.0 - lam_init)
        o_ref[0, qi * tq:(qi + 1) * tq, :] = ot.T.astype(BF16)
        return acc_t[A_VDIM:A_VDIM + 1]

    bounded_phase(0)
    den_lo = den_hi = None
    for qi in range(n_q):
        if qi + 1 < n_q:
            bounded_phase(qi + 1)
        den = value_phase(qi)
        den_lo = den if den_lo is None else jnp.minimum(den_lo, den)
        den_hi = den if den_hi is None else jnp.maximum(den_hi, den)
    trusted = jnp.logical_and(jnp.min(den_lo) >= DEN_RANGE[0], jnp.max(den_hi) <= DEN_RANGE[1])

    @pl.when(jnp.logical_not(trusted))
    def _():
        m_row = score_phase(0)
        for qi in range(n_q):
            exp_phase(qi, m_row)
            if qi + 1 < n_q:
                m_row = score_phase(qi + 1)
            value_phase(qi)


def _attn_prompt(q3, kt3, v3, lams, sw_col, lam_init, tq):
    b, l, _ = q3.shape
    vec = lambda bi, h: (0, 0)
    head = pl.BlockSpec((1, l, LANES), lambda bi, h: (bi, 0, h))
    kern = functools.partial(_attn_prompt_kernel, tq=tq, lam_init=lam_init)
    return pl.pallas_call(
        kern,
        grid=(b, A_HEADS),
        in_specs=[
            head, pl.BlockSpec((1, 2 * A_DIM, l), lambda bi, h: (bi, h, 0)), head,
            pl.BlockSpec((1, A_DIM), vec), pl.BlockSpec((1, A_DIM), vec),
            pl.BlockSpec((1, A_DIM), vec), pl.BlockSpec((1, A_DIM), vec),
            pl.BlockSpec((A_VDIM, 1), vec),
        ],
        out_specs=head,
        out_shape=jax.ShapeDtypeStruct((b, l, A_V), BF16),
        scratch_shapes=[
            pltpu.VMEM((l, 2 * A_DIM), BF16), pltpu.VMEM((A_VDIM + 2 * SUBLANES, l), BF16),
            pltpu.VMEM((2, l, 2 * tq), F32), pltpu.VMEM((2, l, 2 * tq), BF16),
        ],
        compiler_params=_cparams(2),
        name="attn_prompt",
    )(q3, kt3, v3, *lams, sw_col)


def _attn_sample_kernel(q_ref, kc_ref, vc_ref, kn_ref, vn_ref, lq1, lk1, lq2, lk2, sw_ref, o_ref,
                        *, lam_init):
    h = pl.program_id(1)
    l = q_ref.shape[1]
    past = kc_ref.shape[3]
    lam = _lambda_value(lq1, lk1, lq2, lk2, lam_init)
    q2 = jnp.concatenate(_split_maps(q_ref[0]), axis=0)
    vc = vc_ref[0, pl.ds(h, past, stride=A_HEADS), :].astype(BF16)
    sc = _dot(q2, kc_ref[0, 0].astype(BF16))
    sn = _dot_nt(q2, kn_ref[0].astype(BF16))
    mx = jnp.maximum(jnp.max(sc, axis=-1, keepdims=True), jnp.max(sn, axis=-1, keepdims=True))
    pc = jnp.exp2(sc - mx)
    pn = jnp.exp2(sn - mx)
    den = jnp.sum(pc, axis=-1, keepdims=True) + jnp.sum(pn, axis=-1, keepdims=True)
    o2 = (_dot(pc.astype(BF16), vc) + _dot(pn.astype(BF16), vn_ref[0].astype(BF16))) / den
    o_ref[0] = _attn_finish(o2[:l] - lam * o2[l:], sw_ref[...], lam_init)


def _attn_sample(q3, kct, vc2, kn3, vn3, lams, sw_row, lam_init):
    b, l, _ = q3.shape
    past = kct.shape[3]
    vec = lambda bi, h: (0, 0)
    blk = lambda rows: pl.BlockSpec((1, rows, LANES), lambda bi, h: (bi, 0, h))
    kern = functools.partial(_attn_sample_kernel, lam_init=lam_init)
    return pl.pallas_call(
        kern,
        grid=(b, A_HEADS),
        in_specs=[
            blk(l),
            pl.BlockSpec((1, 1, 2 * A_DIM, past), lambda bi, h: (bi, h, 0, 0)),
            pl.BlockSpec((1, past * A_HEADS, A_VDIM), lambda bi, h: (bi, 0, 0)),
            blk(l), blk(l),
            pl.BlockSpec((1, A_DIM), vec), pl.BlockSpec((1, A_DIM), vec),
            pl.BlockSpec((1, A_DIM), vec), pl.BlockSpec((1, A_DIM), vec),
            pl.BlockSpec((1, A_VDIM), vec),
        ],
        out_specs=blk(l),
        out_shape=jax.ShapeDtypeStruct((b, l, A_V), BF16),
        compiler_params=_cparams(2),
        name="attn_sample",
    )(q3, kct, vc2, kn3, vn3, *lams, sw_row)


def _delta_kernel(q_ref, k_ref, v_ref, bg_ref, s0_ref, nw_ref, o_ref, sout_ref, s_scr,
                  *, c, nblk, hb, gb):
    t = pl.program_id(2)

    @pl.when(t == 0)
    def _():
        s_scr[...] = s0_ref[0]

    bg = bg_ref[0]
    lane = lax.broadcasted_iota(jnp.int32, bg.shape, 1)
    bg_t = bg.T
    sub = lax.broadcasted_iota(jnp.int32, bg_t.shape, 0)
    ri = lax.broadcasted_iota(jnp.int32, (c, c), 0)
    ci = lax.broadcasted_iota(jnp.int32, (c, c), 1)
    causal = ri >= ci
    strict = ri > ci
    ltri = causal.astype(BF16)
    utri = (ri <= ci).astype(BF16)
    ltri2 = jnp.concatenate([ltri, ltri], axis=1)
    utri2 = jnp.concatenate([utri, utri], axis=0)
    eye = jnp.where(ri == ci, 1.0, 0.0)

    gates = []
    for hh in range(hb):
        h = pl.program_id(1) * hb + hh
        gates.append((
            jnp.sum(jnp.where(lane == h, bg, 0.0), axis=-1, keepdims=True),
            jnp.sum(jnp.where(lane == h + B_HEADS, bg, 0.0), axis=-1, keepdims=True),
            jnp.sum(jnp.where(sub == h + B_HEADS, bg_t, 0.0), axis=0, keepdims=True)))
    pre = {}
    states = [s_scr[hh] for hh in range(hb)]

    def prepare(blocks):
        items = [(hh, blk) for hh in range(hb) for blk in blocks]
        for hh, blk in items:
            rows = slice(blk * c, (blk + 1) * c)
            cols = slice(hh * LANES, (hh + 1) * LANES)
            bc = gates[hh][0][rows, :]
            q = q_ref[0, rows, cols]
            k = k_ref[0, rows, cols]
            v = v_ref[0, rows, cols]
            ghi, glo = _split(jnp.broadcast_to(gates[hh][1][rows, :], (c, LANES)))
            g_cum = _dot(ltri2, jnp.concatenate([ghi, glo], axis=0))
            rhi, rlo = _split(gates[hh][2][:, rows])
            g_rows = jnp.broadcast_to(jnp.concatenate([rhi, rlo], axis=1), (c, 2 * c))
            g_cum_row = _dot(g_rows, utri2)
            diff = jnp.where(causal, g_cum[:, :c] - g_cum_row, 0.0)
            decay = jnp.where(causal, jnp.exp(diff), 0.0)
            kb = k.astype(BF16)
            kq = _dot_nt(jnp.concatenate([kb, q.astype(BF16)], axis=0), kb)
            m = jnp.where(strict, bc * kq[:c] * decay, 0.0)
            pre[hh, blk] = dict(q=q, k=k, v=v, bc=bc, g_cum=g_cum, decay=decay, qk=kq[c:])
            pre[hh, blk]["pq"] = (eye - m, -m)
        yield
        for it in items:
            pm, qm = pre[it]["pq"]
            qb = qm.astype(BF16)
            pre[it]["pq"] = (pm, _dot(qb, qb))
        yield
        for _ in range(int(math.log2(c)) - 2):
            for it in items:
                pm, qm = pre[it]["pq"]
                qb = qm.astype(BF16)
                both = _dot(jnp.concatenate([pm.astype(BF16), qb], axis=0), qb)
                pre[it]["pq"] = (pm + both[:c], both[c:])
            yield
        for it in items:
            pm, qm = pre[it]["pq"]
            pre[it]["pq"] = (pm + _dot(pm.astype(BF16), qm.astype(BF16)), qm)
        yield
        for it in items:
            d = pre[it]
            e_g = jnp.exp(d["g_cum"])
            rhs = jnp.concatenate([d["v"] * d["bc"], d["k"] * (d["bc"] * e_g)], axis=1)
            sol = _dot(d.pop("pq")[0].astype(BF16), rhs.astype(BF16))
            g_last = d["g_cum"][c - 1:c, :]
            a_loc = jnp.where(causal, d["qk"] * d["decay"], 0.0).astype(BF16)
            k_tail_t = (d["k"] * jnp.exp(g_last - d["g_cum"])).T.astype(BF16)
            d.update(
                u=sol[:, :B_DV],
                state_lhs=jnp.concatenate(
                    [sol[:, B_DV:].astype(BF16), (d["q"] * e_g).astype(BF16)], axis=0),
                value_lhs=jnp.concatenate([a_loc, k_tail_t], axis=0),
                g_tail=jnp.exp(g_last))
        yield

    def scan(blocks):
        for blk in blocks:
            vbs, o_state = [], []
            for hh in range(hb):
                d = pre[hh, blk]
                ws = _dot(d["state_lhs"], states[hh].astype(BF16))
                vbs.append((d["u"] - ws[:c]).astype(BF16))
                o_state.append(ws[c:])
            yield
            for hh in range(hb):
                d = pre.pop((hh, blk))
                av = _dot(d["value_lhs"], vbs[hh])
                o = o_state[hh] + av[:c]
                states[hh] = states[hh] * d["g_tail"] + av[c:]
                o_ref[0, blk * c:(blk + 1) * c, hh * LANES:(hh + 1) * LANES] = (
                    _rms_rows(o, nw_ref[...]).astype(BF16))
            yield

    def run(*stage_iters):
        live = list(stage_iters)
        while live:
            live = [it for it in live if next(it, _DONE) is not _DONE]

    groups = [list(range(g, min(g + gb, nblk))) for g in range(0, nblk, gb)]
    run(prepare(groups[0]))
    for g in range(1, len(groups)):
        run(prepare(groups[g]), scan(groups[g - 1]))
    run(scan(groups[-1]))
    for hh in range(hb):
        s_scr[hh] = states[hh]

    @pl.when(t == pl.num_programs(2) - 1)
    def _():
        for hh in range(hb):
            sout_ref[0, hh] = states[hh]


def _delta(dqkv3, bg3, s0, nw_row, c, tl, hb, gb):
    b, l, _ = dqkv3.shape
    n_hg = B_HEADS // hb
    kern = functools.partial(_delta_kernel, c=c, nblk=tl // c, hb=hb, gb=gb)
    lane_blk = lambda off: pl.BlockSpec((1, tl, hb * LANES), lambda bi, h, t: (bi, t, h + off))
    state = pl.BlockSpec((1, hb, B_DK, B_DV), lambda bi, h, t: (bi, h, 0, 0))
    return pl.pallas_call(
        kern,
        grid=(b, n_hg, l // tl),
        in_specs=[
            lane_blk(0), lane_blk(n_hg), lane_blk(2 * n_hg),
            pl.BlockSpec((1, tl, LANES), lambda bi, h, t: (bi, t, 0)),
            state,
            pl.BlockSpec((1, B_DV), lambda bi, h, t: (0, 0)),
        ],
        out_specs=[lane_blk(0), state],
        out_shape=[
            jax.ShapeDtypeStruct((b, l, B_V), BF16),
            jax.ShapeDtypeStruct((b, B_HEADS, B_DK, B_DV), F32),
        ],
        scratch_shapes=[pltpu.VMEM((hb, B_DK, B_DV), F32)],
        compiler_params=_cparams(3),
        name="delta_rule",
    )(dqkv3, dqkv3, dqkv3, bg3, s0, nw_row)


def _merge_kernel(x_ref, oa_ref, ob_ref, ln_ref, wdg, wga, wgb, wba, wbb, wout, y_ref):
    x = x_ref[...]
    h = _rms_rows(x, ln_ref[...]).astype(BF16)
    ob = ob_ref[...].astype(F32) * _silu(_dot(h, wdg[...]))
    branch_b = _dot(ob.astype(BF16), wbb[...])
    branch_a = _dot(oa_ref[...], wba[...])
    merged = (_sigmoid(_dot(h, wga[...])) * branch_a
              + _sigmoid(_dot(h, wgb[...])) * branch_b)
    y_ref[...] = x + _dot(merged.astype(BF16), wout[...])


def _merge(x2, oa2, ob2, ln_w, wdg, wga, wgb, wba, wbb, wout, tm):
    n = x2.shape[0]
    tile = pl.BlockSpec((tm, D_MODEL), lambda i: (i, 0))
    wspec = pl.BlockSpec((D_MODEL, D_MODEL), lambda i: (0, 0), pipeline_mode=pl.Buffered(1))
    return pl.pallas_call(
        _merge_kernel,
        grid=(n // tm,),
        in_specs=[tile, tile, tile, pl.BlockSpec((1, D_MODEL), lambda i: (0, 0))] + [wspec] * 6,
        out_specs=tile,
        out_shape=jax.ShapeDtypeStruct((n, D_MODEL), F32),
        compiler_params=_cparams(1),
        name="merge_outproj",
    )(x2, oa2, ob2, ln_w, wdg, wga, wgb, wba, wbb, wout)


def _ffn_kernel(x_ref, ln_ref, wgu_ref, wd_ref, y_ref, *, n_chunks):
    x = x_ref[...]
    h = _rms_rows(x, ln_ref[...]).astype(BF16)
    n_tiles = D_FF // MXU_DIM
    edges = [MXU_DIM * ((n_tiles * c + n_chunks - 1) // n_chunks) for c in range(n_chunks + 1)]
    acts = []
    for lo, hi in zip(edges[:-1], edges[1:]):
        gate = _dot(h, wgu_ref[:, lo:hi])
        up = _dot(h, wgu_ref[:, D_FF + lo:D_FF + hi])
        acts.append((_silu(gate) * up).astype(BF16))
    y = x
    for (lo, hi), act in zip(zip(edges[:-1], edges[1:]), acts):
        y = y + _dot(act, wd_ref[lo:hi, :])
    y_ref[...] = y


def _ffn(x2, ln_w, w_gate_up, w_down, tm, n_chunks):
    n = x2.shape[0]
    const = lambda i: (0, 0)
    return pl.pallas_call(
        functools.partial(_ffn_kernel, n_chunks=n_chunks),
        grid=(n // tm,),
        in_specs=[
            pl.BlockSpec((tm, D_MODEL), lambda i: (i, 0)),
            pl.BlockSpec((1, D_MODEL), const),
            pl.BlockSpec((D_MODEL, 2 * D_FF), const, pipeline_mode=pl.Buffered(1)),
            pl.BlockSpec((D_FF, D_MODEL), const, pipeline_mode=pl.Buffered(1)),
        ],
        out_specs=pl.BlockSpec((tm, D_MODEL), lambda i: (i, 0)),
        out_shape=jax.ShapeDtypeStruct((n, D_MODEL), F32),
        compiler_params=_cparams(1),
        name="swiglu_ffn",
    )(x2, ln_w, w_gate_up, w_down)


def _pick_tile(n, target):
    t = min(n, target)
    while n % t:
        t //= 2
    return t


def _prep_weights(ln1_w, w_in, q_norm_w, k_norm_w, lambda_q1, lambda_k1, lambda_q2, lambda_k2,
                  subln_w, w_branch_a, conv_w, a_log, dt_bias, delta_norm_w, w_branch_b,
                  w_out, ln2_w, w_gate_up, w_down):
    o_d = 2 * A_QK + A_V
    o_dg = o_d + CONV_CH
    o_bg = o_dg + B_V
    o_ga = o_bg + 2 * B_HEADS
    o_gb = o_ga + D_MODEL
    pad_lanes = lambda v: jnp.pad(v.astype(F32), (B_HEADS, LANES - 2 * B_HEADS))[None, :]
    return dict(
        ln1=ln1_w[None, :], ln2=ln2_w[None, :],
        w_qkv=w_in[:, :o_d].astype(BF16),
        w_d=w_in[:, o_d:o_dg].astype(BF16),
        w_dg=w_in[:, o_dg:o_bg].astype(BF16),
        w_bg=jnp.pad(w_in[:, o_bg:o_ga], ((0, 0), (0, LANES - 2 * B_HEADS))).astype(BF16),
        w_ga=w_in[:, o_ga:o_gb].astype(BF16),
        w_gb=w_in[:, o_gb:].astype(BF16),
        qn=jnp.tile(q_norm_w, A_QK // A_DIM)[None, :],
        kn=jnp.tile(k_norm_w, A_QK // A_DIM)[None, :],
        kn_col=jnp.tile(k_norm_w, A_QK // A_DIM)[:, None],
        w_kt=w_in[:, A_QK:2 * A_QK].T.astype(BF16),
        lams=tuple(v[None, :] for v in (lambda_q1, lambda_k1, lambda_q2, lambda_k2)),
        subln=subln_w[None, :], subln_col=subln_w[:, None],
        w_ba=w_branch_a.astype(BF16), w_bb=w_branch_b.astype(BF16),
        conv_w=conv_w, alog=pad_lanes(a_log), dtb=pad_lanes(dt_bias),
        dnorm=delta_norm_w[None, :],
        w_out=w_out.astype(BF16),
        w_gate_up=w_gate_up.astype(BF16), w_down=w_down.astype(BF16),
    )


def _layer(x, past_k, past_v, s0, conv_buf, lam_init, p):
    b, l, _ = x.shape
    n = b * l
    x2 = x.reshape(n, D_MODEL)

    prompt = past_k is None
    tm_a = _pick_tile(l, 1024) if prompt else _pick_tile(n, 1024)
    q2, k2, v2 = _inproj_attn(x2, p["ln1"], p["w_qkv"], p["w_kt"], p["qn"], p["kn"], p["kn_col"],
                              tm_a, l, prompt)

    tm_d = _pick_tile(l, 1024) if l >= 512 else l * _pick_tile(b, max(1, 512 // l))
    cbuf8 = jnp.pad(conv_buf, ((0, 0), (SUBLANES - (CONV_W - 1), 0), (0, 0)))
    dqkv, bg, cst = _inproj_delta(x2, p["ln1"], p["w_d"], p["w_bg"], p["conv_w"], cbuf8,
                                  p["alog"], p["dtb"], tm_d, l)
    tiles_per_seq = max(1, l // tm_d)
    new_conv = cst[tiles_per_seq - 1::tiles_per_seq, SUBLANES - (CONV_W - 1):, :]

    q3 = q2.reshape(b, l, A_QK)
    v3 = v2.reshape(b, l, A_V)
    if prompt:
        oa = _attn_prompt(q3, k2, v3, p["lams"], p["subln_col"], lam_init, _pick_tile(l, 256))
        k_out = jnp.transpose(k2.reshape(b, A_HEADS, 2, A_DIM, l), (0, 4, 1, 2, 3))
    else:
        past = past_k.shape[1]
        k3 = k2.reshape(b, l, A_QK)
        kct = jnp.transpose(past_k, (0, 2, 3, 4, 1)).reshape(b, A_HEADS, 2 * A_DIM, past)
        oa = _attn_sample(q3, kct, past_v.reshape(b, past * A_HEADS, A_VDIM), k3, v3, p["lams"],
                          p["subln"], lam_init)
        k_out = k3.reshape(b, l, A_HEADS, 2, A_DIM)

    c = DELTA_BLOCK if l % DELTA_BLOCK == 0 else l
    tl = c * max(1, min(8, l // c))
    ob, s_new = _delta(dqkv.reshape(b, l, CONV_CH), bg.reshape(b, l, LANES), s0, p["dnorm"], c, tl,
                       B_HEADS, 4)

    x1 = _merge(x2, oa.reshape(n, A_V), ob.reshape(n, B_V), p["ln1"], p["w_dg"], p["w_ga"],
                p["w_gb"], p["w_ba"], p["w_bb"], p["w_out"], _pick_tile(n, 512))
    y = _ffn(x1, p["ln2"], p["w_gate_up"], p["w_down"], _pick_tile(n, 512), 2)
    return (y.reshape(b, l, D_MODEL), k_out, v3.reshape(b, l, A_HEADS, A_VDIM), s_new, new_conv)


def kernel(x_prompt, x_sample, cache_k, cache_v, state_delta, state_conv, ln1_w, w_in, q_norm_w,
           k_norm_w, lambda_q1, lambda_k1, lambda_q2, lambda_k2, subln_w, w_branch_a, conv_w,
           a_log, dt_bias, delta_norm_w, w_branch_b, w_out, ln2_w, w_gate_up, w_down):
    depth = ln1_w.shape[0]
    xp, xs = x_prompt, x_sample
    outs_p, outs_s = [], []
    for layer in range(depth):
        lam_init = 0.8 - 0.6 * math.exp(-0.3 * layer)
        p = _prep_weights(*(w[layer] for w in (
            ln1_w, w_in, q_norm_w, k_norm_w, lambda_q1, lambda_k1, lambda_q2, lambda_k2, subln_w,
            w_branch_a, conv_w, a_log, dt_bias, delta_norm_w, w_branch_b, w_out, ln2_w,
            w_gate_up, w_down)))
        bp = xp.shape[0]
        zero_conv = jnp.zeros((bp, CONV_W - 1, CONV_CH), xp.dtype)
        zero_s = jnp.zeros((bp, B_HEADS, B_DK, B_DV), F32)
        xp, *rest_p = _layer(xp, None, None, zero_s, zero_conv, lam_init, p)
        xs, *rest_s = _layer(xs, cache_k[layer], cache_v[layer], state_delta[layer],
                             state_conv[layer], lam_init, p)
        outs_p.append(rest_p)
        outs_s.append(rest_s)
    stack = lambda outs, idx: jnp.stack([o[idx] for o in outs])
    return (xp, xs,
            stack(outs_p, 0), stack(outs_p, 1), stack(outs_p, 2), stack(outs_p, 3),
            stack(outs_s, 0), stack(outs_s, 1), stack(outs_s, 2), stack(outs_s, 3))
```

```python
import functools
import math

import jax
import jax.numpy as jnp
from jax import lax
from jax.experimental import pallas as pl
from jax.experimental.pallas import tpu as pltpu

F32 = jnp.float32
BF16 = jnp.bfloat16

D_MODEL = 1024
CHUNK = 64
A_HEADS = 8
A_DIM = 64
A_VDIM = 2 * A_DIM
B_HEADS = 8
B_DK = 128
B_DV = 128
CONV_W = 4
DELTA_BLOCK = 64
D_FF = -(-8 * D_MODEL // (3 * 256)) * 256
EPS = 1e-6

A_QK = A_HEADS * 2 * A_DIM
A_V = A_HEADS * A_VDIM
B_QK = B_HEADS * B_DK
B_V = B_HEADS * B_DV
CONV_CH = 2 * B_QK + B_V

LANES = 128
SUBLANES = 8
MXU_DIM = 256
NEG_BIG = -1e30
_DONE = object()
BOUND_SLACK = 1.0 + 2.0 ** -6
DEN_RANGE = (2.0 ** -64, 2.0 ** 64)
Q_SCALE = (A_DIM ** -0.5) * math.log2(math.e)
VMEM_LIMIT = 56 * 1024 * 1024


def _cparams(n_axes):
    return pltpu.CompilerParams(dimension_semantics=("arbitrary",) * n_axes,
                                vmem_limit_bytes=VMEM_LIMIT)


def _dot(a, b):
    return jnp.dot(a, b, preferred_element_type=F32)


def _dot_nt(a, b):
    return lax.dot_general(a, b, (((1,), (1,)), ((), ())), preferred_element_type=F32)


def _split(a):
    hi = a.astype(BF16)
    lo = (a - hi.astype(F32)).astype(BF16)
    return hi, lo


def _rms_rows(x, w):
    ms = jnp.mean(x * x, axis=-1, keepdims=True)
    return x * lax.rsqrt(ms + EPS) * w


def _group_sumsq(z, gshift, weight=1.0):
    r = lax.broadcasted_iota(jnp.int32, (MXU_DIM, MXU_DIM), 0) >> gshift
    c = lax.broadcasted_iota(jnp.int32, (MXU_DIM, MXU_DIM), 1) >> gshift
    blk = jnp.where(r == c, weight, 0.0).astype(BF16)
    outs = []
    for s in range(z.shape[1] // MXU_DIM):
        zs = z[:, s * MXU_DIM:(s + 1) * MXU_DIM]
        outs.append(_dot((zs * zs).astype(BF16), blk))
    return jnp.concatenate(outs, axis=1)


def _sigmoid(x):
    return 0.5 * jnp.tanh(0.5 * x) + 0.5


def _silu(x):
    half = 0.5 * x
    return half * jnp.tanh(half) + half


def _inproj_attn_kernel(x_ref, ln_ref, w_ref, wkt_ref, qn_ref, kn_ref, knc_ref, q_ref, k_ref, v_ref,
                        *, k_transposed):
    h = _rms_rows(x_ref[...], ln_ref[...]).astype(BF16)
    zq = _dot(h, w_ref[:, 0:A_QK])
    if k_transposed:
        zk = _dot_nt(wkt_ref[...], h)
    else:
        zk = _dot(h, w_ref[:, A_QK:2 * A_QK])
    v_ref[...] = _dot(h, w_ref[:, 2 * A_QK:])

    n = zq * lax.rsqrt(_group_sumsq(zq, 6) * (1.0 / A_DIM) + EPS) * qn_ref[...]
    q_ref[...] = (n * Q_SCALE).astype(BF16)
    if k_transposed:
        z3 = zk.reshape(A_QK // A_DIM, A_DIM, zk.shape[1])
        ms = jnp.mean(z3 * z3, axis=1, keepdims=True)
        kn3 = knc_ref[...].reshape(A_QK // A_DIM, A_DIM, 1)
        k_ref[0] = (z3 * lax.rsqrt(ms + EPS) * kn3).reshape(zk.shape)
    else:
        k_ref[...] = zk * lax.rsqrt(_group_sumsq(zk, 6) * (1.0 / A_DIM) + EPS) * kn_ref[...]


def _inproj_attn(x2, ln_w, w_qkv, w_kt, qn_row, kn_row, kn_col, tm, seq_len, k_transposed):
    n = x2.shape[0]
    const = lambda i: (0, 0)
    if k_transposed:
        tiles_per_seq = seq_len // tm
        k_spec = pl.BlockSpec((1, A_QK, tm), lambda i: (i // tiles_per_seq, 0, i % tiles_per_seq))
        k_shape = jax.ShapeDtypeStruct((n // seq_len, A_QK, seq_len), F32)
    else:
        k_spec = pl.BlockSpec((tm, A_QK), lambda i: (i, 0))
        k_shape = jax.ShapeDtypeStruct((n, A_QK), F32)
    return pl.pallas_call(
        functools.partial(_inproj_attn_kernel, k_transposed=k_transposed),
        grid=(n // tm,),
        in_specs=[
            pl.BlockSpec((tm, D_MODEL), lambda i: (i, 0)),
            pl.BlockSpec((1, D_MODEL), const),
            pl.BlockSpec((D_MODEL, 2 * A_QK + A_V), const, pipeline_mode=pl.Buffered(1)),
            pl.BlockSpec((A_QK, D_MODEL), const, pipeline_mode=pl.Buffered(1)),
            pl.BlockSpec((1, A_QK), const),
            pl.BlockSpec((1, A_QK), const),
            pl.BlockSpec((A_QK, 1), const),
        ],
        out_specs=[
            pl.BlockSpec((tm, A_QK), lambda i: (i, 0)),
            k_spec,
            pl.BlockSpec((tm, A_V), lambda i: (i, 0)),
        ],
        out_shape=[
            jax.ShapeDtypeStruct((n, A_QK), BF16),
            k_shape,
            jax.ShapeDtypeStruct((n, A_V), F32),
        ],
        compiler_params=_cparams(1),
        name="inproj_attn",
    )(x2, ln_w, w_qkv, w_kt, qn_row, kn_row, kn_col)


def _inproj_delta_kernel(x_ref, ln_ref, w_ref, wbg_ref, cw_ref, cbuf_ref, alog_ref, dtb_ref,
                         dqkv_ref, bg_ref, cst_ref, carry, *, tiles_per_seq, seqs_per_tile):
    i = pl.program_id(0)
    tm = x_ref.shape[0]
    h = _rms_rows(x_ref[...], ln_ref[...]).astype(BF16)
    zs = [_dot(h, w_ref[:, j * B_QK:(j + 1) * B_QK]) for j in range(3)]
    zb = _dot(h, wbg_ref[...])

    lane = lax.broadcasted_iota(jnp.int32, zb.shape, 1)
    a = zb + dtb_ref[...]
    softplus = jnp.maximum(a, 0.0) + jnp.log1p(jnp.exp(-jnp.abs(a)))
    g = -jnp.exp(alog_ref[...]) * softplus
    bg_ref[...] = jnp.where(lane < B_HEADS, _sigmoid(zb), jnp.where(lane < 2 * B_HEADS, g, 0.0))

    first = (i % tiles_per_seq) == 0
    seg = tm // seqs_per_tile
    for j, zfull_tile in enumerate(zs):
        cols = slice(j * B_QK, (j + 1) * B_QK)
        cw = cw_ref[:, cols] * 0.5
        for sq in range(seqs_per_tile):
            rows = slice(sq * seg, (sq + 1) * seg)
            z = zfull_tile[rows, :]
            prev = jnp.where(first, cbuf_ref[sq, :, cols], carry[:, cols])
            tail = z[seg - SUBLANES:, :]
            if sq == seqs_per_tile - 1:
                carry[:, cols] = tail
            cst_ref[sq, :, cols] = tail
            z3 = jnp.concatenate([prev, z], axis=0).reshape(seg // SUBLANES + 1, SUBLANES, B_QK)
            sub = lax.broadcasted_iota(jnp.int32, (1, SUBLANES, 1), 1)
            y = z * cw[CONV_W - 1:CONV_W, :]
            for s in range(1, CONV_W):
                rot = pltpu.roll(z3, s, axis=1)
                shifted = jnp.where(sub < s, rot[:-1], rot[1:]).reshape(seg, B_QK)
                y = y + shifted * cw[CONV_W - 1 - s:CONV_W - s, :]
            y = y * jnp.tanh(y) + y
            if j == 0:
                y = y * lax.rsqrt(_group_sumsq(y, 7, float(B_DK)) + EPS * B_DK)
            elif j == 1:
                y = y * lax.rsqrt(_group_sumsq(y, 7) + EPS)
            dqkv_ref[rows, cols] = y


def _inproj_delta(x2, ln_w, w_d, w_bg, conv_w, cbuf8, alog_row, dtb_row, tm, seq_len):
    n = x2.shape[0]
    tiles_per_seq = max(1, seq_len // tm)
    spt = max(1, tm // seq_len)
    n_tiles = n // tm
    const = lambda i: (0, 0)
    kern = functools.partial(_inproj_delta_kernel, tiles_per_seq=tiles_per_seq, seqs_per_tile=spt)
    return pl.pallas_call(
        kern,
        grid=(n_tiles,),
        in_specs=[
            pl.BlockSpec((tm, D_MODEL), lambda i: (i, 0)),
            pl.BlockSpec((1, D_MODEL), const),
            pl.BlockSpec((D_MODEL, CONV_CH), const, pipeline_mode=pl.Buffered(1)),
            pl.BlockSpec((D_MODEL, LANES), const, pipeline_mode=pl.Buffered(1)),
            pl.BlockSpec((CONV_W, CONV_CH), const),
            pl.BlockSpec((spt, SUBLANES, CONV_CH), lambda i: (i // tiles_per_seq, 0, 0)),
            pl.BlockSpec((1, LANES), const),
            pl.BlockSpec((1, LANES), const),
        ],
        out_specs=[
            pl.BlockSpec((tm, CONV_CH), lambda i: (i, 0)),
            pl.BlockSpec((tm, LANES), lambda i: (i, 0)),
            pl.BlockSpec((spt, SUBLANES, CONV_CH), lambda i: (i, 0, 0)),
        ],
        out_shape=[
            jax.ShapeDtypeStruct((n, CONV_CH), F32),
            jax.ShapeDtypeStruct((n, LANES), F32),
            jax.ShapeDtypeStruct((n_tiles * spt, SUBLANES, CONV_CH), F32),
        ],
        scratch_shapes=[pltpu.VMEM((SUBLANES, CONV_CH), F32)],
        compiler_params=_cparams(1),
        name="inproj_delta",
    )(x2, ln_w, w_d, w_bg, conv_w, cbuf8, alog_row, dtb_row)


def _lambda_value(lq1, lk1, lq2, lk2, lam_init):
    d1 = jnp.sum(lq1[...] * lk1[...], axis=-1, keepdims=True)
    d2 = jnp.sum(lq2[...] * lk2[...], axis=-1, keepdims=True)
    return jnp.exp(d1) - jnp.exp(d2) + lam_init


def _split_maps(q):
    lane = lax.broadcasted_iota(jnp.int32, q.shape, 1)
    zero = jnp.zeros_like(q)
    return jnp.where(lane < A_DIM, q, zero), jnp.where(lane >= A_DIM, q, zero)


def _attn_finish(o, sw, lam_init):
    ms = jnp.mean(o * o, axis=-1, keepdims=True)
    return (o * lax.rsqrt(ms + EPS) * sw * (1.0 - lam_init)).astype(BF16)


def _rows_to_sublanes(x, op):
    return functools.reduce(op, [x[i * SUBLANES:(i + 1) * SUBLANES] for i in range(x.shape[0] // SUBLANES)])


def _attn_prompt_kernel(q_ref, k_ref, v_ref, lq1, lk1, lq2, lk2, swc_ref, o_ref,
                        kb_scr, vt_scr, s_scr, p_scr, *, tq, lam_init):
    seq = q_ref.shape[1]
    lam = _lambda_value(lq1, lk1, lq2, lk2, lam_init)
    kb_scr[...] = k_ref[0].T.astype(BF16)
    vt_scr[0:A_VDIM, :] = v_ref[0].T.astype(BF16)
    vt_scr[A_VDIM:, :] = jnp.ones((vt_scr.shape[0] - A_VDIM, seq), BF16)
    ri = lax.broadcasted_iota(jnp.int32, (tq, 2 * tq), 0)
    ci = lax.broadcasted_iota(jnp.int32, (tq, 2 * tq), 1)
    diag_mask = (ri // CHUNK) <= ((ci & (tq - 1)) // CHUNK)
    n_q = seq // tq
    n_slots = s_scr.shape[0]

    def stacked_queries(qi):
        q2 = jnp.concatenate(_split_maps(q_ref[0, qi * tq:(qi + 1) * tq, :]), axis=0)
        return q2.astype(F32).T

    kt = k_ref[0]
    k_sq = kt * kt
    kmax = [jnp.max(jnp.sum(k_sq[m * A_DIM:(m + 1) * A_DIM], axis=0, keepdims=True),
                    axis=1, keepdims=True) for m in range(2)]
    kmax_row = jnp.where(lax.broadcasted_iota(jnp.int32, (1, 2 * tq), 1) < tq, kmax[0], kmax[1])

    def bounded_phase(qi):
        qf = stacked_queries(qi)
        bound = jnp.sqrt(jnp.sum(qf * qf, axis=0, keepdims=True) * kmax_row) * BOUND_SLACK + 1.0
        q2t = qf.astype(BF16)
        for j in range(qi + 1):
            p = jnp.exp2(_dot(kb_scr[j * tq:(j + 1) * tq, :], q2t) - bound)
            if j == qi:
                p = jnp.where(diag_mask, p, 0.0)
            p_scr[qi % n_slots, j * tq:(j + 1) * tq, :] = p.astype(BF16)

    def score_phase(qi):
        q2t = stacked_queries(qi).astype(BF16)
        m8 = None
        for j in range(qi + 1):
            st = _dot(kb_scr[j * tq:(j + 1) * tq, :], q2t)
            if j == qi:
                st = jnp.where(diag_mask, st, NEG_BIG)
            s_scr[qi % n_slots, j * tq:(j + 1) * tq, :] = st
            part = _rows_to_sublanes(st, jnp.maximum)
            m8 = part if m8 is None else jnp.maximum(m8, part)
        return jnp.max(m8, axis=0, keepdims=True)

    def exp_phase(qi, m_row):
        for j in range(qi + 1):
            p = jnp.exp2(s_scr[qi % n_slots, j * tq:(j + 1) * tq, :] - m_row)
            p_scr[qi % n_slots, j * tq:(j + 1) * tq, :] = p.astype(BF16)

    def value_phase(qi):
        n_keys = (qi + 1) * tq
        acc_t = _dot(vt_scr[:, :n_keys], p_scr[qi % n_slots, :n_keys, :])
        o2t = acc_t[:A_VDIM] / acc_t[A_VDIM:A_VDIM + 1]
        ot = o2t[:, :tq] - lam * o2t[:, tq:]
        ms = jnp.mean(ot * ot, axis=0, keepdims=True)
        ot = ot * lax.rsqrt(ms + EPS) * swc_ref[...] * (1.0 - lam_init)
        o_ref[0, qi * tq:(qi + 1) * tq, :] = ot.T.astype(BF16)
        return acc_t[A_VDIM:A_VDIM + 1]

    bounded_phase(0)
    den_lo = den_hi = None
    for qi in range(n_q):
        if qi + 1 < n_q:
            bounded_phase(qi + 1)
        den = value_phase(qi)
        den_lo = den if den_lo is None else jnp.minimum(den_lo, den)
        den_hi = den if den_hi is None else jnp.maximum(den_hi, den)
    trusted = jnp.logical_and(jnp.min(den_lo) >= DEN_RANGE[0], jnp.max(den_hi) <= DEN_RANGE[1])

    @pl.when(jnp.logical_not(trusted))
    def _():
        m_row = score_phase(0)
        for qi in range(n_q):
            exp_phase(qi, m_row)
            if qi + 1 < n_q:
                m_row = score_phase(qi + 1)
            value_phase(qi)


def _attn_prompt(q3, kt3, v3, lams, sw_col, lam_init, tq):
    b, l, _ = q3.shape
    vec = lambda bi, h: (0, 0)
    head = pl.BlockSpec((1, l, LANES), lambda bi, h: (bi, 0, h))
    kern = functools.partial(_attn_prompt_kernel, tq=tq, lam_init=lam_init)
    return pl.pallas_call(
        kern,
        grid=(b, A_HEADS),
        in_specs=[
            head, pl.BlockSpec((1, 2 * A_DIM, l), lambda bi, h: (bi, h, 0)), head,
            pl.BlockSpec((1, A_DIM), vec), pl.BlockSpec((1, A_DIM), vec),
            pl.BlockSpec((1, A_DIM), vec), pl.BlockSpec((1, A_DIM), vec),
            pl.BlockSpec((A_VDIM, 1), vec),
        ],
        out_specs=head,
        out_shape=jax.ShapeDtypeStruct((b, l, A_V), BF16),
        scratch_shapes=[
            pltpu.VMEM((l, 2 * A_DIM), BF16), pltpu.VMEM((A_VDIM + 2 * SUBLANES, l), BF16),
            pltpu.VMEM((2, l, 2 * tq), F32), pltpu.VMEM((2, l, 2 * tq), BF16),
        ],
        compiler_params=_cparams(2),
        name="attn_prompt",
    )(q3, kt3, v3, *lams, sw_col)


def _attn_sample_kernel(q_ref, kc_ref, vc_ref, kn_ref, vn_ref, lq1, lk1, lq2, lk2, sw_ref, o_ref,
                        *, lam_init, hpb):
    l = q_ref.shape[1]
    past = kc_ref.shape[3]
    lam = _lambda_value(lq1, lk1, lq2, lk2, lam_init)
    for hh in range(hpb):
        h = pl.program_id(1) * hpb + hh
        cols = slice(hh * LANES, (hh + 1) * LANES)
        q2 = jnp.concatenate(_split_maps(q_ref[0, :, cols]), axis=0)
        vc = vc_ref[0, pl.ds(h, past, stride=A_HEADS), :].astype(BF16)
        sc = _dot(q2, kc_ref[0, hh].astype(BF16))
        sn = _dot_nt(q2, kn_ref[0, :, cols].astype(BF16))
        mx = jnp.maximum(jnp.max(sc, axis=-1, keepdims=True), jnp.max(sn, axis=-1, keepdims=True))
        pc = jnp.exp2(sc - mx)
        pn = jnp.exp2(sn - mx)
        den = jnp.sum(pc, axis=-1, keepdims=True) + jnp.sum(pn, axis=-1, keepdims=True)
        o2 = (_dot(pc.astype(BF16), vc)
              + _dot(pn.astype(BF16), vn_ref[0, :, cols].astype(BF16))) / den
        o_ref[0, :, cols] = _attn_finish(o2[:l] - lam * o2[l:], sw_ref[...], lam_init)


def _attn_sample(q3, kct, vc2, kn3, vn3, lams, sw_row, lam_init, hpb=2):
    b, l, _ = q3.shape
    past = kct.shape[3]
    vec = lambda bi, h: (0, 0)
    blk = lambda rows: pl.BlockSpec((1, rows, hpb * LANES), lambda bi, h: (bi, 0, h))
    kern = functools.partial(_attn_sample_kernel, lam_init=lam_init, hpb=hpb)
    return pl.pallas_call(
        kern,
        grid=(b, A_HEADS // hpb),
        in_specs=[
            blk(l),
            pl.BlockSpec((1, hpb, 2 * A_DIM, past), lambda bi, h: (bi, h, 0, 0)),
            pl.BlockSpec((1, past * A_HEADS, A_VDIM), lambda bi, h: (bi, 0, 0)),
            blk(l), blk(l),
            pl.BlockSpec((1, A_DIM), vec), pl.BlockSpec((1, A_DIM), vec),
            pl.BlockSpec((1, A_DIM), vec), pl.BlockSpec((1, A_DIM), vec),
            pl.BlockSpec((1, A_VDIM), vec),
        ],
        out_specs=blk(l),
        out_shape=jax.ShapeDtypeStruct((b, l, A_V), BF16),
        compiler_params=_cparams(2),
        name="attn_sample",
    )(q3, kct, vc2, kn3, vn3, *lams, sw_row)


def _delta_kernel(q_ref, k_ref, v_ref, bg_ref, s0_ref, nw_ref, o_ref, sout_ref, s_scr,
                  *, c, nblk, hb, gb):
    t = pl.program_id(2)

    @pl.when(t == 0)
    def _():
        s_scr[...] = s0_ref[0]

    bg = bg_ref[0]
    lane = lax.broadcasted_iota(jnp.int32, bg.shape, 1)
    bg_t = bg.T
    sub = lax.broadcasted_iota(jnp.int32, bg_t.shape, 0)
    ri = lax.broadcasted_iota(jnp.int32, (c, c), 0)
    ci = lax.broadcasted_iota(jnp.int32, (c, c), 1)
    causal = ri >= ci
    strict = ri > ci
    ltri = causal.astype(BF16)
    utri = (ri <= ci).astype(BF16)
    ltri2 = jnp.concatenate([ltri, ltri], axis=1)
    utri2 = jnp.concatenate([utri, utri], axis=0)
    eye = jnp.where(ri == ci, 1.0, 0.0)

    gates = []
    for hh in range(hb):
        h = pl.program_id(1) * hb + hh
        gates.append((
            jnp.sum(jnp.where(lane == h, bg, 0.0), axis=-1, keepdims=True),
            jnp.sum(jnp.where(lane == h + B_HEADS, bg, 0.0), axis=-1, keepdims=True),
            jnp.sum(jnp.where(sub == h + B_HEADS, bg_t, 0.0), axis=0, keepdims=True)))
    pre = {}
    states = [s_scr[hh] for hh in range(hb)]

    def prepare(blocks):
        items = [(hh, blk) for hh in range(hb) for blk in blocks]
        for hh, blk in items:
            rows = slice(blk * c, (blk + 1) * c)
            cols = slice(hh * LANES, (hh + 1) * LANES)
            bc = gates[hh][0][rows, :]
            q = q_ref[0, rows, cols]
            k = k_ref[0, rows, cols]
            v = v_ref[0, rows, cols]
            ghi, glo = _split(jnp.broadcast_to(gates[hh][1][rows, :], (c, LANES)))
            g_cum = _dot(ltri2, jnp.concatenate([ghi, glo], axis=0))
            rhi, rlo = _split(gates[hh][2][:, rows])
            g_rows = jnp.broadcast_to(jnp.concatenate([rhi, rlo], axis=1), (c, 2 * c))
            g_cum_row = _dot(g_rows, utri2)
            diff = jnp.where(causal, g_cum[:, :c] - g_cum_row, 0.0)
            decay = jnp.where(causal, jnp.exp(diff), 0.0)
            kb = k.astype(BF16)
            kq = _dot_nt(jnp.concatenate([kb, q.astype(BF16)], axis=0), kb)
            m = jnp.where(strict, bc * kq[:c] * decay, 0.0)
            pre[hh, blk] = dict(q=q, k=k, v=v, bc=bc, g_cum=g_cum, decay=decay, qk=kq[c:])
            pre[hh, blk]["pq"] = (eye - m, -m)
        yield
        for it in items:
            pm, qm = pre[it]["pq"]
            qb = qm.astype(BF16)
            pre[it]["pq"] = (pm, _dot(qb, qb))
        yield
        for _ in range(int(math.log2(c)) - 2):
            for it in items:
                pm, qm = pre[it]["pq"]
                qb = qm.astype(BF16)
                both = _dot(jnp.concatenate([pm.astype(BF16), qb], axis=0), qb)
                pre[it]["pq"] = (pm + both[:c], both[c:])
            yield
        for it in items:
            pm, qm = pre[it]["pq"]
            pre[it]["pq"] = (pm + _dot(pm.astype(BF16), qm.astype(BF16)), qm)
        yield
        for it in items:
            d = pre[it]
            e_g = jnp.exp(d["g_cum"])
            rhs = jnp.concatenate([d["v"] * d["bc"], d["k"] * (d["bc"] * e_g)], axis=1)
            sol = _dot(d.pop("pq")[0].astype(BF16), rhs.astype(BF16))
            g_last = d["g_cum"][c - 1:c, :]
            a_loc = jnp.where(causal, d["qk"] * d["decay"], 0.0).astype(BF16)
            k_tail_t = (d["k"] * jnp.exp(g_last - d["g_cum"])).T.astype(BF16)
            d.update(
                u=sol[:, :B_DV],
                state_lhs=jnp.concatenate(
                    [sol[:, B_DV:].astype(BF16), (d["q"] * e_g).astype(BF16)], axis=0),
                value_lhs=jnp.concatenate([a_loc, k_tail_t], axis=0),
                g_tail=jnp.exp(g_last))
        yield

    def scan(blocks):
        for blk in blocks:
            vbs, o_state = [], []
            for hh in range(hb):
                d = pre[hh, blk]
                ws = _dot(d["state_lhs"], states[hh].astype(BF16))
                vbs.append((d["u"] - ws[:c]).astype(BF16))
                o_state.append(ws[c:])
            yield
            for hh in range(hb):
                d = pre.pop((hh, blk))
                av = _dot(d["value_lhs"], vbs[hh])
                o = o_state[hh] + av[:c]
                states[hh] = states[hh] * d["g_tail"] + av[c:]
                o_ref[0, blk * c:(blk + 1) * c, hh * LANES:(hh + 1) * LANES] = (
                    _rms_rows(o, nw_ref[...]).astype(BF16))
            yield

    def run(*stage_iters):
        live = list(stage_iters)
        while live:
            live = [it for it in live if next(it, _DONE) is not _DONE]

    groups = [list(range(g, min(g + gb, nblk))) for g in range(0, nblk, gb)]
    run(prepare(groups[0]))
    for g in range(1, len(groups)):
        run(prepare(groups[g]), scan(groups[g - 1]))
    run(scan(groups[-1]))
    for hh in range(hb):
        s_scr[hh] = states[hh]

    @pl.when(t == pl.num_programs(2) - 1)
    def _():
        for hh in range(hb):
            sout_ref[0, hh] = states[hh]


def _delta(dqkv3, bg3, s0, nw_row, c, tl, hb, gb):
    b, l, _ = dqkv3.shape
    n_hg = B_HEADS // hb
    kern = functools.partial(_delta_kernel, c=c, nblk=tl // c, hb=hb, gb=gb)
    lane_blk = lambda off: pl.BlockSpec((1, tl, hb * LANES), lambda bi, h, t: (bi, t, h + off))
    state = pl.BlockSpec((1, hb, B_DK, B_DV), lambda bi, h, t: (bi, h, 0, 0))
    return pl.pallas_call(
        kern,
        grid=(b, n_hg, l // tl),
        in_specs=[
            lane_blk(0), lane_blk(n_hg), lane_blk(2 * n_hg),
            pl.BlockSpec((1, tl, LANES), lambda bi, h, t: (bi, t, 0)),
            state,
            pl.BlockSpec((1, B_DV), lambda bi, h, t: (0, 0)),
        ],
        out_specs=[lane_blk(0), state],
        out_shape=[
            jax.ShapeDtypeStruct((b, l, B_V), BF16),
            jax.ShapeDtypeStruct((b, B_HEADS, B_DK, B_DV), F32),
        ],
        scratch_shapes=[pltpu.VMEM((hb, B_DK, B_DV), F32)],
        compiler_params=_cparams(3),
        name="delta_rule",
    )(dqkv3, dqkv3, dqkv3, bg3, s0, nw_row)


def _merge_kernel(x_ref, oa_ref, ob_ref, ln_ref, wdg, wga, wgb, wba, wbb, wout, y_ref):
    x = x_ref[...]
    h = _rms_rows(x, ln_ref[...]).astype(BF16)
    ob = ob_ref[...].astype(F32) * _silu(_dot(h, wdg[...]))
    branch_b = _dot(ob.astype(BF16), wbb[...])
    branch_a = _dot(oa_ref[...], wba[...])
    merged = (_sigmoid(_dot(h, wga[...])) * branch_a
              + _sigmoid(_dot(h, wgb[...])) * branch_b)
    y_ref[...] = x + _dot(merged.astype(BF16), wout[...])


def _merge(x2, oa2, ob2, ln_w, wdg, wga, wgb, wba, wbb, wout, tm):
    n = x2.shape[0]
    tile = pl.BlockSpec((tm, D_MODEL), lambda i: (i, 0))
    wspec = pl.BlockSpec((D_MODEL, D_MODEL), lambda i: (0, 0), pipeline_mode=pl.Buffered(1))
    return pl.pallas_call(
        _merge_kernel,
        grid=(n // tm,),
        in_specs=[tile, tile, tile, pl.BlockSpec((1, D_MODEL), lambda i: (0, 0))] + [wspec] * 6,
        out_specs=tile,
        out_shape=jax.ShapeDtypeStruct((n, D_MODEL), F32),
        compiler_params=_cparams(1),
        name="merge_outproj",
    )(x2, oa2, ob2, ln_w, wdg, wga, wgb, wba, wbb, wout)


def _ffn_kernel(x_ref, ln_ref, wgu_ref, wd_ref, y_ref, *, n_chunks):
    x = x_ref[...]
    h = _rms_rows(x, ln_ref[...]).astype(BF16)
    n_tiles = D_FF // MXU_DIM
    edges = [MXU_DIM * ((n_tiles * c + n_chunks - 1) // n_chunks) for c in range(n_chunks + 1)]
    acts = []
    for lo, hi in zip(edges[:-1], edges[1:]):
        gate = _dot(h, wgu_ref[:, lo:hi])
        up = _dot(h, wgu_ref[:, D_FF + lo:D_FF + hi])
        acts.append((_silu(gate) * up).astype(BF16))
    y = x
    for (lo, hi), act in zip(zip(edges[:-1], edges[1:]), acts):
        y = y + _dot(act, wd_ref[lo:hi, :])
    y_ref[...] = y


def _ffn(x2, ln_w, w_gate_up, w_down, tm, n_chunks):
    n = x2.shape[0]
    const = lambda i: (0, 0)
    return pl.pallas_call(
        functools.partial(_ffn_kernel, n_chunks=n_chunks),
        grid=(n // tm,),
        in_specs=[
            pl.BlockSpec((tm, D_MODEL), lambda i: (i, 0)),
            pl.BlockSpec((1, D_MODEL), const),
            pl.BlockSpec((D_MODEL, 2 * D_FF), const, pipeline_mode=pl.Buffered(1)),
            pl.BlockSpec((D_FF, D_MODEL), const, pipeline_mode=pl.Buffered(1)),
        ],
        out_specs=pl.BlockSpec((tm, D_MODEL), lambda i: (i, 0)),
        out_shape=jax.ShapeDtypeStruct((n, D_MODEL), F32),
        compiler_params=_cparams(1),
        name="swiglu_ffn",
    )(x2, ln_w, w_gate_up, w_down)


def _pick_tile(n, target):
    t = min(n, target)
    while n % t:
        t //= 2
    return t


def _prep_weights(ln1_w, w_in, q_norm_w, k_norm_w, lambda_q1, lambda_k1, lambda_q2, lambda_k2,
                  subln_w, w_branch_a, conv_w, a_log, dt_bias, delta_norm_w, w_branch_b,
                  w_out, ln2_w, w_gate_up, w_down):
    o_d = 2 * A_QK + A_V
    o_dg = o_d + CONV_CH
    o_bg = o_dg + B_V
    o_ga = o_bg + 2 * B_HEADS
    o_gb = o_ga + D_MODEL
    pad_lanes = lambda v: jnp.pad(v.astype(F32), (B_HEADS, LANES - 2 * B_HEADS))[None, :]
    return dict(
        ln1=ln1_w[None, :], ln2=ln2_w[None, :],
        w_qkv=w_in[:, :o_d].astype(BF16),
        w_d=w_in[:, o_d:o_dg].astype(BF16),
        w_dg=w_in[:, o_dg:o_bg].astype(BF16),
        w_bg=jnp.pad(w_in[:, o_bg:o_ga], ((0, 0), (0, LANES - 2 * B_HEADS))).astype(BF16),
        w_ga=w_in[:, o_ga:o_gb].astype(BF16),
        w_gb=w_in[:, o_gb:].astype(BF16),
        qn=jnp.tile(q_norm_w, A_QK // A_DIM)[None, :],
        kn=jnp.tile(k_norm_w, A_QK // A_DIM)[None, :],
        kn_col=jnp.tile(k_norm_w, A_QK // A_DIM)[:, None],
        w_kt=w_in[:, A_QK:2 * A_QK].T.astype(BF16),
        lams=tuple(v[None, :] for v in (lambda_q1, lambda_k1, lambda_q2, lambda_k2)),
        subln=subln_w[None, :], subln_col=subln_w[:, None],
        w_ba=w_branch_a.astype(BF16), w_bb=w_branch_b.astype(BF16),
        conv_w=conv_w, alog=pad_lanes(a_log), dtb=pad_lanes(dt_bias),
        dnorm=delta_norm_w[None, :],
        w_out=w_out.astype(BF16),
        w_gate_up=w_gate_up.astype(BF16), w_down=w_down.astype(BF16),
    )


def _layer(x, past_k, past_v, s0, conv_buf, lam_init, p):
    b, l, _ = x.shape
    n = b * l
    x2 = x.reshape(n, D_MODEL)

    prompt = past_k is None
    tm_a = _pick_tile(l, 1024) if prompt else _pick_tile(n, 1024)
    q2, k2, v2 = _inproj_attn(x2, p["ln1"], p["w_qkv"], p["w_kt"], p["qn"], p["kn"], p["kn_col"],
                              tm_a, l, prompt)

    tm_d = _pick_tile(l, 1024) if l >= 512 else l * _pick_tile(b, max(1, 512 // l))
    cbuf8 = jnp.pad(conv_buf, ((0, 0), (SUBLANES - (CONV_W - 1), 0), (0, 0)))
    dqkv, bg, cst = _inproj_delta(x2, p["ln1"], p["w_d"], p["w_bg"], p["conv_w"], cbuf8,
                                  p["alog"], p["dtb"], tm_d, l)
    tiles_per_seq = max(1, l // tm_d)
    new_conv = cst[tiles_per_seq - 1::tiles_per_seq, SUBLANES - (CONV_W - 1):, :]

    q3 = q2.reshape(b, l, A_QK)
    v3 = v2.reshape(b, l, A_V)
    if prompt:
        oa = _attn_prompt(q3, k2, v3, p["lams"], p["subln_col"], lam_init, _pick_tile(l, 256))
        k_out = jnp.transpose(k2.reshape(b, A_HEADS, 2, A_DIM, l), (0, 4, 1, 2, 3))
    else:
        past = past_k.shape[1]
        k3 = k2.reshape(b, l, A_QK)
        kct = jnp.transpose(past_k, (0, 2, 3, 4, 1)).reshape(b, A_HEADS, 2 * A_DIM, past)
        oa = _attn_sample(q3, kct, past_v.reshape(b, past * A_HEADS, A_VDIM), k3, v3, p["lams"],
                          p["subln"], lam_init)
        k_out = k3.reshape(b, l, A_HEADS, 2, A_DIM)

    c = DELTA_BLOCK if l % DELTA_BLOCK == 0 else l
    tl = c * max(1, min(8, l // c))
    ob, s_new = _delta(dqkv.reshape(b, l, CONV_CH), bg.reshape(b, l, LANES), s0, p["dnorm"], c, tl,
                       B_HEADS, 2)

    x1 = _merge(x2, oa.reshape(n, A_V), ob.reshape(n, B_V), p["ln1"], p["w_dg"], p["w_ga"],
                p["w_gb"], p["w_ba"], p["w_bb"], p["w_out"], _pick_tile(n, 512))
    y = _ffn(x1, p["ln2"], p["w_gate_up"], p["w_down"], _pick_tile(n, 512), 2)
    return (y.reshape(b, l, D_MODEL), k_out, v3.reshape(b, l, A_HEADS, A_VDIM), s_new, new_conv)


def kernel(x_prompt, x_sample, cache_k, cache_v, state_delta, state_conv, ln1_w, w_in, q_norm_w,
           k_norm_w, lambda_q1, lambda_k1, lambda_q2, lambda_k2, subln_w, w_branch_a, conv_w,
           a_log, dt_bias, delta_norm_w, w_branch_b, w_out, ln2_w, w_gate_up, w_down):
    depth = ln1_w.shape[0]
    xp, xs = x_prompt, x_sample
    outs_p, outs_s = [], []
    for layer in range(depth):
        lam_init = 0.8 - 0.6 * math.exp(-0.3 * layer)
        p = _prep_weights(*(w[layer] for w in (
            ln1_w, w_in, q_norm_w, k_norm_w, lambda_q1, lambda_k1, lambda_q2, lambda_k2, subln_w,
            w_branch_a, conv_w, a_log, dt_bias, delta_norm_w, w_branch_b, w_out, ln2_w,
            w_gate_up, w_down)))
        bp = xp.shape[0]
        zero_conv = jnp.zeros((bp, CONV_W - 1, CONV_CH), xp.dtype)
        zero_s = jnp.zeros((bp, B_HEADS, B_DK, B_DV), F32)
        xp, *rest_p = _layer(xp, None, None, zero_s, zero_conv, lam_init, p)
        xs, *rest_s = _layer(xs, cache_k[layer], cache_v[layer], state_delta[layer],
                             state_conv[layer], lam_init, p)
        outs_p.append(rest_p)
        outs_s.append(rest_s)
    stack = lambda outs, idx: jnp.stack([o[idx] for o in outs])
    return (xp, xs,
            stack(outs_p, 0), stack(outs_p, 1), stack(outs_p, 2), stack(outs_p, 3),
            stack(outs_s, 0), stack(outs_s, 1), stack(outs_s, 2), stack(outs_s, 3))
```

```python
import functools
import math

import jax
import jax.numpy as jnp
from jax import lax
from jax.experimental import pallas as pl
from jax.experimental.pallas import tpu as pltpu

F32 = jnp.float32
BF16 = jnp.bfloat16

D_MODEL = 1024
CHUNK = 64
A_HEADS = 8
A_DIM = 64
A_VDIM = 2 * A_DIM
B_HEADS = 8
B_DK = 128
B_DV = 128
CONV_W = 4
DELTA_BLOCK = 64
D_FF = -(-8 * D_MODEL // (3 * 256)) * 256
EPS = 1e-6

A_QK = A_HEADS * 2 * A_DIM
A_V = A_HEADS * A_VDIM
B_QK = B_HEADS * B_DK
B_V = B_HEADS * B_DV
CONV_CH = 2 * B_QK + B_V

LANES = 128
SUBLANES = 8
MXU_DIM = 256
NEG_BIG = -1e30
_DONE = object()
BOUND_SLACK = 1.0 + 2.0 ** -6
DEN_RANGE = (2.0 ** -64, 2.0 ** 64)
Q_SCALE = (A_DIM ** -0.5) * math.log2(math.e)
VMEM_LIMIT = 56 * 1024 * 1024


def _cparams(n_axes):
    return pltpu.CompilerParams(dimension_semantics=("arbitrary",) * n_axes,
                                vmem_limit_bytes=VMEM_LIMIT)


def _dot(a, b):
    return jnp.dot(a, b, preferred_element_type=F32)


def _dot_nt(a, b):
    return lax.dot_general(a, b, (((1,), (1,)), ((), ())), preferred_element_type=F32)


def _split(a):
    hi = a.astype(BF16)
    lo = (a - hi.astype(F32)).astype(BF16)
    return hi, lo


def _rms_rows(x, w):
    ms = jnp.mean(x * x, axis=-1, keepdims=True)
    return x * lax.rsqrt(ms + EPS) * w


def _group_sumsq(z, gshift, weight=1.0):
    r = lax.broadcasted_iota(jnp.int32, (MXU_DIM, MXU_DIM), 0) >> gshift
    c = lax.broadcasted_iota(jnp.int32, (MXU_DIM, MXU_DIM), 1) >> gshift
    blk = jnp.where(r == c, weight, 0.0).astype(BF16)
    outs = []
    for s in range(z.shape[1] // MXU_DIM):
        zs = z[:, s * MXU_DIM:(s + 1) * MXU_DIM]
        outs.append(_dot((zs * zs).astype(BF16), blk))
    return jnp.concatenate(outs, axis=1)


def _sigmoid(x):
    return 0.5 * jnp.tanh(0.5 * x) + 0.5


def _silu(x):
    half = 0.5 * x
    return half * jnp.tanh(half) + half


def _inproj_attn_kernel(x_ref, ln_ref, w_ref, wkt_ref, qn_ref, kn_ref, knc_ref, q_ref, k_ref, v_ref,
                        *, k_transposed):
    h = _rms_rows(x_ref[...], ln_ref[...]).astype(BF16)
    zq = _dot(h, w_ref[:, 0:A_QK])
    if k_transposed:
        zk = _dot_nt(wkt_ref[...], h)
    else:
        zk = _dot(h, w_ref[:, A_QK:2 * A_QK])
    v_ref[...] = _dot(h, w_ref[:, 2 * A_QK:])

    n = zq * lax.rsqrt(_group_sumsq(zq, 6) * (1.0 / A_DIM) + EPS) * qn_ref[...]
    q_ref[...] = (n * Q_SCALE).astype(BF16)
    if k_transposed:
        z3 = zk.reshape(A_QK // A_DIM, A_DIM, zk.shape[1])
        ms = jnp.mean(z3 * z3, axis=1, keepdims=True)
        kn3 = knc_ref[...].reshape(A_QK // A_DIM, A_DIM, 1)
        k_ref[0] = (z3 * lax.rsqrt(ms + EPS) * kn3).reshape(zk.shape)
    else:
        k_ref[...] = zk * lax.rsqrt(_group_sumsq(zk, 6) * (1.0 / A_DIM) + EPS) * kn_ref[...]


def _inproj_attn(x2, ln_w, w_qkv, w_kt, qn_row, kn_row, kn_col, tm, seq_len, k_transposed):
    n = x2.shape[0]
    const = lambda i: (0, 0)
    if k_transposed:
        tiles_per_seq = seq_len // tm
        k_spec = pl.BlockSpec((1, A_QK, tm), lambda i: (i // tiles_per_seq, 0, i % tiles_per_seq))
        k_shape = jax.ShapeDtypeStruct((n // seq_len, A_QK, seq_len), F32)
    else:
        k_spec = pl.BlockSpec((tm, A_QK), lambda i: (i, 0))
        k_shape = jax.ShapeDtypeStruct((n, A_QK), F32)
    return pl.pallas_call(
        functools.partial(_inproj_attn_kernel, k_transposed=k_transposed),
        grid=(n // tm,),
        in_specs=[
            pl.BlockSpec((tm, D_MODEL), lambda i: (i, 0)),
            pl.BlockSpec((1, D_MODEL), const),
            pl.BlockSpec((D_MODEL, 2 * A_QK + A_V), const, pipeline_mode=pl.Buffered(1)),
            pl.BlockSpec((A_QK, D_MODEL), const, pipeline_mode=pl.Buffered(1)),
            pl.BlockSpec((1, A_QK), const),
            pl.BlockSpec((1, A_QK), const),
            pl.BlockSpec((A_QK, 1), const),
        ],
        out_specs=[
            pl.BlockSpec((tm, A_QK), lambda i: (i, 0)),
            k_spec,
            pl.BlockSpec((tm, A_V), lambda i: (i, 0)),
        ],
        out_shape=[
            jax.ShapeDtypeStruct((n, A_QK), BF16),
            k_shape,
            jax.ShapeDtypeStruct((n, A_V), F32),
        ],
        compiler_params=_cparams(1),
        name="inproj_attn",
    )(x2, ln_w, w_qkv, w_kt, qn_row, kn_row, kn_col)


def _inproj_delta_kernel(x_ref, ln_ref, w_ref, wbg_ref, cw_ref, cbuf_ref, alog_ref, dtb_ref,
                         dqkv_ref, bg_ref, cst_ref, carry, *, tiles_per_seq, seqs_per_tile):
    i = pl.program_id(0)
    tm = x_ref.shape[0]
    h = _rms_rows(x_ref[...], ln_ref[...]).astype(BF16)
    project = lambda j: _dot(h, w_ref[:, j * B_QK:(j + 1) * B_QK])
    zs = {0: project(0), 1: project(1)}
    zb = _dot(h, wbg_ref[...])

    lane = lax.broadcasted_iota(jnp.int32, zb.shape, 1)
    a = zb + dtb_ref[...]
    softplus = jnp.maximum(a, 0.0) + jnp.log1p(jnp.exp(-jnp.abs(a)))
    g = -jnp.exp(alog_ref[...]) * softplus
    bg_ref[...] = jnp.where(lane < B_HEADS, _sigmoid(zb), jnp.where(lane < 2 * B_HEADS, g, 0.0))

    first = (i % tiles_per_seq) == 0
    seg = tm // seqs_per_tile
    for j in range(3):
        zfull_tile = zs.pop(j)
        cols = slice(j * B_QK, (j + 1) * B_QK)
        cw = cw_ref[:, cols] * 0.5
        for sq in range(seqs_per_tile):
            rows = slice(sq * seg, (sq + 1) * seg)
            z = zfull_tile[rows, :]
            prev = jnp.where(first, cbuf_ref[sq, :, cols], carry[:, cols])
            tail = z[seg - SUBLANES:, :]
            if sq == seqs_per_tile - 1:
                carry[:, cols] = tail
            cst_ref[sq, :, cols] = tail
            z3 = jnp.concatenate([prev, z], axis=0).reshape(seg // SUBLANES + 1, SUBLANES, B_QK)
            sub = lax.broadcasted_iota(jnp.int32, (1, SUBLANES, 1), 1)
            y = z * cw[CONV_W - 1:CONV_W, :]
            for s in range(1, CONV_W):
                rot = pltpu.roll(z3, s, axis=1)
                shifted = jnp.where(sub < s, rot[:-1], rot[1:]).reshape(seg, B_QK)
                y = y + shifted * cw[CONV_W - 1 - s:CONV_W - s, :]
            y = y * jnp.tanh(y) + y
            if j == 0:
                y = y * lax.rsqrt(_group_sumsq(y, 7, float(B_DK)) + EPS * B_DK)
            elif j == 1:
                y = y * lax.rsqrt(_group_sumsq(y, 7) + EPS)
            dqkv_ref[rows, cols] = y
        if j == 0:
            zs[2] = project(2)


def _inproj_delta(x2, ln_w, w_d, w_bg, conv_w, cbuf8, alog_row, dtb_row, tm, seq_len):
    n = x2.shape[0]
    tiles_per_seq = max(1, seq_len // tm)
    spt = max(1, tm // seq_len)
    n_tiles = n // tm
    const = lambda i: (0, 0)
    kern = functools.partial(_inproj_delta_kernel, tiles_per_seq=tiles_per_seq, seqs_per_tile=spt)
    return pl.pallas_call(
        kern,
        grid=(n_tiles,),
        in_specs=[
            pl.BlockSpec((tm, D_MODEL), lambda i: (i, 0)),
            pl.BlockSpec((1, D_MODEL), const),
            pl.BlockSpec((D_MODEL, CONV_CH), const, pipeline_mode=pl.Buffered(1)),
            pl.BlockSpec((D_MODEL, LANES), const, pipeline_mode=pl.Buffered(1)),
            pl.BlockSpec((CONV_W, CONV_CH), const),
            pl.BlockSpec((spt, SUBLANES, CONV_CH), lambda i: (i // tiles_per_seq, 0, 0)),
            pl.BlockSpec((1, LANES), const),
            pl.BlockSpec((1, LANES), const),
        ],
        out_specs=[
            pl.BlockSpec((tm, CONV_CH), lambda i: (i, 0)),
            pl.BlockSpec((tm, LANES), lambda i: (i, 0)),
            pl.BlockSpec((spt, SUBLANES, CONV_CH), lambda i: (i, 0, 0)),
        ],
        out_shape=[
            jax.ShapeDtypeStruct((n, CONV_CH), F32),
            jax.ShapeDtypeStruct((n, LANES), F32),
            jax.ShapeDtypeStruct((n_tiles * spt, SUBLANES, CONV_CH), F32),
        ],
        scratch_shapes=[pltpu.VMEM((SUBLANES, CONV_CH), F32)],
        compiler_params=_cparams(1),
        name="inproj_delta",
    )(x2, ln_w, w_d, w_bg, conv_w, cbuf8, alog_row, dtb_row)


def _lambda_value(lq1, lk1, lq2, lk2, lam_init):
    d1 = jnp.sum(lq1[...] * lk1[...], axis=-1, keepdims=True)
    d2 = jnp.sum(lq2[...] * lk2[...], axis=-1, keepdims=True)
    return jnp.exp(d1) - jnp.exp(d2) + lam_init


def _split_maps(q):
    lane = lax.broadcasted_iota(jnp.int32, q.shape, 1)
    zero = jnp.zeros_like(q)
    return jnp.where(lane < A_DIM, q, zero), jnp.where(lane >= A_DIM, q, zero)


def _attn_finish(o, sw, lam_init):
    ms = jnp.mean(o * o, axis=-1, keepdims=True)
    return (o * lax.rsqrt(ms + EPS) * sw * (1.0 - lam_init)).astype(BF16)


def _rows_to_sublanes(x, op):
    return functools.reduce(op, [x[i * SUBLANES:(i + 1) * SUBLANES] for i in range(x.shape[0] // SUBLANES)])


def _attn_prompt_kernel(q_ref, k_ref, v_ref, lq1, lk1, lq2, lk2, swc_ref, o_ref,
                        kb_scr, vt_scr, s_scr, p_scr, *, tq, lam_init):
    seq = q_ref.shape[1]
    lam = _lambda_value(lq1, lk1, lq2, lk2, lam_init)
    kb_scr[...] = k_ref[0].T.astype(BF16)
    vt_scr[0:A_VDIM, :] = v_ref[0].T.astype(BF16)
    vt_scr[A_VDIM:, :] = jnp.ones((vt_scr.shape[0] - A_VDIM, seq), BF16)
    ri = lax.broadcasted_iota(jnp.int32, (tq, 2 * tq), 0)
    ci = lax.broadcasted_iota(jnp.int32, (tq, 2 * tq), 1)
    diag_mask = (ri // CHUNK) <= ((ci & (tq - 1)) // CHUNK)
    n_q = seq // tq
    n_slots = s_scr.shape[0]

    def stacked_queries(qi):
        q2 = jnp.concatenate(_split_maps(q_ref[0, qi * tq:(qi + 1) * tq, :]), axis=0)
        return q2.astype(F32).T

    kt = k_ref[0]
    k_sq = kt * kt
    kmax = [jnp.max(jnp.sum(k_sq[m * A_DIM:(m + 1) * A_DIM], axis=0, keepdims=True),
                    axis=1, keepdims=True) for m in range(2)]
    kmax_row = jnp.where(lax.broadcasted_iota(jnp.int32, (1, 2 * tq), 1) < tq, kmax[0], kmax[1])

    def bounded_phase(qi):
        qf = stacked_queries(qi)
        bound = jnp.sqrt(jnp.sum(qf * qf, axis=0, keepdims=True) * kmax_row) * BOUND_SLACK + 1.0
        q2t = qf.astype(BF16)
        for j in range(qi + 1):
            p = jnp.exp2(_dot(kb_scr[j * tq:(j + 1) * tq, :], q2t) - bound)
            if j == qi:
                p = jnp.where(diag_mask, p, 0.0)
            p_scr[qi % n_slots, j * tq:(j + 1) * tq, :] = p.astype(BF16)

    def score_phase(qi):
        q2t = stacked_queries(qi).astype(BF16)
        m8 = None
        for j in range(qi + 1):
            st = _dot(kb_scr[j * tq:(j + 1) * tq, :], q2t)
            if j == qi:
                st = jnp.where(diag_mask, st, NEG_BIG)
            s_scr[qi % n_slots, j * tq:(j + 1) * tq, :] = st
            part = _rows_to_sublanes(st, jnp.maximum)
            m8 = part if m8 is None else jnp.maximum(m8, part)
        return jnp.max(m8, axis=0, keepdims=True)

    def exp_phase(qi, m_row):
        for j in range(qi + 1):
            p = jnp.exp2(s_scr[qi % n_slots, j * tq:(j + 1) * tq, :] - m_row)
            p_scr[qi % n_slots, j * tq:(j + 1) * tq, :] = p.astype(BF16)

    def value_phase(qi):
        n_keys = (qi + 1) * tq
        acc_t = _dot(vt_scr[:, :n_keys], p_scr[qi % n_slots, :n_keys, :])
        o2t = acc_t[:A_VDIM] / acc_t[A_VDIM:A_VDIM + 1]
        ot = o2t[:, :tq] - lam * o2t[:, tq:]
        ms = jnp.mean(ot * ot, axis=0, keepdims=True)
        ot = ot * lax.rsqrt(ms + EPS) * swc_ref[...] * (1.0 - lam_init)
        o_ref[0, qi * tq:(qi + 1) * tq, :] = ot.T.astype(BF16)
        return acc_t[A_VDIM:A_VDIM + 1]

    bounded_phase(0)
    den_lo = den_hi = None
    for qi in range(n_q):
        if qi + 1 < n_q:
            bounded_phase(qi + 1)
        den = value_phase(qi)
        den_lo = den if den_lo is None else jnp.minimum(den_lo, den)
        den_hi = den if den_hi is None else jnp.maximum(den_hi, den)
    trusted = jnp.logical_and(jnp.min(den_lo) >= DEN_RANGE[0], jnp.max(den_hi) <= DEN_RANGE[1])

    @pl.when(jnp.logical_not(trusted))
    def _():
        m_row = score_phase(0)
        for qi in range(n_q):
            exp_phase(qi, m_row)
            if qi + 1 < n_q:
                m_row = score_phase(qi + 1)
            value_phase(qi)


def _attn_prompt(q3, kt3, v3, lams, sw_col, lam_init, tq):
    b, l, _ = q3.shape
    vec = lambda bi, h: (0, 0)
    head = pl.BlockSpec((1, l, LANES), lambda bi, h: (bi, 0, h))
    kern = functools.partial(_attn_prompt_kernel, tq=tq, lam_init=lam_init)
    return pl.pallas_call(
        kern,
        grid=(b, A_HEADS),
        in_specs=[
            head, pl.BlockSpec((1, 2 * A_DIM, l), lambda bi, h: (bi, h, 0)), head,
            pl.BlockSpec((1, A_DIM), vec), pl.BlockSpec((1, A_DIM), vec),
            pl.BlockSpec((1, A_DIM), vec), pl.BlockSpec((1, A_DIM), vec),
            pl.BlockSpec((A_VDIM, 1), vec),
        ],
        out_specs=head,
        out_shape=jax.ShapeDtypeStruct((b, l, A_V), BF16),
        scratch_shapes=[
            pltpu.VMEM((l, 2 * A_DIM), BF16), pltpu.VMEM((A_VDIM + 2 * SUBLANES, l), BF16),
            pltpu.VMEM((2, l, 2 * tq), F32), pltpu.VMEM((2, l, 2 * tq), BF16),
        ],
        compiler_params=_cparams(2),
        name="attn_prompt",
    )(q3, kt3, v3, *lams, sw_col)


def _attn_sample_kernel(q_ref, kc_ref, vc_ref, kn_ref, vn_ref, lq1, lk1, lq2, lk2, sw_ref, o_ref,
                        *, lam_init, hpb):
    l = q_ref.shape[1]
    past = kc_ref.shape[3]
    lam = _lambda_value(lq1, lk1, lq2, lk2, lam_init)
    for hh in range(hpb):
        h = pl.program_id(1) * hpb + hh
        cols = slice(hh * LANES, (hh + 1) * LANES)
        q2 = jnp.concatenate(_split_maps(q_ref[0, :, cols]), axis=0)
        vc = vc_ref[0, pl.ds(h, past, stride=A_HEADS), :].astype(BF16)
        sc = _dot(q2, kc_ref[0, hh].astype(BF16))
        sn = _dot_nt(q2, kn_ref[0, :, cols].astype(BF16))
        mx = jnp.maximum(jnp.max(sc, axis=-1, keepdims=True), jnp.max(sn, axis=-1, keepdims=True))
        pc = jnp.exp2(sc - mx)
        pn = jnp.exp2(sn - mx)
        den = jnp.sum(pc, axis=-1, keepdims=True) + jnp.sum(pn, axis=-1, keepdims=True)
        o2 = (_dot(pc.astype(BF16), vc)
              + _dot(pn.astype(BF16), vn_ref[0, :, cols].astype(BF16))) / den
        o_ref[0, :, cols] = _attn_finish(o2[:l] - lam * o2[l:], sw_ref[...], lam_init)


def _attn_sample(q3, kct, vc2, kn3, vn3, lams, sw_row, lam_init, hpb=2):
    b, l, _ = q3.shape
    past = kct.shape[3]
    vec = lambda bi, h: (0, 0)
    blk = lambda rows: pl.BlockSpec((1, rows, hpb * LANES), lambda bi, h: (bi, 0, h))
    kern = functools.partial(_attn_sample_kernel, lam_init=lam_init, hpb=hpb)
    return pl.pallas_call(
        kern,
        grid=(b, A_HEADS // hpb),
        in_specs=[
            blk(l),
            pl.BlockSpec((1, hpb, 2 * A_DIM, past), lambda bi, h: (bi, h, 0, 0)),
            pl.BlockSpec((1, past * A_HEADS, A_VDIM), lambda bi, h: (bi, 0, 0)),
            blk(l), blk(l),
            pl.BlockSpec((1, A_DIM), vec), pl.BlockSpec((1, A_DIM), vec),
            pl.BlockSpec((1, A_DIM), vec), pl.BlockSpec((1, A_DIM), vec),
            pl.BlockSpec((1, A_VDIM), vec),
        ],
        out_specs=blk(l),
        out_shape=jax.ShapeDtypeStruct((b, l, A_V), BF16),
        compiler_params=_cparams(2),
        name="attn_sample",
    )(q3, kct, vc2, kn3, vn3, *lams, sw_row)


def _delta_kernel(q_ref, k_ref, v_ref, bg_ref, s0_ref, nw_ref, o_ref, sout_ref, s_scr,
                  *, c, nblk, hb, gb):
    t = pl.program_id(2)

    @pl.when(t == 0)
    def _():
        s_scr[...] = s0_ref[0]

    bg = bg_ref[0]
    lane = lax.broadcasted_iota(jnp.int32, bg.shape, 1)
    bg_t = bg.T
    sub = lax.broadcasted_iota(jnp.int32, bg_t.shape, 0)
    ri = lax.broadcasted_iota(jnp.int32, (c, c), 0)
    ci = lax.broadcasted_iota(jnp.int32, (c, c), 1)
    causal = ri >= ci
    strict = ri > ci
    ltri = causal.astype(BF16)
    utri = (ri <= ci).astype(BF16)
    ltri2 = jnp.concatenate([ltri, ltri], axis=1)
    utri2 = jnp.concatenate([utri, utri], axis=0)
    eye = jnp.where(ri == ci, 1.0, 0.0)

    gates = []
    for hh in range(hb):
        h = pl.program_id(1) * hb + hh
        gates.append((
            jnp.sum(jnp.where(lane == h, bg, 0.0), axis=-1, keepdims=True),
            jnp.sum(jnp.where(lane == h + B_HEADS, bg, 0.0), axis=-1, keepdims=True),
            jnp.sum(jnp.where(sub == h + B_HEADS, bg_t, 0.0), axis=0, keepdims=True)))
    pre = {}
    states = [s_scr[hh] for hh in range(hb)]

    def prepare(blocks):
        items = [(hh, blk) for hh in range(hb) for blk in blocks]
        for hh, blk in items:
            rows = slice(blk * c, (blk + 1) * c)
            cols = slice(hh * LANES, (hh + 1) * LANES)
            bc = gates[hh][0][rows, :]
            q = q_ref[0, rows, cols]
            k = k_ref[0, rows, cols]
            v = v_ref[0, rows, cols]
            ghi, glo = _split(jnp.broadcast_to(gates[hh][1][rows, :], (c, LANES)))
            g_cum = _dot(ltri2, jnp.concatenate([ghi, glo], axis=0))
            rhi, rlo = _split(gates[hh][2][:, rows])
            g_rows = jnp.broadcast_to(jnp.concatenate([rhi, rlo], axis=1), (c, 2 * c))
            g_cum_row = _dot(g_rows, utri2)
            diff = jnp.where(causal, g_cum[:, :c] - g_cum_row, 0.0)
            decay = jnp.where(causal, jnp.exp(diff), 0.0)
            kb = k.astype(BF16)
            kq = _dot_nt(jnp.concatenate([kb, q.astype(BF16)], axis=0), kb)
            m = jnp.where(strict, bc * kq[:c] * decay, 0.0)
            pre[hh, blk] = dict(q=q, k=k, v=v, bc=bc, g_cum=g_cum, decay=decay, qk=kq[c:])
            pre[hh, blk]["pq"] = (eye - m, -m)
        yield
        for it in items:
            pm, qm = pre[it]["pq"]
            qb = qm.astype(BF16)
            pre[it]["pq"] = (pm, _dot(qb, qb))
        yield
        for _ in range(int(math.log2(c)) - 2):
            for it in items:
                pm, qm = pre[it]["pq"]
                qb = qm.astype(BF16)
                both = _dot(jnp.concatenate([pm.astype(BF16), qb], axis=0), qb)
                pre[it]["pq"] = (pm + both[:c], both[c:])
            yield
        for it in items:
            pm, qm = pre[it]["pq"]
            pre[it]["pq"] = (pm + _dot(pm.astype(BF16), qm.astype(BF16)), qm)
        yield
        for it in items:
            d = pre[it]
            e_g = jnp.exp(d["g_cum"])
            rhs = jnp.concatenate([d["v"] * d["bc"], d["k"] * (d["bc"] * e_g)], axis=1)
            sol = _dot(d.pop("pq")[0].astype(BF16), rhs.astype(BF16))
            g_last = d["g_cum"][c - 1:c, :]
            a_loc = jnp.where(causal, d["qk"] * d["decay"], 0.0).astype(BF16)
            k_tail_t = (d["k"] * jnp.exp(g_last - d["g_cum"])).T.astype(BF16)
            d.update(
                u=sol[:, :B_DV],
                state_lhs=jnp.concatenate(
                    [sol[:, B_DV:].astype(BF16), (d["q"] * e_g).astype(BF16)], axis=0),
                value_lhs=jnp.concatenate([a_loc, k_tail_t], axis=0),
                g_tail=jnp.exp(g_last))
        yield

    def scan(blocks):
        for blk in blocks:
            vbs, o_state = [], []
            for hh in range(hb):
                d = pre[hh, blk]
                ws = _dot(d["state_lhs"], states[hh].astype(BF16))
                vbs.append((d["u"] - ws[:c]).astype(BF16))
                o_state.append(ws[c:])
            yield
            for hh in range(hb):
                d = pre.pop((hh, blk))
                av = _dot(d["value_lhs"], vbs[hh])
                o = o_state[hh] + av[:c]
                states[hh] = states[hh] * d["g_tail"] + av[c:]
                o_ref[0, blk * c:(blk + 1) * c, hh * LANES:(hh + 1) * LANES] = (
                    _rms_rows(o, nw_ref[...]).astype(BF16))
            yield

    def run(*stage_iters):
        live = list(stage_iters)
        while live:
            live = [it for it in live if next(it, _DONE) is not _DONE]

    groups = [list(range(g, min(g + gb, nblk))) for g in range(0, nblk, gb)]
    run(prepare(groups[0]))
    for g in range(1, len(groups)):
        run(prepare(groups[g]), scan(groups[g - 1]))
    run(scan(groups[-1]))
    for hh in range(hb):
        s_scr[hh] = states[hh]

    @pl.when(t == pl.num_programs(2) - 1)
    def _():
        for hh in range(hb):
            sout_ref[0, hh] = states[hh]


def _delta(dqkv3, bg3, s0, nw_row, c, tl, hb, gb):
    b, l, _ = dqkv3.shape
    n_hg = B_HEADS // hb
    kern = functools.partial(_delta_kernel, c=c, nblk=tl // c, hb=hb, gb=gb)
    lane_blk = lambda off: pl.BlockSpec((1, tl, hb * LANES), lambda bi, h, t: (bi, t, h + off))
    state = pl.BlockSpec((1, hb, B_DK, B_DV), lambda bi, h, t: (bi, h, 0, 0))
    return pl.pallas_call(
        kern,
        grid=(b, n_hg, l // tl),
        in_specs=[
            lane_blk(0), lane_blk(n_hg), lane_blk(2 * n_hg),
            pl.BlockSpec((1, tl, LANES), lambda bi, h, t: (bi, t, 0)),
            state,
            pl.BlockSpec((1, B_DV), lambda bi, h, t: (0, 0)),
        ],
        out_specs=[lane_blk(0), state],
        out_shape=[
            jax.ShapeDtypeStruct((b, l, B_V), BF16),
            jax.ShapeDtypeStruct((b, B_HEADS, B_DK, B_DV), F32),
        ],
        scratch_shapes=[pltpu.VMEM((hb, B_DK, B_DV), F32)],
        compiler_params=_cparams(3),
        name="delta_rule",
    )(dqkv3, dqkv3, dqkv3, bg3, s0, nw_row)


def _merge_kernel(x_ref, oa_ref, ob_ref, ln_ref, wdg, wga, wgb, wba, wbb, wout, y_ref):
    x = x_ref[...]
    h = _rms_rows(x, ln_ref[...]).astype(BF16)
    ob = ob_ref[...].astype(F32) * _silu(_dot(h, wdg[...]))
    branch_b = _dot(ob.astype(BF16), wbb[...])
    branch_a = _dot(oa_ref[...], wba[...])
    merged = (_sigmoid(_dot(h, wga[...])) * branch_a
              + _sigmoid(_dot(h, wgb[...])) * branch_b)
    y_ref[...] = x + _dot(merged.astype(BF16), wout[...])


def _merge(x2, oa2, ob2, ln_w, wdg, wga, wgb, wba, wbb, wout, tm):
    n = x2.shape[0]
    tile = pl.BlockSpec((tm, D_MODEL), lambda i: (i, 0))
    wspec = pl.BlockSpec((D_MODEL, D_MODEL), lambda i: (0, 0), pipeline_mode=pl.Buffered(1))
    return pl.pallas_call(
        _merge_kernel,
        grid=(n // tm,),
        in_specs=[tile, tile, tile, pl.BlockSpec((1, D_MODEL), lambda i: (0, 0))] + [wspec] * 6,
        out_specs=tile,
        out_shape=jax.ShapeDtypeStruct((n, D_MODEL), F32),
        compiler_params=_cparams(1),
        name="merge_outproj",
    )(x2, oa2, ob2, ln_w, wdg, wga, wgb, wba, wbb, wout)


def _ffn_kernel(x_ref, ln_ref, wgu_ref, wd_ref, y_ref, *, n_chunks):
    x = x_ref[...]
    h = _rms_rows(x, ln_ref[...]).astype(BF16)
    n_tiles = D_FF // MXU_DIM
    edges = [MXU_DIM * ((n_tiles * c + n_chunks - 1) // n_chunks) for c in range(n_chunks + 1)]
    acts = []
    for lo, hi in zip(edges[:-1], edges[1:]):
        gate = _dot(h, wgu_ref[:, lo:hi])
        up = _dot(h, wgu_ref[:, D_FF + lo:D_FF + hi])
        acts.append((_silu(gate) * up).astype(BF16))
    y = x
    for (lo, hi), act in zip(zip(edges[:-1], edges[1:]), acts):
        y = y + _dot(act, wd_ref[lo:hi, :])
    y_ref[...] = y


def _ffn(x2, ln_w, w_gate_up, w_down, tm, n_chunks):
    n = x2.shape[0]
    const = lambda i: (0, 0)
    return pl.pallas_call(
        functools.partial(_ffn_kernel, n_chunks=n_chunks),
        grid=(n // tm,),
        in_specs=[
            pl.BlockSpec((tm, D_MODEL), lambda i: (i, 0)),
            pl.BlockSpec((1, D_MODEL), const),
            pl.BlockSpec((D_MODEL, 2 * D_FF), const, pipeline_mode=pl.Buffered(1)),
            pl.BlockSpec((D_FF, D_MODEL), const, pipeline_mode=pl.Buffered(1)),
        ],
        out_specs=pl.BlockSpec((tm, D_MODEL), lambda i: (i, 0)),
        out_shape=jax.ShapeDtypeStruct((n, D_MODEL), F32),
        compiler_params=_cparams(1),
        name="swiglu_ffn",
    )(x2, ln_w, w_gate_up, w_down)


def _pick_tile(n, target):
    t = min(n, target)
    while n % t:
        t //= 2
    return t


def _prep_weights(ln1_w, w_in, q_norm_w, k_norm_w, lambda_q1, lambda_k1, lambda_q2, lambda_k2,
                  subln_w, w_branch_a, conv_w, a_log, dt_bias, delta_norm_w, w_branch_b,
                  w_out, ln2_w, w_gate_up, w_down):
    o_d = 2 * A_QK + A_V
    o_dg = o_d + CONV_CH
    o_bg = o_dg + B_V
    o_ga = o_bg + 2 * B_HEADS
    o_gb = o_ga + D_MODEL
    pad_lanes = lambda v: jnp.pad(v.astype(F32), (B_HEADS, LANES - 2 * B_HEADS))[None, :]
    return dict(
        ln1=ln1_w[None, :], ln2=ln2_w[None, :],
        w_qkv=w_in[:, :o_d].astype(BF16),
        w_d=w_in[:, o_d:o_dg].astype(BF16),
        w_dg=w_in[:, o_dg:o_bg].astype(BF16),
        w_bg=jnp.pad(w_in[:, o_bg:o_ga], ((0, 0), (0, LANES - 2 * B_HEADS))).astype(BF16),
        w_ga=w_in[:, o_ga:o_gb].astype(BF16),
        w_gb=w_in[:, o_gb:].astype(BF16),
        qn=jnp.tile(q_norm_w, A_QK // A_DIM)[None, :],
        kn=jnp.tile(k_norm_w, A_QK // A_DIM)[None, :],
        kn_col=jnp.tile(k_norm_w, A_QK // A_DIM)[:, None],
        w_kt=w_in[:, A_QK:2 * A_QK].T.astype(BF16),
        lams=tuple(v[None, :] for v in (lambda_q1, lambda_k1, lambda_q2, lambda_k2)),
        subln=subln_w[None, :], subln_col=subln_w[:, None],
        w_ba=w_branch_a.astype(BF16), w_bb=w_branch_b.astype(BF16),
        conv_w=conv_w, alog=pad_lanes(a_log), dtb=pad_lanes(dt_bias),
        dnorm=delta_norm_w[None, :],
        w_out=w_out.astype(BF16),
        w_gate_up=w_gate_up.astype(BF16), w_down=w_down.astype(BF16),
    )


def _layer(x, past_k, past_v, s0, conv_buf, lam_init, p):
    b, l, _ = x.shape
    n = b * l
    x2 = x.reshape(n, D_MODEL)

    prompt = past_k is None
    tm_a = _pick_tile(l, 1024) if prompt else _pick_tile(n, 1024)
    q2, k2, v2 = _inproj_attn(x2, p["ln1"], p["w_qkv"], p["w_kt"], p["qn"], p["kn"], p["kn_col"],
                              tm_a, l, prompt)

    tm_d = _pick_tile(l, 1024) if l >= 512 else l * _pick_tile(b, max(1, 512 // l))
    cbuf8 = jnp.pad(conv_buf, ((0, 0), (SUBLANES - (CONV_W - 1), 0), (0, 0)))
    dqkv, bg, cst = _inproj_delta(x2, p["ln1"], p["w_d"], p["w_bg"], p["conv_w"], cbuf8,
                                  p["alog"], p["dtb"], tm_d, l)
    tiles_per_seq = max(1, l // tm_d)
    new_conv = cst[tiles_per_seq - 1::tiles_per_seq, SUBLANES - (CONV_W - 1):, :]

    q3 = q2.reshape(b, l, A_QK)
    v3 = v2.reshape(b, l, A_V)
    if prompt:
        oa = _attn_prompt(q3, k2, v3, p["lams"], p["subln_col"], lam_init, _pick_tile(l, 128))
        k_out = jnp.transpose(k2.reshape(b, A_HEADS, 2, A_DIM, l), (0, 4, 1, 2, 3))
    else:
        past = past_k.shape[1]
        k3 = k2.reshape(b, l, A_QK)
        kct = jnp.transpose(past_k, (0, 2, 3, 4, 1)).reshape(b, A_HEADS, 2 * A_DIM, past)
        oa = _attn_sample(q3, kct, past_v.reshape(b, past * A_HEADS, A_VDIM), k3, v3, p["lams"],
                          p["subln"], lam_init)
        k_out = k3.reshape(b, l, A_HEADS, 2, A_DIM)

    c = DELTA_BLOCK if l % DELTA_BLOCK == 0 else l
    tl = c * max(1, min(8, l // c))
    ob, s_new = _delta(dqkv.reshape(b, l, CONV_CH), bg.reshape(b, l, LANES), s0, p["dnorm"], c, tl,
                       B_HEADS, 2)

    x1 = _merge(x2, oa.reshape(n, A_V), ob.reshape(n, B_V), p["ln1"], p["w_dg"], p["w_ga"],
                p["w_gb"], p["w_ba"], p["w_bb"], p["w_out"], _pick_tile(n, 512))
    y = _ffn(x1, p["ln2"], p["w_gate_up"], p["w_down"], _pick_tile(n, 512), 2)
    return (y.reshape(b, l, D_MODEL), k_out, v3.reshape(b, l, A_HEADS, A_VDIM), s_new, new_conv)


def kernel(x_prompt, x_sample, cache_k, cache_v, state_delta, state_conv, ln1_w, w_in, q_norm_w,
           k_norm_w, lambda_q1, lambda_k1, lambda_q2, lambda_k2, subln_w, w_branch_a, conv_w,
           a_log, dt_bias, delta_norm_w, w_branch_b, w_out, ln2_w, w_gate_up, w_down):
    depth = ln1_w.shape[0]
    xp, xs = x_prompt, x_sample
    outs_p, outs_s = [], []
    for layer in range(depth):
        lam_init = 0.8 - 0.6 * math.exp(-0.3 * layer)
        p = _prep_weights(*(w[layer] for w in (
            ln1_w, w_in, q_norm_w, k_norm_w, lambda_q1, lambda_k1, lambda_q2, lambda_k2, subln_w,
            w_branch_a, conv_w, a_log, dt_bias, delta_norm_w, w_branch_b, w_out, ln2_w,
            w_gate_up, w_down)))
        bp = xp.shape[0]
        zero_conv = jnp.zeros((bp, CONV_W - 1, CONV_CH), xp.dtype)
        zero_s = jnp.zeros((bp, B_HEADS, B_DK, B_DV), F32)
        xp, *rest_p = _layer(xp, None, None, zero_s, zero_conv, lam_init, p)
        xs, *rest_s = _layer(xs, cache_k[layer], cache_v[layer], state_delta[layer],
                             state_conv[layer], lam_init, p)
        outs_p.append(rest_p)
        outs_s.append(rest_s)
    stack = lambda outs, idx: jnp.stack([o[idx] for o in outs])
    return (xp, xs,
            stack(outs_p, 0), stack(outs_p, 1), stack(outs_p, 2), stack(outs_p, 3),
            stack(outs_s, 0), stack(outs_s, 1), stack(outs_s, 2), stack(outs_s, 3))
```

```python
import functools
import math

import jax
import jax.numpy as jnp
from jax import lax
from jax.experimental import pallas as pl
from jax.experimental.pallas import tpu as pltpu

F32 = jnp.float32
BF16 = jnp.bfloat16

D_MODEL = 1024
CHUNK = 64
A_HEADS = 8
A_DIM = 64
A_VDIM = 2 * A_DIM
B_HEADS = 8
B_DK = 128
B_DV = 128
CONV_W = 4
DELTA_BLOCK = 64
D_FF = -(-8 * D_MODEL // (3 * 256)) * 256
EPS = 1e-6

A_QK = A_HEADS * 2 * A_DIM
A_V = A_HEADS * A_VDIM
B_QK = B_HEADS * B_DK
B_V = B_HEADS * B_DV
CONV_CH = 2 * B_QK + B_V

LANES = 128
SUBLANES = 8
MXU_DIM = 256
NEG_BIG = -1e30
_DONE = object()
BOUND_SLACK = 1.0 + 2.0 ** -6
DEN_RANGE = (2.0 ** -64, 2.0 ** 64)
Q_SCALE = (A_DIM ** -0.5) * math.log2(math.e)
VMEM_LIMIT = 56 * 1024 * 1024


def _cparams(n_axes):
    return pltpu.CompilerParams(dimension_semantics=("arbitrary",) * n_axes,
                                vmem_limit_bytes=VMEM_LIMIT)


def _dot(a, b):
    return jnp.dot(a, b, preferred_element_type=F32)


def _dot_nt(a, b):
    return lax.dot_general(a, b, (((1,), (1,)), ((), ())), preferred_element_type=F32)


def _split(a):
    hi = a.astype(BF16)
    lo = (a - hi.astype(F32)).astype(BF16)
    return hi, lo


def _rms_rows(x, w):
    ms = jnp.mean(x * x, axis=-1, keepdims=True)
    return x * lax.rsqrt(ms + EPS) * w


def _group_sumsq(z, gshift, weight=1.0):
    r = lax.broadcasted_iota(jnp.int32, (MXU_DIM, MXU_DIM), 0) >> gshift
    c = lax.broadcasted_iota(jnp.int32, (MXU_DIM, MXU_DIM), 1) >> gshift
    blk = jnp.where(r == c, weight, 0.0).astype(BF16)
    outs = []
    for s in range(z.shape[1] // MXU_DIM):
        zs = z[:, s * MXU_DIM:(s + 1) * MXU_DIM]
        outs.append(_dot((zs * zs).astype(BF16), blk))
    return jnp.concatenate(outs, axis=1)


def _sigmoid(x):
    return 0.5 * jnp.tanh(0.5 * x) + 0.5


def _silu(x):
    half = 0.5 * x
    return half * jnp.tanh(half) + half


def _inproj_attn_kernel(x_ref, ln_ref, w_ref, wkt_ref, qn_ref, kn_ref, knc_ref, q_ref, k_ref, v_ref,
                        *, k_transposed):
    h = _rms_rows(x_ref[...], ln_ref[...]).astype(BF16)
    zq = _dot(h, w_ref[:, 0:A_QK])
    if k_transposed:
        zk = _dot_nt(wkt_ref[...], h)
    else:
        zk = _dot(h, w_ref[:, A_QK:2 * A_QK])
    v_ref[...] = _dot(h, w_ref[:, 2 * A_QK:])

    n = zq * lax.rsqrt(_group_sumsq(zq, 6) * (1.0 / A_DIM) + EPS) * qn_ref[...]
    q_ref[...] = (n * Q_SCALE).astype(BF16)
    if k_transposed:
        z3 = zk.reshape(A_QK // A_DIM, A_DIM, zk.shape[1])
        ms = jnp.mean(z3 * z3, axis=1, keepdims=True)
        kn3 = knc_ref[...].reshape(A_QK // A_DIM, A_DIM, 1)
        k_ref[0] = (z3 * lax.rsqrt(ms + EPS) * kn3).reshape(zk.shape)
    else:
        k_ref[...] = zk * lax.rsqrt(_group_sumsq(zk, 6) * (1.0 / A_DIM) + EPS) * kn_ref[...]


def _inproj_attn(x2, ln_w, w_qkv, w_kt, qn_row, kn_row, kn_col, tm, seq_len, k_transposed):
    n = x2.shape[0]
    const = lambda i: (0, 0)
    if k_transposed:
        tiles_per_seq = seq_len // tm
        k_spec = pl.BlockSpec((1, A_QK, tm), lambda i: (i // tiles_per_seq, 0, i % tiles_per_seq))
        k_shape = jax.ShapeDtypeStruct((n // seq_len, A_QK, seq_len), F32)
    else:
        k_spec = pl.BlockSpec((tm, A_QK), lambda i: (i, 0))
        k_shape = jax.ShapeDtypeStruct((n, A_QK), F32)
    return pl.pallas_call(
        functools.partial(_inproj_attn_kernel, k_transposed=k_transposed),
        grid=(n // tm,),
        in_specs=[
            pl.BlockSpec((tm, D_MODEL), lambda i: (i, 0)),
            pl.BlockSpec((1, D_MODEL), const),
            pl.BlockSpec((D_MODEL, 2 * A_QK + A_V), const, pipeline_mode=pl.Buffered(1)),
            pl.BlockSpec((A_QK, D_MODEL), const, pipeline_mode=pl.Buffered(1)),
            pl.BlockSpec((1, A_QK), const),
            pl.BlockSpec((1, A_QK), const),
            pl.BlockSpec((A_QK, 1), const),
        ],
        out_specs=[
            pl.BlockSpec((tm, A_QK), lambda i: (i, 0)),
            k_spec,
            pl.BlockSpec((tm, A_V), lambda i: (i, 0)),
        ],
        out_shape=[
            jax.ShapeDtypeStruct((n, A_QK), BF16),
            k_shape,
            jax.ShapeDtypeStruct((n, A_V), F32),
        ],
        compiler_params=_cparams(1),
        name="inproj_attn",
    )(x2, ln_w, w_qkv, w_kt, qn_row, kn_row, kn_col)


def _inproj_delta_kernel(x_ref, ln_ref, w_ref, wbg_ref, cw_ref, cbuf_ref, alog_ref, dtb_ref,
                         dqkv_ref, bg_ref, cst_ref, carry, *, tiles_per_seq, seqs_per_tile):
    i = pl.program_id(0)
    tm = x_ref.shape[0]
    h = _rms_rows(x_ref[...], ln_ref[...]).astype(BF16)
    project = lambda j: _dot(h, w_ref[:, j * B_QK:(j + 1) * B_QK])
    zs = {0: project(0), 1: project(1)}
    zb = _dot(h, wbg_ref[...])

    lane = lax.broadcasted_iota(jnp.int32, zb.shape, 1)
    a = zb + dtb_ref[...]
    softplus = jnp.maximum(a, 0.0) + jnp.log1p(jnp.exp(-jnp.abs(a)))
    g = -jnp.exp(alog_ref[...]) * softplus
    bg_ref[...] = jnp.where(lane < B_HEADS, _sigmoid(zb), jnp.where(lane < 2 * B_HEADS, g, 0.0))

    first = (i % tiles_per_seq) == 0
    seg = tm // seqs_per_tile
    for j in range(3):
        zfull_tile = zs.pop(j)
        cols = slice(j * B_QK, (j + 1) * B_QK)
        cw = cw_ref[:, cols] * 0.5
        for sq in range(seqs_per_tile):
            rows = slice(sq * seg, (sq + 1) * seg)
            z = zfull_tile[rows, :]
            prev = jnp.where(first, cbuf_ref[sq, :, cols], carry[:, cols])
            tail = z[seg - SUBLANES:, :]
            if sq == seqs_per_tile - 1:
                carry[:, cols] = tail
            cst_ref[sq, :, cols] = tail
            z3 = jnp.concatenate([prev, z], axis=0).reshape(seg // SUBLANES + 1, SUBLANES, B_QK)
            sub = lax.broadcasted_iota(jnp.int32, (1, SUBLANES, 1), 1)
            y = z * cw[CONV_W - 1:CONV_W, :]
            for s in range(1, CONV_W):
                rot = pltpu.roll(z3, s, axis=1)
                shifted = jnp.where(sub < s, rot[:-1], rot[1:]).reshape(seg, B_QK)
                y = y + shifted * cw[CONV_W - 1 - s:CONV_W - s, :]
            y = y * jnp.tanh(y) + y
            if j == 0:
                y = y * lax.rsqrt(_group_sumsq(y, 7, float(B_DK)) + EPS * B_DK)
            elif j == 1:
                y = y * lax.rsqrt(_group_sumsq(y, 7) + EPS)
            dqkv_ref[rows, cols] = y
        if j == 0:
            zs[2] = project(2)


def _inproj_delta(x2, ln_w, w_d, w_bg, conv_w, cbuf8, alog_row, dtb_row, tm, seq_len):
    n = x2.shape[0]
    tiles_per_seq = max(1, seq_len // tm)
    spt = max(1, tm // seq_len)
    n_tiles = n // tm
    const = lambda i: (0, 0)
    kern = functools.partial(_inproj_delta_kernel, tiles_per_seq=tiles_per_seq, seqs_per_tile=spt)
    return pl.pallas_call(
        kern,
        grid=(n_tiles,),
        in_specs=[
            pl.BlockSpec((tm, D_MODEL), lambda i: (i, 0)),
            pl.BlockSpec((1, D_MODEL), const),
            pl.BlockSpec((D_MODEL, CONV_CH), const, pipeline_mode=pl.Buffered(1)),
            pl.BlockSpec((D_MODEL, LANES), const, pipeline_mode=pl.Buffered(1)),
            pl.BlockSpec((CONV_W, CONV_CH), const),
            pl.BlockSpec((spt, SUBLANES, CONV_CH), lambda i: (i // tiles_per_seq, 0, 0)),
            pl.BlockSpec((1, LANES), const),
            pl.BlockSpec((1, LANES), const),
        ],
        out_specs=[
            pl.BlockSpec((tm, CONV_CH), lambda i: (i, 0)),
            pl.BlockSpec((tm, LANES), lambda i: (i, 0)),
            pl.BlockSpec((spt, SUBLANES, CONV_CH), lambda i: (i, 0, 0)),
        ],
        out_shape=[
            jax.ShapeDtypeStruct((n, CONV_CH), F32),
            jax.ShapeDtypeStruct((n, LANES), F32),
            jax.ShapeDtypeStruct((n_tiles * spt, SUBLANES, CONV_CH), F32),
        ],
        scratch_shapes=[pltpu.VMEM((SUBLANES, CONV_CH), F32)],
        compiler_params=_cparams(1),
        name="inproj_delta",
    )(x2, ln_w, w_d, w_bg, conv_w, cbuf8, alog_row, dtb_row)


def _lambda_value(lq1, lk1, lq2, lk2, lam_init):
    d1 = jnp.sum(lq1[...] * lk1[...], axis=-1, keepdims=True)
    d2 = jnp.sum(lq2[...] * lk2[...], axis=-1, keepdims=True)
    return jnp.exp(d1) - jnp.exp(d2) + lam_init


def _split_maps(q):
    lane = lax.broadcasted_iota(jnp.int32, q.shape, 1)
    zero = jnp.zeros_like(q)
    return jnp.where(lane < A_DIM, q, zero), jnp.where(lane >= A_DIM, q, zero)


def _attn_finish(o, sw, lam_init):
    ms = jnp.mean(o * o, axis=-1, keepdims=True)
    return (o * lax.rsqrt(ms + EPS) * sw * (1.0 - lam_init)).astype(BF16)


def _rows_to_sublanes(x, op):
    return functools.reduce(op, [x[i * SUBLANES:(i + 1) * SUBLANES] for i in range(x.shape[0] // SUBLANES)])


def _attn_prompt_kernel(q_ref, k_ref, v_ref, lq1, lk1, lq2, lk2, swc_ref, o_ref,
                        kb_scr, vt_scr, s_scr, p_scr, *, tq, lam_init):
    seq = q_ref.shape[1]
    lam = _lambda_value(lq1, lk1, lq2, lk2, lam_init)
    kb_scr[...] = k_ref[0].T.astype(BF16)
    vt_scr[0:A_VDIM, :] = v_ref[0].T.astype(BF16)
    vt_scr[A_VDIM:, :] = jnp.ones((vt_scr.shape[0] - A_VDIM, seq), BF16)
    ri = lax.broadcasted_iota(jnp.int32, (tq, 2 * tq), 0)
    ci = lax.broadcasted_iota(jnp.int32, (tq, 2 * tq), 1)
    diag_mask = (ri // CHUNK) <= ((ci & (tq - 1)) // CHUNK)
    n_q = seq // tq
    n_slots = s_scr.shape[0]

    def stacked_queries(qi):
        q2 = jnp.concatenate(_split_maps(q_ref[0, qi * tq:(qi + 1) * tq, :]), axis=0)
        return q2.astype(F32).T

    kt = k_ref[0]
    k_sq = kt * kt
    kmax = [jnp.max(jnp.sum(k_sq[m * A_DIM:(m + 1) * A_DIM], axis=0, keepdims=True),
                    axis=1, keepdims=True) for m in range(2)]
    kmax_row = jnp.where(lax.broadcasted_iota(jnp.int32, (1, 2 * tq), 1) < tq, kmax[0], kmax[1])

    def bounded_phase(qi):
        qf = stacked_queries(qi)
        bound = jnp.sqrt(jnp.sum(qf * qf, axis=0, keepdims=True) * kmax_row) * BOUND_SLACK + 1.0
        q2t = qf.astype(BF16)
        for j in range(qi + 1):
            p = jnp.exp2(_dot(kb_scr[j * tq:(j + 1) * tq, :], q2t) - bound)
            if j == qi:
                p = jnp.where(diag_mask, p, 0.0)
            p_scr[qi % n_slots, j * tq:(j + 1) * tq, :] = p.astype(BF16)

    def score_phase(qi):
        q2t = stacked_queries(qi).astype(BF16)
        m8 = None
        for j in range(qi + 1):
            st = _dot(kb_scr[j * tq:(j + 1) * tq, :], q2t)
            if j == qi:
                st = jnp.where(diag_mask, st, NEG_BIG)
            s_scr[qi % n_slots, j * tq:(j + 1) * tq, :] = st
            part = _rows_to_sublanes(st, jnp.maximum)
            m8 = part if m8 is None else jnp.maximum(m8, part)
        return jnp.max(m8, axis=0, keepdims=True)

    def exp_phase(qi, m_row):
        for j in range(qi + 1):
            p = jnp.exp2(s_scr[qi % n_slots, j * tq:(j + 1) * tq, :] - m_row)
            p_scr[qi % n_slots, j * tq:(j + 1) * tq, :] = p.astype(BF16)

    def value_phase(qi):
        n_keys = (qi + 1) * tq
        acc_t = _dot(vt_scr[:, :n_keys], p_scr[qi % n_slots, :n_keys, :])
        o2t = acc_t[:A_VDIM] / acc_t[A_VDIM:A_VDIM + 1]
        ot = o2t[:, :tq] - lam * o2t[:, tq:]
        ms = jnp.mean(ot * ot, axis=0, keepdims=True)
        ot = ot * lax.rsqrt(ms + EPS) * swc_ref[...] * (1.0 - lam_init)
        o_ref[0, qi * tq:(qi + 1) * tq, :] = ot.T.astype(BF16)
        return acc_t[A_VDIM:A_VDIM + 1]

    bounded_phase(0)
    den_lo = den_hi = None
    for qi in range(n_q):
        if qi + 1 < n_q:
            bounded_phase(qi + 1)
        den = value_phase(qi)
        den_lo = den if den_lo is None else jnp.minimum(den_lo, den)
        den_hi = den if den_hi is None else jnp.maximum(den_hi, den)
    trusted = jnp.logical_and(jnp.min(den_lo) >= DEN_RANGE[0], jnp.max(den_hi) <= DEN_RANGE[1])

    @pl.when(jnp.logical_not(trusted))
    def _():
        m_row = score_phase(0)
        for qi in range(n_q):
            exp_phase(qi, m_row)
            if qi + 1 < n_q:
                m_row = score_phase(qi + 1)
            value_phase(qi)


def _attn_prompt(q3, kt3, v3, lams, sw_col, lam_init, tq):
    b, l, _ = q3.shape
    vec = lambda bi, h: (0, 0)
    head = pl.BlockSpec((1, l, LANES), lambda bi, h: (bi, 0, h))
    kern = functools.partial(_attn_prompt_kernel, tq=tq, lam_init=lam_init)
    return pl.pallas_call(
        kern,
        grid=(b, A_HEADS),
        in_specs=[
            head, pl.BlockSpec((1, 2 * A_DIM, l), lambda bi, h: (bi, h, 0)), head,
            pl.BlockSpec((1, A_DIM), vec), pl.BlockSpec((1, A_DIM), vec),
            pl.BlockSpec((1, A_DIM), vec), pl.BlockSpec((1, A_DIM), vec),
            pl.BlockSpec((A_VDIM, 1), vec),
        ],
        out_specs=head,
        out_shape=jax.ShapeDtypeStruct((b, l, A_V), BF16),
        scratch_shapes=[
            pltpu.VMEM((l, 2 * A_DIM), BF16), pltpu.VMEM((A_VDIM + 2 * SUBLANES, l), BF16),
            pltpu.VMEM((2, l, 2 * tq), F32), pltpu.VMEM((2, l, 2 * tq), BF16),
        ],
        compiler_params=_cparams(2),
        name="attn_prompt",
    )(q3, kt3, v3, *lams, sw_col)


def _attn_sample_kernel(q_ref, kc_ref, vc_ref, kn_ref, vn_ref, lq1, lk1, lq2, lk2, sw_ref, o_ref,
                        *, lam_init, hpb):
    l = q_ref.shape[1]
    past = kc_ref.shape[3]
    lam = _lambda_value(lq1, lk1, lq2, lk2, lam_init)
    for hh in range(hpb):
        h = pl.program_id(1) * hpb + hh
        cols = slice(hh * LANES, (hh + 1) * LANES)
        q2 = jnp.concatenate(_split_maps(q_ref[0, :, cols]), axis=0)
        vc = vc_ref[0, pl.ds(h, past, stride=A_HEADS), :].astype(BF16)
        sc = _dot(q2, kc_ref[0, hh].astype(BF16))
        sn = _dot_nt(q2, kn_ref[0, :, cols].astype(BF16))
        mx = jnp.maximum(jnp.max(sc, axis=-1, keepdims=True), jnp.max(sn, axis=-1, keepdims=True))
        pc = jnp.exp2(sc - mx)
        pn = jnp.exp2(sn - mx)
        den = jnp.sum(pc, axis=-1, keepdims=True) + jnp.sum(pn, axis=-1, keepdims=True)
        o2 = (_dot(pc.astype(BF16), vc)
              + _dot(pn.astype(BF16), vn_ref[0, :, cols].astype(BF16))) / den
        o_ref[0, :, cols] = _attn_finish(o2[:l] - lam * o2[l:], sw_ref[...], lam_init)


def _attn_sample(q3, kct, vc2, kn3, vn3, lams, sw_row, lam_init, hpb=2):
    b, l, _ = q3.shape
    past = kct.shape[3]
    vec = lambda bi, h: (0, 0)
    blk = lambda rows: pl.BlockSpec((1, rows, hpb * LANES), lambda bi, h: (bi, 0, h))
    kern = functools.partial(_attn_sample_kernel, lam_init=lam_init, hpb=hpb)
    return pl.pallas_call(
        kern,
        grid=(b, A_HEADS // hpb),
        in_specs=[
            blk(l),
            pl.BlockSpec((1, hpb, 2 * A_DIM, past), lambda bi, h: (bi, h, 0, 0)),
            pl.BlockSpec((1, past * A_HEADS, A_VDIM), lambda bi, h: (bi, 0, 0)),
            blk(l), blk(l),
            pl.BlockSpec((1, A_DIM), vec), pl.BlockSpec((1, A_DIM), vec),
            pl.BlockSpec((1, A_DIM), vec), pl.BlockSpec((1, A_DIM), vec),
            pl.BlockSpec((1, A_VDIM), vec),
        ],
        out_specs=blk(l),
        out_shape=jax.ShapeDtypeStruct((b, l, A_V), BF16),
        compiler_params=_cparams(2),
        name="attn_sample",
    )(q3, kct, vc2, kn3, vn3, *lams, sw_row)


def _delta_kernel(q_ref, k_ref, v_ref, bg_ref, s0_ref, nw_ref, o_ref, sout_ref, s_scr,
                  *, c, nblk, hb, gb):
    t = pl.program_id(2)

    @pl.when(t == 0)
    def _():
        s_scr[...] = s0_ref[0]

    bg = bg_ref[0]
    lane = lax.broadcasted_iota(jnp.int32, bg.shape, 1)
    bg_t = bg.T
    sub = lax.broadcasted_iota(jnp.int32, bg_t.shape, 0)
    ri = lax.broadcasted_iota(jnp.int32, (c, c), 0)
    ci = lax.broadcasted_iota(jnp.int32, (c, c), 1)
    causal = ri >= ci
    strict = ri > ci
    ltri = causal.astype(BF16)
    utri = (ri <= ci).astype(BF16)
    ltri2 = jnp.concatenate([ltri, ltri], axis=1)
    utri2 = jnp.concatenate([utri, utri], axis=0)
    eye = jnp.where(ri == ci, 1.0, 0.0)

    gates = []
    for hh in range(hb):
        h = pl.program_id(1) * hb + hh
        gates.append((
            jnp.sum(jnp.where(lane == h, bg, 0.0), axis=-1, keepdims=True),
            jnp.sum(jnp.where(lane == h + B_HEADS, bg, 0.0), axis=-1, keepdims=True),
            jnp.sum(jnp.where(sub == h + B_HEADS, bg_t, 0.0), axis=0, keepdims=True)))
    pre = {}
    states = [s_scr[hh] for hh in range(hb)]

    def prepare(blocks):
        items = [(hh, blk) for blk in blocks for hh in range(hb)]
        for hh, blk in items:
            rows = slice(blk * c, (blk + 1) * c)
            cols = slice(hh * LANES, (hh + 1) * LANES)
            bc = gates[hh][0][rows, :]
            q = q_ref[0, rows, cols]
            k = k_ref[0, rows, cols]
            v = v_ref[0, rows, cols]
            ghi, glo = _split(jnp.broadcast_to(gates[hh][1][rows, :], (c, LANES)))
            g_cum = _dot(ltri2, jnp.concatenate([ghi, glo], axis=0))
            rhi, rlo = _split(gates[hh][2][:, rows])
            g_rows = jnp.broadcast_to(jnp.concatenate([rhi, rlo], axis=1), (c, 2 * c))
            g_cum_row = _dot(g_rows, utri2)
            diff = jnp.where(causal, g_cum[:, :c] - g_cum_row, 0.0)
            decay = jnp.where(causal, jnp.exp(diff), 0.0)
            kb = k.astype(BF16)
            kq = _dot_nt(jnp.concatenate([kb, q.astype(BF16)], axis=0), kb)
            m = jnp.where(strict, bc * kq[:c] * decay, 0.0)
            pre[hh, blk] = dict(q=q, k=k, v=v, bc=bc, g_cum=g_cum, decay=decay, qk=kq[c:])
            pre[hh, blk]["pq"] = (eye - m, -m)
        yield
        for it in items:
            pm, qm = pre[it]["pq"]
            qb = qm.astype(BF16)
            pre[it]["pq"] = (pm, _dot(qb, qb))
        yield
        for _ in range(int(math.log2(c)) - 2):
            for it in items:
                pm, qm = pre[it]["pq"]
                qb = qm.astype(BF16)
                both = _dot(jnp.concatenate([pm.astype(BF16), qb], axis=0), qb)
                pre[it]["pq"] = (pm + both[:c], both[c:])
            yield
        for it in items:
            pm, qm = pre[it]["pq"]
            pre[it]["pq"] = (pm + _dot(pm.astype(BF16), qm.astype(BF16)), qm)
        yield
        for it in items:
            d = pre[it]
            e_g = jnp.exp(d["g_cum"])
            rhs = jnp.concatenate([d["v"] * d["bc"], d["k"] * (d["bc"] * e_g)], axis=1)
            sol = _dot(d.pop("pq")[0].astype(BF16), rhs.astype(BF16))
            g_last = d["g_cum"][c - 1:c, :]
            a_loc = jnp.where(causal, d["qk"] * d["decay"], 0.0).astype(BF16)
            k_tail_t = (d["k"] * jnp.exp(g_last - d["g_cum"])).T.astype(BF16)
            d.update(
                u=sol[:, :B_DV],
                state_lhs=jnp.concatenate(
                    [sol[:, B_DV:].astype(BF16), (d["q"] * e_g).astype(BF16)], axis=0),
                value_lhs=jnp.concatenate([a_loc, k_tail_t], axis=0),
                g_tail=jnp.exp(g_last))
        yield

    def scan(blocks):
        for blk in blocks:
            vbs, o_state = [], []
            for hh in range(hb):
                d = pre[hh, blk]
                ws = _dot(d["state_lhs"], states[hh].astype(BF16))
                vbs.append((d["u"] - ws[:c]).astype(BF16))
                o_state.append(ws[c:])
            yield
            for hh in range(hb):
                d = pre.pop((hh, blk))
                av = _dot(d["value_lhs"], vbs[hh])
                o = o_state[hh] + av[:c]
                states[hh] = states[hh] * d["g_tail"] + av[c:]
                o_ref[0, blk * c:(blk + 1) * c, hh * LANES:(hh + 1) * LANES] = (
                    _rms_rows(o, nw_ref[...]).astype(BF16))
            yield

    def run(*stage_iters):
        live = list(stage_iters)
        while live:
            live = [it for it in live if next(it, _DONE) is not _DONE]

    groups = [list(range(g, min(g + gb, nblk))) for g in range(0, nblk, gb)]
    run(prepare(groups[0]))
    for g in range(1, len(groups)):
        run(prepare(groups[g]), scan(groups[g - 1]))
    run(scan(groups[-1]))
    for hh in range(hb):
        s_scr[hh] = states[hh]

    @pl.when(t == pl.num_programs(2) - 1)
    def _():
        for hh in range(hb):
            sout_ref[0, hh] = states[hh]


def _delta(dqkv3, bg3, s0, nw_row, c, tl, hb, gb):
    b, l, _ = dqkv3.shape
    n_hg = B_HEADS // hb
    kern = functools.partial(_delta_kernel, c=c, nblk=tl // c, hb=hb, gb=gb)
    lane_blk = lambda off: pl.BlockSpec((1, tl, hb * LANES), lambda bi, h, t: (bi, t, h + off))
    state = pl.BlockSpec((1, hb, B_DK, B_DV), lambda bi, h, t: (bi, h, 0, 0))
    return pl.pallas_call(
        kern,
        grid=(b, n_hg, l // tl),
        in_specs=[
            lane_blk(0), lane_blk(n_hg), lane_blk(2 * n_hg),
            pl.BlockSpec((1, tl, LANES), lambda bi, h, t: (bi, t, 0)),
            state,
            pl.BlockSpec((1, B_DV), lambda bi, h, t: (0, 0)),
        ],
        out_specs=[lane_blk(0), state],
        out_shape=[
            jax.ShapeDtypeStruct((b, l, B_V), BF16),
            jax.ShapeDtypeStruct((b, B_HEADS, B_DK, B_DV), F32),
        ],
        scratch_shapes=[pltpu.VMEM((hb, B_DK, B_DV), F32)],
        compiler_params=_cparams(3),
        name="delta_rule",
    )(dqkv3, dqkv3, dqkv3, bg3, s0, nw_row)


def _merge_kernel(x_ref, oa_ref, ob_ref, ln_ref, wdg, wga, wgb, wba, wbb, wout, y_ref):
    x = x_ref[...]
    h = _rms_rows(x, ln_ref[...]).astype(BF16)
    ob = ob_ref[...].astype(F32) * _silu(_dot(h, wdg[...]))
    branch_b = _dot(ob.astype(BF16), wbb[...])
    branch_a = _dot(oa_ref[...], wba[...])
    merged = (_sigmoid(_dot(h, wga[...])) * branch_a
              + _sigmoid(_dot(h, wgb[...])) * branch_b)
    y_ref[...] = x + _dot(merged.astype(BF16), wout[...])


def _merge(x2, oa2, ob2, ln_w, wdg, wga, wgb, wba, wbb, wout, tm):
    n = x2.shape[0]
    tile = pl.BlockSpec((tm, D_MODEL), lambda i: (i, 0))
    wspec = pl.BlockSpec((D_MODEL, D_MODEL), lambda i: (0, 0), pipeline_mode=pl.Buffered(1))
    return pl.pallas_call(
        _merge_kernel,
        grid=(n // tm,),
        in_specs=[tile, tile, tile, pl.BlockSpec((1, D_MODEL), lambda i: (0, 0))] + [wspec] * 6,
        out_specs=tile,
        out_shape=jax.ShapeDtypeStruct((n, D_MODEL), F32),
        compiler_params=_cparams(1),
        name="merge_outproj",
    )(x2, oa2, ob2, ln_w, wdg, wga, wgb, wba, wbb, wout)


def _ffn_kernel(x_ref, ln_ref, wgu_ref, wd_ref, y_ref, *, n_chunks):
    x = x_ref[...]
    h = _rms_rows(x, ln_ref[...]).astype(BF16)
    n_tiles = D_FF // MXU_DIM
    edges = [MXU_DIM * ((n_tiles * c + n_chunks - 1) // n_chunks) for c in range(n_chunks + 1)]
    acts = []
    for lo, hi in zip(edges[:-1], edges[1:]):
        gate = _dot(h, wgu_ref[:, lo:hi])
        up = _dot(h, wgu_ref[:, D_FF + lo:D_FF + hi])
        acts.append((_silu(gate) * up).astype(BF16))
    y = x
    for (lo, hi), act in zip(zip(edges[:-1], edges[1:]), acts):
        y = y + _dot(act, wd_ref[lo:hi, :])
    y_ref[...] = y


def _ffn(x2, ln_w, w_gate_up, w_down, tm, n_chunks):
    n = x2.shape[0]
    const = lambda i: (0, 0)
    return pl.pallas_call(
        functools.partial(_ffn_kernel, n_chunks=n_chunks),
        grid=(n // tm,),
        in_specs=[
            pl.BlockSpec((tm, D_MODEL), lambda i: (i, 0)),
            pl.BlockSpec((1, D_MODEL), const),
            pl.BlockSpec((D_MODEL, 2 * D_FF), const, pipeline_mode=pl.Buffered(1)),
            pl.BlockSpec((D_FF, D_MODEL), const, pipeline_mode=pl.Buffered(1)),
        ],
        out_specs=pl.BlockSpec((tm, D_MODEL), lambda i: (i, 0)),
        out_shape=jax.ShapeDtypeStruct((n, D_MODEL), F32),
        compiler_params=_cparams(1),
        name="swiglu_ffn",
    )(x2, ln_w, w_gate_up, w_down)


def _pick_tile(n, target):
    t = min(n, target)
    while n % t:
        t //= 2
    return t


def _prep_weights(ln1_w, w_in, q_norm_w, k_norm_w, lambda_q1, lambda_k1, lambda_q2, lambda_k2,
                  subln_w, w_branch_a, conv_w, a_log, dt_bias, delta_norm_w, w_branch_b,
                  w_out, ln2_w, w_gate_up, w_down):
    o_d = 2 * A_QK + A_V
    o_dg = o_d + CONV_CH
    o_bg = o_dg + B_V
    o_ga = o_bg + 2 * B_HEADS
    o_gb = o_ga + D_MODEL
    pad_lanes = lambda v: jnp.pad(v.astype(F32), (B_HEADS, LANES - 2 * B_HEADS))[None, :]
    return dict(
        ln1=ln1_w[None, :], ln2=ln2_w[None, :],
        w_qkv=w_in[:, :o_d].astype(BF16),
        w_d=w_in[:, o_d:o_dg].astype(BF16),
        w_dg=w_in[:, o_dg:o_bg].astype(BF16),
        w_bg=jnp.pad(w_in[:, o_bg:o_ga], ((0, 0), (0, LANES - 2 * B_HEADS))).astype(BF16),
        w_ga=w_in[:, o_ga:o_gb].astype(BF16),
        w_gb=w_in[:, o_gb:].astype(BF16),
        qn=jnp.tile(q_norm_w, A_QK // A_DIM)[None, :],
        kn=jnp.tile(k_norm_w, A_QK // A_DIM)[None, :],
        kn_col=jnp.tile(k_norm_w, A_QK // A_DIM)[:, None],
        w_kt=w_in[:, A_QK:2 * A_QK].T.astype(BF16),
        lams=tuple(v[None, :] for v in (lambda_q1, lambda_k1, lambda_q2, lambda_k2)),
        subln=subln_w[None, :], subln_col=subln_w[:, None],
        w_ba=w_branch_a.astype(BF16), w_bb=w_branch_b.astype(BF16),
        conv_w=conv_w, alog=pad_lanes(a_log), dtb=pad_lanes(dt_bias),
        dnorm=delta_norm_w[None, :],
        w_out=w_out.astype(BF16),
        w_gate_up=w_gate_up.astype(BF16), w_down=w_down.astype(BF16),
    )


def _layer(x, past_k, past_v, s0, conv_buf, lam_init, p):
    b, l, _ = x.shape
    n = b * l
    x2 = x.reshape(n, D_MODEL)

    prompt = past_k is None
    tm_a = _pick_tile(l, 1024) if prompt else _pick_tile(n, 1024)
    q2, k2, v2 = _inproj_attn(x2, p["ln1"], p["w_qkv"], p["w_kt"], p["qn"], p["kn"], p["kn_col"],
                              tm_a, l, prompt)

    tm_d = _pick_tile(l, 1024) if l >= 512 else l * _pick_tile(b, max(1, 512 // l))
    cbuf8 = jnp.pad(conv_buf, ((0, 0), (SUBLANES - (CONV_W - 1), 0), (0, 0)))
    dqkv, bg, cst = _inproj_delta(x2, p["ln1"], p["w_d"], p["w_bg"], p["conv_w"], cbuf8,
                                  p["alog"], p["dtb"], tm_d, l)
    tiles_per_seq = max(1, l // tm_d)
    new_conv = cst[tiles_per_seq - 1::tiles_per_seq, SUBLANES - (CONV_W - 1):, :]

    q3 = q2.reshape(b, l, A_QK)
    v3 = v2.reshape(b, l, A_V)
    if prompt:
        oa = _attn_prompt(q3, k2, v3, p["lams"], p["subln_col"], lam_init, _pick_tile(l, 128))
        k_out = jnp.transpose(k2.reshape(b, A_HEADS, 2, A_DIM, l), (0, 4, 1, 2, 3))
    else:
        past = past_k.shape[1]
        k3 = k2.reshape(b, l, A_QK)
        kct = jnp.transpose(past_k, (0, 2, 3, 4, 1)).reshape(b, A_HEADS, 2 * A_DIM, past)
        oa = _attn_sample(q3, kct, past_v.reshape(b, past * A_HEADS, A_VDIM), k3, v3, p["lams"],
                          p["subln"], lam_init)
        k_out = k3.reshape(b, l, A_HEADS, 2, A_DIM)

    c = DELTA_BLOCK if l % DELTA_BLOCK == 0 else l
    tl = c * max(1, min(8, l // c))
    ob, s_new = _delta(dqkv.reshape(b, l, CONV_CH), bg.reshape(b, l, LANES), s0, p["dnorm"], c, tl,
                       B_HEADS, 2)

    x1 = _merge(x2, oa.reshape(n, A_V), ob.reshape(n, B_V), p["ln1"], p["w_dg"], p["w_ga"],
                p["w_gb"], p["w_ba"], p["w_bb"], p["w_out"], _pick_tile(n, 512))
    y = _ffn(x1, p["ln2"], p["w_gate_up"], p["w_down"], _pick_tile(n, 512), 2)
    return (y.reshape(b, l, D_MODEL), k_out, v3.reshape(b, l, A_HEADS, A_VDIM), s_new, new_conv)


def kernel(x_prompt, x_sample, cache_k, cache_v, state_delta, state_conv, ln1_w, w_in, q_norm_w,
           k_norm_w, lambda_q1, lambda_k1, lambda_q2, lambda_k2, subln_w, w_branch_a, conv_w,
           a_log, dt_bias, delta_norm_w, w_branch_b, w_out, ln2_w, w_gate_up, w_down):
    depth = ln1_w.shape[0]
    xp, xs = x_prompt, x_sample
    outs_p, outs_s = [], []
    for layer in range(depth):
        lam_init = 0.8 - 0.6 * math.exp(-0.3 * layer)
        p = _prep_weights(*(w[layer] for w in (
            ln1_w, w_in, q_norm_w, k_norm_w, lambda_q1, lambda_k1, lambda_q2, lambda_k2, subln_w,
            w_branch_a, conv_w, a_log, dt_bias, delta_norm_w, w_branch_b, w_out, ln2_w,
            w_gate_up, w_down)))
        bp = xp.shape[0]
        zero_conv = jnp.zeros((bp, CONV_W - 1, CONV_CH), xp.dtype)
        zero_s = jnp.zeros((bp, B_HEADS, B_DK, B_DV), F32)
        xp, *rest_p = _layer(xp, None, None, zero_s, zero_conv, lam_init, p)
        xs, *rest_s = _layer(xs, cache_k[layer], cache_v[layer], state_delta[layer],
                             state_conv[layer], lam_init, p)
        outs_p.append(rest_p)
        outs_s.append(rest_s)
    stack = lambda outs, idx: jnp.stack([o[idx] for o in outs])
    return (xp, xs,
            stack(outs_p, 0), stack(outs_p, 1), stack(outs_p, 2), stack(outs_p, 3),
            stack(outs_s, 0), stack(outs_s, 1), stack(outs_s, 2), stack(outs_s, 3))
```

```python
import functools
import math

import jax
import jax.numpy as jnp
from jax import lax
from jax.experimental import pallas as pl
from jax.experimental.pallas import tpu as pltpu

F32 = jnp.float32
BF16 = jnp.bfloat16

D_MODEL = 1024
CHUNK = 64
A_HEADS = 8
A_DIM = 64
A_VDIM = 2 * A_DIM
B_HEADS = 8
B_DK = 128
B_DV = 128
CONV_W = 4
DELTA_BLOCK = 64
D_FF = -(-8 * D_MODEL // (3 * 256)) * 256
EPS = 1e-6

A_QK = A_HEADS * 2 * A_DIM
A_V = A_HEADS * A_VDIM
B_QK = B_HEADS * B_DK
B_V = B_HEADS * B_DV
CONV_CH = 2 * B_QK + B_V

LANES = 128
SUBLANES = 8
MXU_DIM = 256
NEG_BIG = -1e30
_DONE = object()
BOUND_SLACK = 1.0 + 2.0 ** -6
DEN_RANGE = (2.0 ** -64, 2.0 ** 64)
Q_SCALE = (A_DIM ** -0.5) * math.log2(math.e)
VMEM_LIMIT = 56 * 1024 * 1024


def _cparams(n_axes):
    return pltpu.CompilerParams(dimension_semantics=("arbitrary",) * n_axes,
                                vmem_limit_bytes=VMEM_LIMIT)


def _dot(a, b):
    return jnp.dot(a, b, preferred_element_type=F32)


def _dot_nt(a, b):
    return lax.dot_general(a, b, (((1,), (1,)), ((), ())), preferred_element_type=F32)


def _split(a):
    hi = a.astype(BF16)
    lo = (a - hi.astype(F32)).astype(BF16)
    return hi, lo


def _rms_rows(x, w):
    ms = jnp.mean(x * x, axis=-1, keepdims=True)
    return x * lax.rsqrt(ms + EPS) * w


def _group_sumsq(z, gshift, weight=1.0):
    r = lax.broadcasted_iota(jnp.int32, (MXU_DIM, MXU_DIM), 0) >> gshift
    c = lax.broadcasted_iota(jnp.int32, (MXU_DIM, MXU_DIM), 1) >> gshift
    blk = jnp.where(r == c, weight, 0.0).astype(BF16)
    outs = []
    for s in range(z.shape[1] // MXU_DIM):
        zs = z[:, s * MXU_DIM:(s + 1) * MXU_DIM]
        outs.append(_dot((zs * zs).astype(BF16), blk))
    return jnp.concatenate(outs, axis=1)


def _sigmoid(x):
    return 0.5 * jnp.tanh(0.5 * x) + 0.5


def _silu(x):
    half = 0.5 * x
    return half * jnp.tanh(half) + half


def _inproj_attn_kernel(x_ref, ln_ref, w_ref, wkt_ref, qn_ref, kn_ref, knc_ref, q_ref, k_ref, v_ref,
                        *, k_transposed):
    h = _rms_rows(x_ref[...], ln_ref[...]).astype(BF16)
    zq = _dot(h, w_ref[:, 0:A_QK])
    if k_transposed:
        zk = _dot_nt(wkt_ref[...], h)
    else:
        zk = _dot(h, w_ref[:, A_QK:2 * A_QK])
    v_ref[...] = _dot(h, w_ref[:, 2 * A_QK:])

    n = zq * lax.rsqrt(_group_sumsq(zq, 6) * (1.0 / A_DIM) + EPS) * qn_ref[...]
    q_ref[...] = (n * Q_SCALE).astype(BF16)
    if k_transposed:
        z3 = zk.reshape(A_QK // A_DIM, A_DIM, zk.shape[1])
        ms = jnp.mean(z3 * z3, axis=1, keepdims=True)
        kn3 = knc_ref[...].reshape(A_QK // A_DIM, A_DIM, 1)
        k_ref[0] = (z3 * lax.rsqrt(ms + EPS) * kn3).reshape(zk.shape)
    else:
        k_ref[...] = zk * lax.rsqrt(_group_sumsq(zk, 6) * (1.0 / A_DIM) + EPS) * kn_ref[...]


def _inproj_attn(x2, ln_w, w_qkv, w_kt, qn_row, kn_row, kn_col, tm, seq_len, k_transposed):
    n = x2.shape[0]
    const = lambda i: (0, 0)
    if k_transposed:
        tiles_per_seq = seq_len // tm
        k_spec = pl.BlockSpec((1, A_QK, tm), lambda i: (i // tiles_per_seq, 0, i % tiles_per_seq))
        k_shape = jax.ShapeDtypeStruct((n // seq_len, A_QK, seq_len), F32)
    else:
        k_spec = pl.BlockSpec((tm, A_QK), lambda i: (i, 0))
        k_shape = jax.ShapeDtypeStruct((n, A_QK), F32)
    return pl.pallas_call(
        functools.partial(_inproj_attn_kernel, k_transposed=k_transposed),
        grid=(n // tm,),
        in_specs=[
            pl.BlockSpec((tm, D_MODEL), lambda i: (i, 0)),
            pl.BlockSpec((1, D_MODEL), const),
            pl.BlockSpec((D_MODEL, 2 * A_QK + A_V), const, pipeline_mode=pl.Buffered(1)),
            pl.BlockSpec((A_QK, D_MODEL), const, pipeline_mode=pl.Buffered(1)),
            pl.BlockSpec((1, A_QK), const),
            pl.BlockSpec((1, A_QK), const),
            pl.BlockSpec((A_QK, 1), const),
        ],
        out_specs=[
            pl.BlockSpec((tm, A_QK), lambda i: (i, 0)),
            k_spec,
            pl.BlockSpec((tm, A_V), lambda i: (i, 0)),
        ],
        out_shape=[
            jax.ShapeDtypeStruct((n, A_QK), BF16),
            k_shape,
            jax.ShapeDtypeStruct((n, A_V), F32),
        ],
        compiler_params=_cparams(1),
        name="inproj_attn",
    )(x2, ln_w, w_qkv, w_kt, qn_row, kn_row, kn_col)


def _inproj_delta_kernel(x_ref, ln_ref, w_ref, wbg_ref, cw_ref, cbuf_ref, alog_ref, dtb_ref,
                         dqkv_ref, bg_ref, cst_ref, carry, *, tiles_per_seq, seqs_per_tile):
    i = pl.program_id(0)
    tm = x_ref.shape[0]
    h = _rms_rows(x_ref[...], ln_ref[...]).astype(BF16)
    project = lambda j: _dot(h, w_ref[:, j * B_QK:(j + 1) * B_QK])
    zs = {0: project(0), 1: project(1)}
    zb = _dot(h, wbg_ref[...])

    lane = lax.broadcasted_iota(jnp.int32, zb.shape, 1)
    a = zb + dtb_ref[...]
    softplus = jnp.maximum(a, 0.0) + jnp.log1p(jnp.exp(-jnp.abs(a)))
    g = -jnp.exp(alog_ref[...]) * softplus
    bg_ref[...] = jnp.where(lane < B_HEADS, _sigmoid(zb), jnp.where(lane < 2 * B_HEADS, g, 0.0))

    first = (i % tiles_per_seq) == 0
    seg = tm // seqs_per_tile
    for j in range(3):
        zfull_tile = zs.pop(j)
        cols = slice(j * B_QK, (j + 1) * B_QK)
        cw = cw_ref[:, cols] * 0.5
        for sq in range(seqs_per_tile):
            rows = slice(sq * seg, (sq + 1) * seg)
            z = zfull_tile[rows, :]
            prev = jnp.where(first, cbuf_ref[sq, :, cols], carry[:, cols])
            tail = z[seg - SUBLANES:, :]
            if sq == seqs_per_tile - 1:
                carry[:, cols] = tail
            cst_ref[sq, :, cols] = tail
            z3 = jnp.concatenate([prev, z], axis=0).reshape(seg // SUBLANES + 1, SUBLANES, B_QK)
            sub = lax.broadcasted_iota(jnp.int32, (1, SUBLANES, 1), 1)
            y = z * cw[CONV_W - 1:CONV_W, :]
            for s in range(1, CONV_W):
                rot = pltpu.roll(z3, s, axis=1)
                shifted = jnp.where(sub < s, rot[:-1], rot[1:]).reshape(seg, B_QK)
                y = y + shifted * cw[CONV_W - 1 - s:CONV_W - s, :]
            y = y * jnp.tanh(y) + y
            if j == 0:
                y = y * lax.rsqrt(_group_sumsq(y, 7, float(B_DK)) + EPS * B_DK)
            elif j == 1:
                y = y * lax.rsqrt(_group_sumsq(y, 7) + EPS)
            dqkv_ref[rows, cols] = y
        if j == 0:
            zs[2] = project(2)


def _inproj_delta(x2, ln_w, w_d, w_bg, conv_w, cbuf8, alog_row, dtb_row, tm, seq_len):
    n = x2.shape[0]
    tiles_per_seq = max(1, seq_len // tm)
    spt = max(1, tm // seq_len)
    n_tiles = n // tm
    const = lambda i: (0, 0)
    kern = functools.partial(_inproj_delta_kernel, tiles_per_seq=tiles_per_seq, seqs_per_tile=spt)
    return pl.pallas_call(
        kern,
        grid=(n_tiles,),
        in_specs=[
            pl.BlockSpec((tm, D_MODEL), lambda i: (i, 0)),
            pl.BlockSpec((1, D_MODEL), const),
            pl.BlockSpec((D_MODEL, CONV_CH), const, pipeline_mode=pl.Buffered(1)),
            pl.BlockSpec((D_MODEL, LANES), const, pipeline_mode=pl.Buffered(1)),
            pl.BlockSpec((CONV_W, CONV_CH), const),
            pl.BlockSpec((spt, SUBLANES, CONV_CH), lambda i: (i // tiles_per_seq, 0, 0)),
            pl.BlockSpec((1, LANES), const),
            pl.BlockSpec((1, LANES), const),
        ],
        out_specs=[
            pl.BlockSpec((tm, CONV_CH), lambda i: (i, 0)),
            pl.BlockSpec((tm, LANES), lambda i: (i, 0)),
            pl.BlockSpec((spt, SUBLANES, CONV_CH), lambda i: (i, 0, 0)),
        ],
        out_shape=[
            jax.ShapeDtypeStruct((n, CONV_CH), F32),
            jax.ShapeDtypeStruct((n, LANES), F32),
            jax.ShapeDtypeStruct((n_tiles * spt, SUBLANES, CONV_CH), F32),
        ],
        scratch_shapes=[pltpu.VMEM((SUBLANES, CONV_CH), F32)],
        compiler_params=_cparams(1),
        name="inproj_delta",
    )(x2, ln_w, w_d, w_bg, conv_w, cbuf8, alog_row, dtb_row)


def _lambda_value(lq1, lk1, lq2, lk2, lam_init):
    d1 = jnp.sum(lq1[...] * lk1[...], axis=-1, keepdims=True)
    d2 = jnp.sum(lq2[...] * lk2[...], axis=-1, keepdims=True)
    return jnp.exp(d1) - jnp.exp(d2) + lam_init


def _split_maps(q):
    lane = lax.broadcasted_iota(jnp.int32, q.shape, 1)
    zero = jnp.zeros_like(q)
    return jnp.where(lane < A_DIM, q, zero), jnp.where(lane >= A_DIM, q, zero)


def _attn_finish(o, sw, lam_init):
    ms = jnp.mean(o * o, axis=-1, keepdims=True)
    return (o * lax.rsqrt(ms + EPS) * sw * (1.0 - lam_init)).astype(BF16)


def _rows_to_sublanes(x, op):
    return functools.reduce(op, [x[i * SUBLANES:(i + 1) * SUBLANES] for i in range(x.shape[0] // SUBLANES)])


def _attn_prompt_kernel(q_ref, k_ref, v_ref, lq1, lk1, lq2, lk2, swc_ref, o_ref,
                        kb_scr, vt_scr, s_scr, p_scr, *, tq, lam_init):
    seq = q_ref.shape[1]
    lam = _lambda_value(lq1, lk1, lq2, lk2, lam_init)
    kb_scr[...] = k_ref[0].T.astype(BF16)
    vt_scr[0:A_VDIM, :] = v_ref[0].T.astype(BF16)
    vt_scr[A_VDIM:, :] = jnp.ones((vt_scr.shape[0] - A_VDIM, seq), BF16)
    ri = lax.broadcasted_iota(jnp.int32, (tq, 2 * tq), 0)
    ci = lax.broadcasted_iota(jnp.int32, (tq, 2 * tq), 1)
    diag_mask = (ri // CHUNK) <= ((ci & (tq - 1)) // CHUNK)
    n_q = seq // tq
    n_slots = s_scr.shape[0]

    def stacked_queries(qi):
        q2 = jnp.concatenate(_split_maps(q_ref[0, qi * tq:(qi + 1) * tq, :]), axis=0)
        return q2.astype(F32).T

    kt = k_ref[0]
    k_sq = kt * kt
    kmax = [jnp.max(jnp.sum(k_sq[m * A_DIM:(m + 1) * A_DIM], axis=0, keepdims=True),
                    axis=1, keepdims=True) for m in range(2)]
    kmax_row = jnp.where(lax.broadcasted_iota(jnp.int32, (1, 2 * tq), 1) < tq, kmax[0], kmax[1])

    def bounded_phase(qi):
        qf = stacked_queries(qi)
        bound = jnp.sqrt(jnp.sum(qf * qf, axis=0, keepdims=True) * kmax_row) * BOUND_SLACK + 1.0
        q2t = qf.astype(BF16)
        for j in range(qi + 1):
            p = jnp.exp2(_dot(kb_scr[j * tq:(j + 1) * tq, :], q2t) - bound)
            if j == qi:
                p = jnp.where(diag_mask, p, 0.0)
            p_scr[qi % n_slots, j * tq:(j + 1) * tq, :] = p.astype(BF16)

    def score_phase(qi):
        q2t = stacked_queries(qi).astype(BF16)
        m8 = None
        for j in range(qi + 1):
            st = _dot(kb_scr[j * tq:(j + 1) * tq, :], q2t)
            if j == qi:
                st = jnp.where(diag_mask, st, NEG_BIG)
            s_scr[qi % n_slots, j * tq:(j + 1) * tq, :] = st
            part = _rows_to_sublanes(st, jnp.maximum)
            m8 = part if m8 is None else jnp.maximum(m8, part)
        return jnp.max(m8, axis=0, keepdims=True)

    def exp_phase(qi, m_row):
        for j in range(qi + 1):
            p = jnp.exp2(s_scr[qi % n_slots, j * tq:(j + 1) * tq, :] - m_row)
            p_scr[qi % n_slots, j * tq:(j + 1) * tq, :] = p.astype(BF16)

    def value_phase(qi):
        n_keys = (qi + 1) * tq
        acc_t = _dot(vt_scr[:, :n_keys], p_scr[qi % n_slots, :n_keys, :])
        o2t = acc_t[:A_VDIM] / acc_t[A_VDIM:A_VDIM + 1]
        ot = o2t[:, :tq] - lam * o2t[:, tq:]
        ms = jnp.mean(ot * ot, axis=0, keepdims=True)
        ot = ot * lax.rsqrt(ms + EPS) * swc_ref[...] * (1.0 - lam_init)
        o_ref[0, qi * tq:(qi + 1) * tq, :] = ot.T.astype(BF16)
        return acc_t[A_VDIM:A_VDIM + 1]

    bounded_phase(0)
    den_lo = den_hi = None
    for qi in range(n_q):
        if qi + 1 < n_q:
            bounded_phase(qi + 1)
        den = value_phase(qi)
        den_lo = den if den_lo is None else jnp.minimum(den_lo, den)
        den_hi = den if den_hi is None else jnp.maximum(den_hi, den)
    trusted = jnp.logical_and(jnp.min(den_lo) >= DEN_RANGE[0], jnp.max(den_hi) <= DEN_RANGE[1])

    @pl.when(jnp.logical_not(trusted))
    def _():
        m_row = score_phase(0)
        for qi in range(n_q):
            exp_phase(qi, m_row)
            if qi + 1 < n_q:
                m_row = score_phase(qi + 1)
            value_phase(qi)


def _attn_prompt(q3, kt3, v3, lams, sw_col, lam_init, tq):
    b, l, _ = q3.shape
    vec = lambda bi, h: (0, 0)
    head = pl.BlockSpec((1, l, LANES), lambda bi, h: (bi, 0, h))
    kern = functools.partial(_attn_prompt_kernel, tq=tq, lam_init=lam_init)
    return pl.pallas_call(
        kern,
        grid=(b, A_HEADS),
        in_specs=[
            head, pl.BlockSpec((1, 2 * A_DIM, l), lambda bi, h: (bi, h, 0)), head,
            pl.BlockSpec((1, A_DIM), vec), pl.BlockSpec((1, A_DIM), vec),
            pl.BlockSpec((1, A_DIM), vec), pl.BlockSpec((1, A_DIM), vec),
            pl.BlockSpec((A_VDIM, 1), vec),
        ],
        out_specs=head,
        out_shape=jax.ShapeDtypeStruct((b, l, A_V), BF16),
        scratch_shapes=[
            pltpu.VMEM((l, 2 * A_DIM), BF16), pltpu.VMEM((A_VDIM + 2 * SUBLANES, l), BF16),
            pltpu.VMEM((2, l, 2 * tq), F32), pltpu.VMEM((2, l, 2 * tq), BF16),
        ],
        compiler_params=_cparams(2),
        name="attn_prompt",
    )(q3, kt3, v3, *lams, sw_col)


def _attn_sample_kernel(q_ref, kc_ref, vc_ref, kn_ref, vn_ref, lq1, lk1, lq2, lk2, sw_ref, o_ref,
                        *, lam_init, hpb):
    l = q_ref.shape[1]
    past = kc_ref.shape[3]
    lam = _lambda_value(lq1, lk1, lq2, lk2, lam_init)
    for hh in range(hpb):
        h = pl.program_id(1) * hpb + hh
        cols = slice(hh * LANES, (hh + 1) * LANES)
        q2 = jnp.concatenate(_split_maps(q_ref[0, :, cols]), axis=0)
        vc = vc_ref[0, pl.ds(h, past, stride=A_HEADS), :].astype(BF16)
        sc = _dot(q2, kc_ref[0, hh].astype(BF16))
        sn = _dot_nt(q2, kn_ref[0, :, cols].astype(BF16))
        mx = jnp.maximum(jnp.max(sc, axis=-1, keepdims=True), jnp.max(sn, axis=-1, keepdims=True))
        pc = jnp.exp2(sc - mx)
        pn = jnp.exp2(sn - mx)
        den = jnp.sum(pc, axis=-1, keepdims=True) + jnp.sum(pn, axis=-1, keepdims=True)
        o2 = (_dot(pc.astype(BF16), vc)
              + _dot(pn.astype(BF16), vn_ref[0, :, cols].astype(BF16))) / den
        o_ref[0, :, cols] = _attn_finish(o2[:l] - lam * o2[l:], sw_ref[...], lam_init)


def _attn_sample(q3, kct, vc2, kn3, vn3, lams, sw_row, lam_init, hpb=4):
    b, l, _ = q3.shape
    past = kct.shape[3]
    vec = lambda bi, h: (0, 0)
    blk = lambda rows: pl.BlockSpec((1, rows, hpb * LANES), lambda bi, h: (bi, 0, h))
    kern = functools.partial(_attn_sample_kernel, lam_init=lam_init, hpb=hpb)
    return pl.pallas_call(
        kern,
        grid=(b, A_HEADS // hpb),
        in_specs=[
            blk(l),
            pl.BlockSpec((1, hpb, 2 * A_DIM, past), lambda bi, h: (bi, h, 0, 0)),
            pl.BlockSpec((1, past * A_HEADS, A_VDIM), lambda bi, h: (bi, 0, 0)),
            blk(l), blk(l),
            pl.BlockSpec((1, A_DIM), vec), pl.BlockSpec((1, A_DIM), vec),
            pl.BlockSpec((1, A_DIM), vec), pl.BlockSpec((1, A_DIM), vec),
            pl.BlockSpec((1, A_VDIM), vec),
        ],
        out_specs=blk(l),
        out_shape=jax.ShapeDtypeStruct((b, l, A_V), BF16),
        compiler_params=_cparams(2),
        name="attn_sample",
    )(q3, kct, vc2, kn3, vn3, *lams, sw_row)


def _delta_kernel(q_ref, k_ref, v_ref, bg_ref, s0_ref, nw_ref, o_ref, sout_ref, s_scr,
                  *, c, nblk, hb, gb):
    t = pl.program_id(2)

    @pl.when(t == 0)
    def _():
        s_scr[...] = s0_ref[0]

    bg = bg_ref[0]
    lane = lax.broadcasted_iota(jnp.int32, bg.shape, 1)
    bg_t = bg.T
    sub = lax.broadcasted_iota(jnp.int32, bg_t.shape, 0)
    ri = lax.broadcasted_iota(jnp.int32, (c, c), 0)
    ci = lax.broadcasted_iota(jnp.int32, (c, c), 1)
    causal = ri >= ci
    strict = ri > ci
    ltri = causal.astype(BF16)
    utri = (ri <= ci).astype(BF16)
    ltri2 = jnp.concatenate([ltri, ltri], axis=1)
    utri2 = jnp.concatenate([utri, utri], axis=0)
    eye = jnp.where(ri == ci, 1.0, 0.0)

    gates = []
    for hh in range(hb):
        h = pl.program_id(1) * hb + hh
        gates.append((
            jnp.sum(jnp.where(lane == h, bg, 0.0), axis=-1, keepdims=True),
            jnp.sum(jnp.where(lane == h + B_HEADS, bg, 0.0), axis=-1, keepdims=True),
            jnp.sum(jnp.where(sub == h + B_HEADS, bg_t, 0.0), axis=0, keepdims=True)))
    pre = {}
    states = [s_scr[hh] for hh in range(hb)]

    def prepare(blocks):
        items = [(hh, blk) for blk in blocks for hh in range(hb)]
        for hh, blk in items:
            rows = slice(blk * c, (blk + 1) * c)
            cols = slice(hh * LANES, (hh + 1) * LANES)
            bc = gates[hh][0][rows, :]
            q = q_ref[0, rows, cols]
            k = k_ref[0, rows, cols]
            v = v_ref[0, rows, cols]
            ghi, glo = _split(jnp.broadcast_to(gates[hh][1][rows, :], (c, LANES)))
            g_cum = _dot(ltri2, jnp.concatenate([ghi, glo], axis=0))
            rhi, rlo = _split(gates[hh][2][:, rows])
            g_rows = jnp.broadcast_to(jnp.concatenate([rhi, rlo], axis=1), (c, 2 * c))
            g_cum_row = _dot(g_rows, utri2)
            diff = jnp.where(causal, g_cum[:, :c] - g_cum_row, 0.0)
            decay = jnp.where(causal, jnp.exp(diff), 0.0)
            kb = k.astype(BF16)
            kq = _dot_nt(jnp.concatenate([kb, q.astype(BF16)], axis=0), kb)
            m = jnp.where(strict, bc * kq[:c] * decay, 0.0)
            pre[hh, blk] = dict(q=q, k=k, v=v, bc=bc, g_cum=g_cum, decay=decay, qk=kq[c:])
            pre[hh, blk]["pq"] = (eye - m, -m)
        yield
        for it in items:
            pm, qm = pre[it]["pq"]
            qb = qm.astype(BF16)
            pre[it]["pq"] = (pm, _dot(qb, qb))
        yield
        for _ in range(int(math.log2(c)) - 2):
            for it in items:
                pm, qm = pre[it]["pq"]
                qb = qm.astype(BF16)
                both = _dot(jnp.concatenate([pm.astype(BF16), qb], axis=0), qb)
                pre[it]["pq"] = (pm + both[:c], both[c:])
            yield
        for it in items:
            pm, qm = pre[it]["pq"]
            pre[it]["pq"] = (pm + _dot(pm.astype(BF16), qm.astype(BF16)), qm)
        yield
        for it in items:
            d = pre[it]
            e_g = jnp.exp(d["g_cum"])
            rhs = jnp.concatenate([d["v"] * d["bc"], d["k"] * (d["bc"] * e_g)], axis=1)
            sol = _dot(d.pop("pq")[0].astype(BF16), rhs.astype(BF16))
            g_last = d["g_cum"][c - 1:c, :]
            a_loc = jnp.where(causal, d["qk"] * d["decay"], 0.0).astype(BF16)
            k_tail_t = (d["k"] * jnp.exp(g_last - d["g_cum"])).T.astype(BF16)
            d.update(
                u=sol[:, :B_DV],
                state_lhs=jnp.concatenate(
                    [sol[:, B_DV:].astype(BF16), (d["q"] * e_g).astype(BF16)], axis=0),
                value_lhs=jnp.concatenate([a_loc, k_tail_t], axis=0),
                g_tail=jnp.exp(g_last))
        yield

    def scan(blocks):
        for blk in blocks:
            vbs, o_state = [], []
            for hh in range(hb):
                d = pre[hh, blk]
                ws = _dot(d["state_lhs"], states[hh].astype(BF16))
                vbs.append((d["u"] - ws[:c]).astype(BF16))
                o_state.append(ws[c:])
            yield
            for hh in range(hb):
                d = pre.pop((hh, blk))
                av = _dot(d["value_lhs"], vbs[hh])
                o = o_state[hh] + av[:c]
                states[hh] = states[hh] * d["g_tail"] + av[c:]
                o_ref[0, blk * c:(blk + 1) * c, hh * LANES:(hh + 1) * LANES] = (
                    _rms_rows(o, nw_ref[...]).astype(BF16))
            yield

    def run(*stage_iters):
        live = list(stage_iters)
        while live:
            live = [it for it in live if next(it, _DONE) is not _DONE]

    groups = [list(range(g, min(g + gb, nblk))) for g in range(0, nblk, gb)]
    run(prepare(groups[0]))
    for g in range(1, len(groups)):
        run(prepare(groups[g]), scan(groups[g - 1]))
    run(scan(groups[-1]))
    for hh in range(hb):
        s_scr[hh] = states[hh]

    @pl.when(t == pl.num_programs(2) - 1)
    def _():
        for hh in range(hb):
            sout_ref[0, hh] = states[hh]


def _delta(dqkv3, bg3, s0, nw_row, c, tl, hb, gb):
    b, l, _ = dqkv3.shape
    n_hg = B_HEADS // hb
    kern = functools.partial(_delta_kernel, c=c, nblk=tl // c, hb=hb, gb=gb)
    lane_blk = lambda off: pl.BlockSpec((1, tl, hb * LANES), lambda bi, h, t: (bi, t, h + off))
    state = pl.BlockSpec((1, hb, B_DK, B_DV), lambda bi, h, t: (bi, h, 0, 0))
    return pl.pallas_call(
        kern,
        grid=(b, n_hg, l // tl),
        in_specs=[
            lane_blk(0), lane_blk(n_hg), lane_blk(2 * n_hg),
            pl.BlockSpec((1, tl, LANES), lambda bi, h, t: (bi, t, 0)),
            state,
            pl.BlockSpec((1, B_DV), lambda bi, h, t: (0, 0)),
        ],
        out_specs=[lane_blk(0), state],
        out_shape=[
            jax.ShapeDtypeStruct((b, l, B_V), BF16),
            jax.ShapeDtypeStruct((b, B_HEADS, B_DK, B_DV), F32),
        ],
        scratch_shapes=[pltpu.VMEM((hb, B_DK, B_DV), F32)],
        compiler_params=_cparams(3),
        name="delta_rule",
    )(dqkv3, dqkv3, dqkv3, bg3, s0, nw_row)


def _merge_kernel(x_ref, oa_ref, ob_ref, ln_ref, wdg, wga, wgb, wba, wbb, wout, y_ref):
    x = x_ref[...]
    h = _rms_rows(x, ln_ref[...]).astype(BF16)
    ob = ob_ref[...].astype(F32) * _silu(_dot(h, wdg[...]))
    branch_b = _dot(ob.astype(BF16), wbb[...])
    branch_a = _dot(oa_ref[...], wba[...])
    merged = (_sigmoid(_dot(h, wga[...])) * branch_a
              + _sigmoid(_dot(h, wgb[...])) * branch_b)
    y_ref[...] = x + _dot(merged.astype(BF16), wout[...])


def _merge(x2, oa2, ob2, ln_w, wdg, wga, wgb, wba, wbb, wout, tm):
    n = x2.shape[0]
    tile = pl.BlockSpec((tm, D_MODEL), lambda i: (i, 0))
    wspec = pl.BlockSpec((D_MODEL, D_MODEL), lambda i: (0, 0), pipeline_mode=pl.Buffered(1))
    return pl.pallas_call(
        _merge_kernel,
        grid=(n // tm,),
        in_specs=[tile, tile, tile, pl.BlockSpec((1, D_MODEL), lambda i: (0, 0))] + [wspec] * 6,
        out_specs=tile,
        out_shape=jax.ShapeDtypeStruct((n, D_MODEL), F32),
        compiler_params=_cparams(1),
        name="merge_outproj",
    )(x2, oa2, ob2, ln_w, wdg, wga, wgb, wba, wbb, wout)


def _ffn_kernel(x_ref, ln_ref, wgu_ref, wd_ref, y_ref, *, n_chunks):
    x = x_ref[...]
    h = _rms_rows(x, ln_ref[...]).astype(BF16)
    n_tiles = D_FF // MXU_DIM
    edges = [MXU_DIM * ((n_tiles * c + n_chunks - 1) // n_chunks) for c in range(n_chunks + 1)]
    acts = []
    for lo, hi in zip(edges[:-1], edges[1:]):
        gate = _dot(h, wgu_ref[:, lo:hi])
        up = _dot(h, wgu_ref[:, D_FF + lo:D_FF + hi])
        acts.append((_silu(gate) * up).astype(BF16))
    y = x
    for (lo, hi), act in zip(zip(edges[:-1], edges[1:]), acts):
        y = y + _dot(act, wd_ref[lo:hi, :])
    y_ref[...] = y


def _ffn(x2, ln_w, w_gate_up, w_down, tm, n_chunks):
    n = x2.shape[0]
    const = lambda i: (0, 0)
    return pl.pallas_call(
        functools.partial(_ffn_kernel, n_chunks=n_chunks),
        grid=(n // tm,),
        in_specs=[
            pl.BlockSpec((tm, D_MODEL), lambda i: (i, 0)),
            pl.BlockSpec((1, D_MODEL), const),
            pl.BlockSpec((D_MODEL, 2 * D_FF), const, pipeline_mode=pl.Buffered(1)),
            pl.BlockSpec((D_FF, D_MODEL), const, pipeline_mode=pl.Buffered(1)),
        ],
        out_specs=pl.BlockSpec((tm, D_MODEL), lambda i: (i, 0)),
        out_shape=jax.ShapeDtypeStruct((n, D_MODEL), F32),
        compiler_params=_cparams(1),
        name="swiglu_ffn",
    )(x2, ln_w, w_gate_up, w_down)


def _pick_tile(n, target):
    t = min(n, target)
    while n % t:
        t //= 2
    return t


def _prep_weights(ln1_w, w_in, q_norm_w, k_norm_w, lambda_q1, lambda_k1, lambda_q2, lambda_k2,
                  subln_w, w_branch_a, conv_w, a_log, dt_bias, delta_norm_w, w_branch_b,
                  w_out, ln2_w, w_gate_up, w_down):
    o_d = 2 * A_QK + A_V
    o_dg = o_d + CONV_CH
    o_bg = o_dg + B_V
    o_ga = o_bg + 2 * B_HEADS
    o_gb = o_ga + D_MODEL
    pad_lanes = lambda v: jnp.pad(v.astype(F32), (B_HEADS, LANES - 2 * B_HEADS))[None, :]
    return dict(
        ln1=ln1_w[None, :], ln2=ln2_w[None, :],
        w_qkv=w_in[:, :o_d].astype(BF16),
        w_d=w_in[:, o_d:o_dg].astype(BF16),
        w_dg=w_in[:, o_dg:o_bg].astype(BF16),
        w_bg=jnp.pad(w_in[:, o_bg:o_ga], ((0, 0), (0, LANES - 2 * B_HEADS))).astype(BF16),
        w_ga=w_in[:, o_ga:o_gb].astype(BF16),
        w_gb=w_in[:, o_gb:].astype(BF16),
        qn=jnp.tile(q_norm_w, A_QK // A_DIM)[None, :],
        kn=jnp.tile(k_norm_w, A_QK // A_DIM)[None, :],
        kn_col=jnp.tile(k_norm_w, A_QK // A_DIM)[:, None],
        w_kt=w_in[:, A_QK:2 * A_QK].T.astype(BF16),
        lams=tuple(v[None, :] for v in (lambda_q1, lambda_k1, lambda_q2, lambda_k2)),
        subln=subln_w[None, :], subln_col=subln_w[:, None],
        w_ba=w_branch_a.astype(BF16), w_bb=w_branch_b.astype(BF16),
        conv_w=conv_w, alog=pad_lanes(a_log), dtb=pad_lanes(dt_bias),
        dnorm=delta_norm_w[None, :],
        w_out=w_out.astype(BF16),
        w_gate_up=w_gate_up.astype(BF16), w_down=w_down.astype(BF16),
    )


def _layer(x, past_k, past_v, s0, conv_buf, lam_init, p):
    b, l, _ = x.shape
    n = b * l
    x2 = x.reshape(n, D_MODEL)

    prompt = past_k is None
    tm_a = _pick_tile(l, 1024) if prompt else _pick_tile(n, 1024)
    q2, k2, v2 = _inproj_attn(x2, p["ln1"], p["w_qkv"], p["w_kt"], p["qn"], p["kn"], p["kn_col"],
                              tm_a, l, prompt)

    tm_d = _pick_tile(l, 1024) if l >= 512 else l * _pick_tile(b, max(1, 512 // l))
    cbuf8 = jnp.pad(conv_buf, ((0, 0), (SUBLANES - (CONV_W - 1), 0), (0, 0)))
    dqkv, bg, cst = _inproj_delta(x2, p["ln1"], p["w_d"], p["w_bg"], p["conv_w"], cbuf8,
                                  p["alog"], p["dtb"], tm_d, l)
    tiles_per_seq = max(1, l // tm_d)
    new_conv = cst[tiles_per_seq - 1::tiles_per_seq, SUBLANES - (CONV_W - 1):, :]

    q3 = q2.reshape(b, l, A_QK)
    v3 = v2.reshape(b, l, A_V)
    if prompt:
        oa = _attn_prompt(q3, k2, v3, p["lams"], p["subln_col"], lam_init, _pick_tile(l, 128))
        k_out = jnp.transpose(k2.reshape(b, A_HEADS, 2, A_DIM, l), (0, 4, 1, 2, 3))
    else:
        past = past_k.shape[1]
        k3 = k2.reshape(b, l, A_QK)
        kct = jnp.transpose(past_k, (0, 2, 3, 4, 1)).reshape(b, A_HEADS, 2 * A_DIM, past)
        oa = _attn_sample(q3, kct, past_v.reshape(b, past * A_HEADS, A_VDIM), k3, v3, p["lams"],
                          p["subln"], lam_init)
        k_out = k3.reshape(b, l, A_HEADS, 2, A_DIM)

    c = DELTA_BLOCK if l % DELTA_BLOCK == 0 else l
    tl = c * max(1, min(8, l // c))
    ob, s_new = _delta(dqkv.reshape(b, l, CONV_CH), bg.reshape(b, l, LANES), s0, p["dnorm"], c, tl,
                       B_HEADS, 2)

    x1 = _merge(x2, oa.reshape(n, A_V), ob.reshape(n, B_V), p["ln1"], p["w_dg"], p["w_ga"],
                p["w_gb"], p["w_ba"], p["w_bb"], p["w_out"], _pick_tile(n, 512))
    y = _ffn(x1, p["ln2"], p["w_gate_up"], p["w_down"], _pick_tile(n, 512), 2)
    return (y.reshape(b, l, D_MODEL), k_out, v3.reshape(b, l, A_HEADS, A_VDIM), s_new, new_conv)


def kernel(x_prompt, x_sample, cache_k, cache_v, state_delta, state_conv, ln1_w, w_in, q_norm_w,
           k_norm_w, lambda_q1, lambda_k1, lambda_q2, lambda_k2, subln_w, w_branch_a, conv_w,
           a_log, dt_bias, delta_norm_w, w_branch_b, w_out, ln2_w, w_gate_up, w_down):
    depth = ln1_w.shape[0]
    xp, xs = x_prompt, x_sample
    outs_p, outs_s = [], []
    for layer in range(depth):
        lam_init = 0.8 - 0.6 * math.exp(-0.3 * layer)
        p = _prep_weights(*(w[layer] for w in (
            ln1_w, w_in, q_norm_w, k_norm_w, lambda_q1, lambda_k1, lambda_q2, lambda_k2, subln_w,
            w_branch_a, conv_w, a_log, dt_bias, delta_norm_w, w_branch_b, w_out, ln2_w,
            w_gate_up, w_down)))
        bp = xp.shape[0]
        zero_conv = jnp.zeros((bp, CONV_W - 1, CONV_CH), xp.dtype)
        zero_s = jnp.zeros((bp, B_HEADS, B_DK, B_DV), F32)
        xp, *rest_p = _layer(xp, None, None, zero_s, zero_conv, lam_init, p)
        xs, *rest_s = _layer(xs, cache_k[layer], cache_v[layer], state_delta[layer],
                             state_conv[layer], lam_init, p)
        outs_p.append(rest_p)
        outs_s.append(rest_s)
    stack = lambda outs, idx: jnp.stack([o[idx] for o in outs])
    return (xp, xs,
            stack(outs_p, 0), stack(outs_p, 1), stack(outs_p, 2), stack(outs_p, 3),
            stack(outs_s, 0), stack(outs_s, 1), stack(outs_s, 2), stack(outs_s, 3))
```
